```python
import jax, jax.numpy as jnp
from jax import lax
import numpy as np

D_MODEL = 1024
BATCH = 32
SEQ = 256
DEPTH = 1
DEC_BATCH = 2
DEC_SEQ = 1024
PAST_LEN = 256

GRID_W = 64
HEAD_DIM = 64
A_HEADS = 8
A_KV_HEADS = 2
A_WINDOW = 128
A_BLOCK = 128
B_HEADS = 8
NA_ROWS = 8
NA_COLS = 16
D_FF = -(-8 * D_MODEL // (3 * 256)) * 256
ROPE_BASE = 10000.0
RMS_EPS = 1e-6
NEG = -1e30

A_Q = A_HEADS * HEAD_DIM
A_KV = A_KV_HEADS * HEAD_DIM
B_W = B_HEADS * HEAD_DIM
D_IN = A_Q + 2 * A_KV + 3 * B_W + 2 * D_MODEL
SPLITS = [A_Q, A_Q + A_KV, A_Q + 2 * A_KV, A_Q + 2 * A_KV + B_W,
          A_Q + 2 * A_KV + 2 * B_W, A_Q + 2 * A_KV + 3 * B_W,
          A_Q + 2 * A_KV + 3 * B_W + D_MODEL]

kernel_name = "hybrid_dit_window_natten_prefix_step"


def rmsnorm(x, g):
    xf = x.astype(jnp.float32)
    y = xf * lax.rsqrt(jnp.mean(xf * xf, axis=-1, keepdims=True) + RMS_EPS)
    return (y * g.astype(jnp.float32)).astype(x.dtype)


def ada_mod(cvec, w_ada_l, b_ada_l):
    m = jax.nn.silu(cvec) @ w_ada_l + b_ada_l
    return jnp.split(m, 6, axis=-1)


def modulate(h, shift, scale):
    return h * (1 + scale) + shift


def rope_axis(x, pos):
    half = x.shape[-1] // 2
    inv_freq = 1.0 / (ROPE_BASE ** (jnp.arange(half, dtype=jnp.float32) / half))
    ang = pos.astype(jnp.float32)[:, None] * inv_freq[None, :]
    cos = jnp.cos(ang)[:, None, :]
    sin = jnp.sin(ang)[:, None, :]
    xf = x.astype(jnp.float32)
    x1, x2 = xf[..., :half], xf[..., half:]
    return jnp.concatenate([x1 * cos - x2 * sin, x2 * cos + x1 * sin], axis=-1).astype(x.dtype)


def rope_2d(x):
    n = x.shape[1]
    t = jnp.arange(n)
    d2 = HEAD_DIM // 2
    return jnp.concatenate([rope_axis(x[..., :d2], t // GRID_W), rope_axis(x[..., d2:], t % GRID_W)], axis=-1)


def project(h, w_in_l, qn_a, kn_a, qn_b, kn_b):
    u = h @ w_in_l
    qa, ka, va, qb, kb, vb, ga, gb = jnp.split(u, SPLITS, axis=-1)
    lead = h.shape[:-1]
    qa = rmsnorm(qa.reshape(*lead, A_HEADS, HEAD_DIM), qn_a)
    ka = rmsnorm(ka.reshape(*lead, A_KV_HEADS, HEAD_DIM), kn_a)
    va = va.reshape(*lead, A_KV_HEADS, HEAD_DIM)
    qb = rmsnorm(qb.reshape(*lead, B_HEADS, HEAD_DIM), qn_b)
    kb = rmsnorm(kb.reshape(*lead, B_HEADS, HEAD_DIM), kn_b)
    vb = vb.reshape(*lead, B_HEADS, HEAD_DIM)
    return qa, ka, va, qb, kb, vb, ga, gb


def merge_out(ya, yb, ga, gb, w_out_a_l, w_out_b_l, w_o_l):
    return (jax.nn.sigmoid(ga) * (ya @ w_out_a_l) + jax.nn.sigmoid(gb) * (yb @ w_out_b_l)) @ w_o_l


def swiglu(h, wg, wu, wd):
    return (jax.nn.silu(h @ wg) * (h @ wu)) @ wd


def ctx_attention(q, k, v, sink):
    b, l, h, d = q.shape
    kv = k.shape[2]
    g = h // kv
    m = k.shape[1]
    qg = q.reshape(b, l, kv, g, d)
    s = jnp.einsum('blkgd,bmkd->bkglm', qg, k).astype(jnp.float32) * (d ** -0.5)
    if sink is not None:
        s_sink = jnp.broadcast_to(sink.reshape(kv, g)[None, :, :, None, None].astype(jnp.float32), (b, kv, g, l, 1))
        s = jnp.concatenate([s, s_sink], axis=-1)
    p = jax.nn.softmax(s, axis=-1)[..., :m].astype(v.dtype)
    return jnp.einsum('bkglm,bmkd->blkgd', p, v).reshape(b, l, h * d)


def window_attention(q, k, v, ck, cv, sink):
    b, n, h, d = q.shape
    kv = k.shape[2]
    g = h // kv
    nb = n // A_BLOCK
    lc = ck.shape[1]
    qb = q.reshape(b, nb, A_BLOCK, kv, g, d)
    pad = ((0, 0), (A_BLOCK, A_BLOCK), (0, 0), (0, 0))
    kp = jnp.pad(k, pad).reshape(b, nb + 2, A_BLOCK, kv, d)
    vp = jnp.pad(v, pad).reshape(b, nb + 2, A_BLOCK, kv, d)
    kb = jnp.concatenate([kp[:, :nb], kp[:, 1:nb + 1], kp[:, 2:]], axis=2)
    vb = jnp.concatenate([vp[:, :nb], vp[:, 1:nb + 1], vp[:, 2:]], axis=2)
    qpos = jnp.arange(n).reshape(nb, A_BLOCK)
    kpos = (jnp.arange(nb)[:, None] - 1) * A_BLOCK + jnp.arange(3 * A_BLOCK)[None, :]
    kq = kpos[:, None, :]
    valid = (jnp.abs(qpos[:, :, None] - kq) <= A_WINDOW) & (kq >= 0) & (kq < n)
    scale = d ** -0.5
    s_loc = jnp.einsum('bnqkgd,bnskd->bnkgqs', qb, kb).astype(jnp.float32) * scale
    s_loc = jnp.where(valid[None, :, None, None], s_loc, NEG)
    s_ctx = jnp.einsum('bnqkgd,blkd->bnkgql', qb, ck).astype(jnp.float32) * scale
    s_sink = jnp.broadcast_to(sink.reshape(kv, g)[None, None, :, :, None, None].astype(jnp.float32), (b, nb, kv, g, A_BLOCK, 1))
    p = jax.nn.softmax(jnp.concatenate([s_loc, s_ctx, s_sink], axis=-1), axis=-1)
    p_loc = p[..., :3 * A_BLOCK].astype(v.dtype)
    p_ctx = p[..., 3 * A_BLOCK:3 * A_BLOCK + lc].astype(v.dtype)
    out = jnp.einsum('bnkgqs,bnskd->bnqkgd', p_loc, vb) + jnp.einsum('bnkgql,blkd->bnqkgd', p_ctx, cv)
    return out.reshape(b, n, h * d)


def neighbourhood_attention(q, k, v, ck, cv, rpb):
    b, n, h, d = q.shape
    rows_n = n // GRID_W
    wr = min(NA_ROWS, rows_n)
    s_len = wr * GRID_W
    rq = jnp.arange(rows_n)
    rs = jnp.clip(rq - wr // 2, 0, rows_n - wr)
    rows = rs[:, None] + jnp.arange(wr)[None, :]
    kg = k.reshape(b, rows_n, GRID_W, h, d)[:, rows].reshape(b, rows_n, s_len, h, d)
    vg = v.reshape(b, rows_n, GRID_W, h, d)[:, rows].reshape(b, rows_n, s_len, h, d)
    qg = q.reshape(b, rows_n, GRID_W, h, d)
    cq = jnp.arange(GRID_W)
    cs = jnp.clip(cq - NA_COLS // 2, 0, GRID_W - NA_COLS)
    kcol = jnp.arange(s_len) % GRID_W
    krow = jnp.repeat(rows, GRID_W, axis=1)
    col_valid = (kcol[None, :] >= cs[:, None]) & (kcol[None, :] < cs[:, None] + NA_COLS)
    dr = krow - rq[:, None] + (NA_ROWS - 1)
    dc = jnp.clip(kcol[None, :] - cq[:, None], -(NA_COLS - 1), NA_COLS - 1) + (NA_COLS - 1)
    bias = jnp.transpose(rpb[:, dr[:, None, :], dc[None, :, :]], (1, 0, 2, 3))
    scale = d ** -0.5
    s_loc = jnp.einsum('brqhd,brshd->brhqs', qg, kg).astype(jnp.float32) * scale + bias.astype(jnp.float32)[None]
    s_loc = jnp.where(col_valid[None, None, None], s_loc, NEG)
    s_ctx = jnp.einsum('brqhd,blhd->brhql', qg, ck).astype(jnp.float32) * scale
    p = jax.nn.softmax(jnp.concatenate([s_loc, s_ctx], axis=-1), axis=-1)
    p_loc = p[..., :s_len].astype(v.dtype)
    p_ctx = p[..., s_len:].astype(v.dtype)
    out = jnp.einsum('brhqs,brshd->brqhd', p_loc, vg) + jnp.einsum('brhql,blhd->brqhd', p_ctx, cv)
    return out.reshape(b, n, h * d)


def setup_inputs(seed: int = 0) -> dict:
    key = jax.random.key(seed)
    ks = jax.random.split(key, 26)
    f32 = jnp.float32
    nrm = lambda k, s: jax.random.normal(k, s, dtype=f32)
    return {
        "x_prompt": nrm(ks[0], (BATCH, SEQ, D_MODEL)),
        "x_sample": nrm(ks[1], (DEC_BATCH, DEC_SEQ, D_MODEL)),
        "cache_a_k": nrm(ks[2], (DEC_BATCH, DEPTH, PAST_LEN, A_KV_HEADS, HEAD_DIM)),
        "cache_a_v": nrm(ks[3], (DEC_BATCH, DEPTH, PAST_LEN, A_KV_HEADS, HEAD_DIM)),
        "cache_b_k": nrm(ks[4], (DEC_BATCH, DEPTH, PAST_LEN, B_HEADS, HEAD_DIM)),
        "cache_b_v": nrm(ks[5], (DEC_BATCH, DEPTH, PAST_LEN, B_HEADS, HEAD_DIM)),
        "c": nrm(ks[6], (DEC_BATCH, D_MODEL)),
        "c_ctx": nrm(ks[7], (D_MODEL,)),
        "w_ada": nrm(ks[8], (DEPTH, D_MODEL, 6 * D_MODEL)) * (0.5 * D_MODEL ** -0.5),
        "b_ada": nrm(ks[9], (DEPTH, 6 * D_MODEL)) * 0.02,
        "g_attn": 1.0 + 0.05 * nrm(ks[10], (DEPTH, D_MODEL)),
        "w_in": nrm(ks[11], (DEPTH, D_MODEL, D_IN)) * D_MODEL ** -0.5,
        "q_norm_a": 1.0 + 0.05 * nrm(ks[12], (DEPTH, HEAD_DIM)),
        "k_norm_a": 1.0 + 0.05 * nrm(ks[13], (DEPTH, HEAD_DIM)),
        "q_norm_b": 1.0 + 0.05 * nrm(ks[14], (DEPTH, HEAD_DIM)),
        "k_norm_b": 1.0 + 0.05 * nrm(ks[15], (DEPTH, HEAD_DIM)),
        "sink_a": nrm(ks[16], (DEPTH, A_HEADS)),
        "rpb_b": 0.1 * nrm(ks[17], (DEPTH, B_HEADS, 2 * NA_ROWS - 1, 2 * NA_COLS - 1)),
        "w_out_a": nrm(ks[18], (DEPTH, A_Q, D_MODEL)) * A_Q ** -0.5,
        "w_out_b": nrm(ks[19], (DEPTH, B_W, D_MODEL)) * B_W ** -0.5,
        "w_o": nrm(ks[20], (DEPTH, D_MODEL, D_MODEL)) * D_MODEL ** -0.5,
        "g_ffn": 1.0 + 0.05 * nrm(ks[21], (DEPTH, D_MODEL)),
        "w_gate": nrm(ks[22], (DEPTH, D_MODEL, D_FF)) * D_MODEL ** -0.5,
        "w_up": nrm(ks[23], (DEPTH, D_MODEL, D_FF)) * D_MODEL ** -0.5,
        "w_down": nrm(ks[24], (DEPTH, D_FF, D_MODEL)) * D_FF ** -0.5,
    }


def reference(x_prompt, x_sample, cache_a_k, cache_a_v, cache_b_k, cache_b_v, c, c_ctx,
              w_ada, b_ada, g_attn, w_in, q_norm_a, k_norm_a, q_norm_b, k_norm_b,
              sink_a, rpb_b, w_out_a, w_out_b, w_o, g_ffn, w_gate, w_up, w_down):
    xp = x_prompt
    xs = x_sample
    ak_list, av_list, bk_list, bv_list = [], [], [], []
    for l in range(DEPTH):
        sh1, sc1, gt1, sh2, sc2, gt2 = ada_mod(c_ctx, w_ada[l], b_ada[l])
        h = modulate(rmsnorm(xp, g_attn[l]), sh1, sc1)
        qa, ka, va, qb, kb, vb, ga, gb = project(h, w_in[l], q_norm_a[l], k_norm_a[l], q_norm_b[l], k_norm_b[l])
        ya = ctx_attention(qa, ka, va, sink_a[l])
        yb = ctx_attention(qb, kb, vb, None)
        xp = xp + gt1 * merge_out(ya, yb, ga, gb, w_out_a[l], w_out_b[l], w_o[l])
        xp = xp + gt2 * swiglu(modulate(rmsnorm(xp, g_ffn[l]), sh2, sc2), w_gate[l], w_up[l], w_down[l])
        ak_list.append(ka)
        av_list.append(va)
        bk_list.append(kb)
        bv_list.append(vb)

        sh1, sc1, gt1, sh2, sc2, gt2 = [m[:, None, :] for m in ada_mod(c, w_ada[l], b_ada[l])]
        h = modulate(rmsnorm(xs, g_attn[l]), sh1, sc1)
        qa, ka, va, qb, kb, vb, ga, gb = project(h, w_in[l], q_norm_a[l], k_norm_a[l], q_norm_b[l], k_norm_b[l])
        qa = rope_2d(qa)
        ka = rope_2d(ka)
        ya = window_attention(qa, ka, va, cache_a_k[:, l], cache_a_v[:, l], sink_a[l])
        yb = neighbourhood_attention(qb, kb, vb, cache_b_k[:, l], cache_b_v[:, l], rpb_b[l])
        xs = xs + gt1 * merge_out(ya, yb, ga, gb, w_out_a[l], w_out_b[l], w_o[l])
        xs = xs + gt2 * swiglu(modulate(rmsnorm(xs, g_ffn[l]), sh2, sc2), w_gate[l], w_up[l], w_down[l])
    new_a_k = jnp.stack(ak_list, axis=1)
    new_a_v = jnp.stack(av_list, axis=1)
    new_b_k = jnp.stack(bk_list, axis=1)
    new_b_v = jnp.stack(bv_list, axis=1)
    return (xp, xs, new_a_k, new_a_v, new_b_k, new_b_v)
```

```python
import functools

import jax
import jax.numpy as jnp
from jax import lax
from jax.experimental import pallas as pl
from jax.experimental.pallas import tpu as pltpu

D_MODEL = 1024
SEQ = 256
DEC_SEQ = 1024
PAST_LEN = 256
GRID_W = 64
HEAD_DIM = 64
A_HEADS = 8
A_KV_HEADS = 2
A_WINDOW = 128
B_HEADS = 8
NA_ROWS = 8
NA_COLS = 16
D_FF = 2816
ROPE_BASE = 10000.0
RMS_EPS = 1e-6
NEG = -1e30

A_Q = A_HEADS * HEAD_DIM
A_KV = A_KV_HEADS * HEAD_DIM
B_W = B_HEADS * HEAD_DIM
O_QA, O_KA, O_VA = 0, A_Q, A_Q + A_KV
O_QB = A_Q + 2 * A_KV
O_KB, O_VB = O_QB + B_W, O_QB + 2 * B_W
O_GA = O_QB + 3 * B_W
O_GB = O_GA + D_MODEL
D_IN = O_GB + D_MODEL

LANES = 128
TOK = 256
VMEM_LIMIT = 58 * 1024 * 1024

F32 = jnp.float32
BF16 = jnp.bfloat16


def _dot(a, b):
    return jnp.dot(a, b, preferred_element_type=F32)


def _dot_nt(a, b):
    return lax.dot_general(a, b, (((1,), (1,)), ((), ())), preferred_element_type=F32)


def _split_bf16(x):
    hi = x.astype(BF16)
    lo = (x - hi.astype(F32)).astype(BF16)
    return hi, lo


def _sigmoid(x):
    return 1.0 / (1.0 + jnp.exp(-x))


def _rms_mod(x, g, shift, scale):
    ms = jnp.mean(x * x, axis=-1, keepdims=True)
    return (x * lax.rsqrt(ms + RMS_EPS) * g) * (1.0 + scale) + shift


def _head_norm(u, w, bd):
    width = u.shape[1]
    uu = u * u
    hi, lo = _split_bf16(uu)
    step = min(width, bd.shape[0])
    parts = []
    for c in range(width // step):
        sl = slice(c * step, (c + 1) * step)
        b = bd[:step, :step]
        parts.append(_dot(hi[:, sl], b) + _dot(lo[:, sl], b))
    ss = parts[0] if len(parts) == 1 else jnp.concatenate(parts, axis=1)
    return u * lax.rsqrt(ss * (1.0 / HEAD_DIM) + RMS_EPS) * w


def _project(h, win_ref, nrm_ref, bd_ref):
    bd = bd_ref[...]
    qa = _head_norm(_dot(h, win_ref[:, O_QA:O_KA]), nrm_ref[0:1, :], bd)
    ka = _head_norm(_dot(h, win_ref[:, O_KA:O_VA]), nrm_ref[1:2, :A_KV], bd)
    va = _dot(h, win_ref[:, O_VA:O_QB])
    qb = _head_norm(_dot(h, win_ref[:, O_QB:O_KB]), nrm_ref[2:3, :], bd)
    kb = _head_norm(_dot(h, win_ref[:, O_KB:O_VB]), nrm_ref[3:4, :], bd)
    vb = _dot(h, win_ref[:, O_VB:O_GA])
    ga = _dot(h, win_ref[:, O_GA:O_GB])
    gb = _dot(h, win_ref[:, O_GB:D_IN])
    return qa, ka, va, qb, kb, vb, ga, gb


def _place_halves(x, lo_mask):
    h0_lo = jnp.where(lo_mask, x, 0.0)
    h1_hi = jnp.where(lo_mask, 0.0, x)
    return h0_lo, pltpu.roll(h0_lo, 64, 1), pltpu.roll(h1_hi, 64, 1), h1_hi


def _attend(q, k, v, sink=None, mask=None):
    s = _dot_nt(q, k)
    if mask is not None:
        s = jnp.where(mask, s, NEG)
    m = jnp.max(s, axis=-1, keepdims=True)
    if sink is not None:
        m = jnp.maximum(m, sink)
    p = jnp.exp(s - m)
    l = jnp.sum(p, axis=-1, keepdims=True)
    if sink is not None:
        l = l + jnp.exp(sink - m)
    return _dot(p.astype(BF16), v) * (1.0 / l)


def _tail(x, ya, yb, ga, gb, mod_ref, woa_ref, wob_ref, wo_ref, gf_ref, wg_ref, wu_ref, wd_ref):
    d = D_MODEL
    gt1, sh2, sc2, gt2 = (mod_ref[:, 2 * d:3 * d], mod_ref[:, 3 * d:4 * d],
                          mod_ref[:, 4 * d:5 * d], mod_ref[:, 5 * d:6 * d])
    ma = _dot(ya.astype(BF16), woa_ref[...])
    mb = _dot(yb.astype(BF16), wob_ref[...])
    mg = _sigmoid(ga) * ma + _sigmoid(gb) * mb
    x1 = x + gt1 * _dot(mg.astype(BF16), wo_ref[...])
    h2 = _rms_mod(x1, gf_ref[...], sh2, sc2).astype(BF16)
    gate = _dot(h2, wg_ref[...])
    up = _dot(h2, wu_ref[...])
    act = (gate * _sigmoid(gate)) * up
    return x1 + gt2 * _dot(act.astype(BF16), wd_ref[...])


def _ada_kernel(c_ref, w_ref, b_ref, o_ref):
    c = c_ref[...]
    s_hi, s_lo = _split_bf16(c * _sigmoid(c))
    w_hi, w_lo = _split_bf16(w_ref[...])
    o_ref[...] = _dot(s_hi, w_hi) + _dot(s_lo, w_hi) + _dot(s_hi, w_lo) + b_ref[...]


def _ada(cvec, w, b):
    n = w.shape[1]
    tn = 1536
    return pl.pallas_call(
        _ada_kernel,
        grid=(n // tn,),
        in_specs=[pl.BlockSpec((8, D_MODEL), lambda j: (0, 0)),
                  pl.BlockSpec((D_MODEL, tn), lambda j: (0, j)),
                  pl.BlockSpec((1, tn), lambda j: (0, j))],
        out_specs=pl.BlockSpec((8, tn), lambda j: (0, j)),
        out_shape=jax.ShapeDtypeStruct((8, n), F32),
        compiler_params=pltpu.CompilerParams(vmem_limit_bytes=VMEM_LIMIT),
        name="ada_mod",
    )(cvec, w, b)


def _ctx_kernel(sink_ref, x_ref, mod_ref, gattn_ref, win_ref, nrm_ref, bd_ref,
                woa_ref, wob_ref, wo_ref, gf_ref, wg_ref, wu_ref, wd_ref,
                y_ref, ka_ref, va_ref, kb_ref, vb_ref):
    d = D_MODEL
    x = x_ref[0]
    mod = mod_ref.at[0]
    h = _rms_mod(x, gattn_ref[...], mod[:, 0:d], mod[:, d:2 * d]).astype(BF16)
    qa, ka, va, qb, kb, vb, ga, gb = _project(h, win_ref, nrm_ref, bd_ref)
    ka_ref[0] = ka
    va_ref[0] = va
    kb_ref[0] = kb
    vb_ref[0] = vb

    lo_mask = lax.broadcasted_iota(jnp.int32, (TOK, LANES), 1) < HEAD_DIM
    scale = HEAD_DIM ** -0.5
    qa = (qa * scale).astype(BF16)
    qb = (qb * scale).astype(BF16)

    k_pl = [t.astype(BF16) for t in _place_halves(ka, lo_mask)]
    v_pl = [t.astype(BF16) for t in _place_halves(va, lo_mask)]
    ya = []
    for c in range(A_Q // LANES):
        j = c // 2
        qc = qa[:, c * LANES:(c + 1) * LANES]
        acc = None
        for half in range(2):
            o = _attend(qc, k_pl[2 * j + half], v_pl[2 * j + half], sink=sink_ref[2 * c + half])
            acc = o if acc is None else acc + o
        ya.append(acc)
    ya = jnp.concatenate(ya, axis=1)

    yb = []
    for c in range(B_W // LANES):
        sl = slice(c * LANES, (c + 1) * LANES)
        qc = qb[:, sl]
        acc = None
        for half in range(2):
            hm = lo_mask if half == 0 else jnp.logical_not(lo_mask)
            km = jnp.where(hm, kb[:, sl], 0.0).astype(BF16)
            vm = jnp.where(hm, vb[:, sl], 0.0).astype(BF16)
            o = _attend(qc, km, vm)
            acc = o if acc is None else acc + o
        yb.append(acc)
    yb = jnp.concatenate(yb, axis=1)

    y_ref[0] = _tail(x, ya, yb, ga, gb, mod, woa_ref, wob_ref, wo_ref, gf_ref, wg_ref, wu_ref, wd_ref)


def _const_spec(shape):
    nd = len(shape)
    return pl.BlockSpec(shape, lambda *_: (0,) * nd, pipeline_mode=pl.Buffered(1))


def _ctx_layer(x, mods, sink, gattn, win, nrm, bd, woa, wob, wo, gf, wg, wu, wd):
    nb = x.shape[0]
    tok_spec = lambda w: pl.BlockSpec((1, SEQ, w), lambda b: (b, 0, 0))
    in_specs = [
        pl.BlockSpec(memory_space=pltpu.SMEM),
        tok_spec(D_MODEL),
        pl.BlockSpec((1, 1, 6 * D_MODEL), lambda b: (0, 0, 0)),
        _const_spec(gattn.shape), _const_spec(win.shape), _const_spec(nrm.shape), _const_spec(bd.shape),
        _const_spec(woa.shape), _const_spec(wob.shape), _const_spec(wo.shape), _const_spec(gf.shape),
        _const_spec(wg.shape), _const_spec(wu.shape), _const_spec(wd.shape),
    ]
    out_specs = [tok_spec(D_MODEL), tok_spec(A_KV), tok_spec(A_KV), tok_spec(B_W), tok_spec(B_W)]
    out_shape = [jax.ShapeDtypeStruct((nb, SEQ, w), F32) for w in (D_MODEL, A_KV, A_KV, B_W, B_W)]
    return pl.pallas_call(
        _ctx_kernel,
        grid=(nb,),
        in_specs=in_specs,
        out_specs=out_specs,
        out_shape=out_shape,
        compiler_params=pltpu.CompilerParams(dimension_semantics=("arbitrary",), vmem_limit_bytes=VMEM_LIMIT),
        name="ctx_layer",
    )(sink, x, mods, gattn, win, nrm, bd, woa, wob, wo, gf, wg, wu, wd)


def _rope(x, cos, sin, bit16):
    swapped = jnp.where(bit16, pltpu.roll(x, 16, 1), pltpu.roll(x, LANES - 16, 1))
    return x * cos + swapped * sin


def _lat_proj_kernel(x_ref, mod_ref, gattn_ref, win_ref, nrm_ref, bd_ref, cos_ref, sin_ref,
                     qa_ref, ka_ref, va_ref, qb_ref, kb_ref, vb_ref, ga_ref, gb_ref):
    d = D_MODEL
    x = x_ref[0]
    mod = mod_ref.at[0]
    h = _rms_mod(x, gattn_ref[...], mod[:, 0:d], mod[:, d:2 * d]).astype(BF16)
    qa, ka, va, qb, kb, vb, ga, gb = _project(h, win_ref, nrm_ref, bd_ref)
    lane = lax.broadcasted_iota(jnp.int32, (TOK, LANES), 1)
    lo_mask = lane < HEAD_DIM
    bit16 = (lane & 16) != 0
    cos, sin = cos_ref[...], sin_ref[...]
    scale = HEAD_DIM ** -0.5
    for c in range(A_Q // LANES):
        sl = slice(c * LANES, (c + 1) * LANES)
        qa_ref[0, :, sl] = (_rope(qa[:, sl], cos, sin, bit16) * scale).astype(BF16)
    ka = _rope(ka, cos, sin, bit16)
    for i, t in enumerate(_place_halves(ka, lo_mask)):
        ka_ref[0, :, i * LANES:(i + 1) * LANES] = t.astype(BF16)
    for i, t in enumerate(_place_halves(va, lo_mask)):
        va_ref[0, :, i * LANES:(i + 1) * LANES] = t.astype(BF16)
    qb_ref[0] = (qb * scale).astype(BF16)
    kb_ref[0] = kb.astype(BF16)
    vb_ref[0] = vb.astype(BF16)
    ga_ref[0] = ga
    gb_ref[0] = gb


def _lat_proj(x, mods, gattn, win, nrm, bd, cos, sin):
    nb, n, _ = x.shape
    nt = n // TOK
    tok_spec = lambda w: pl.BlockSpec((1, TOK, w), lambda b, t: (b, t, 0))
    in_specs = [
        tok_spec(D_MODEL),
        pl.BlockSpec((1, 1, 6 * D_MODEL), lambda b, t: (1 + b, 0, 0)),
        _const_spec(gattn.shape), _const_spec(win.shape), _const_spec(nrm.shape), _const_spec(bd.shape),
        pl.BlockSpec((TOK, LANES), lambda b, t: (t, 0)),
        pl.BlockSpec((TOK, LANES), lambda b, t: (t, 0)),
    ]
    widths = (A_Q, 4 * LANES, 4 * LANES, B_W, B_W, B_W, D_MODEL, D_MODEL)
    dtypes = (BF16,) * 6 + (F32, F32)
    return pl.pallas_call(
        _lat_proj_kernel,
        grid=(nb, nt),
        in_specs=in_specs,
        out_specs=[tok_spec(w) for w in widths],
        out_shape=[jax.ShapeDtypeStruct((nb, n, w), dt) for w, dt in zip(widths, dtypes)],
        compiler_params=pltpu.CompilerParams(dimension_semantics=("arbitrary", "arbitrary"),
                                             vmem_limit_bytes=VMEM_LIMIT),
        name="lat_proj",
    )(x, mods, gattn, win, nrm, bd, cos, sin)


def _bias_kernel(val_ref, o_ref):
    w = lax.broadcasted_iota(jnp.int32, (GRID_W, LANES), 0)
    kcol = lax.broadcasted_iota(jnp.int32, (GRID_W, LANES), 1) & (GRID_W - 1)
    dcol = kcol - w + (NA_COLS - 1)
    cs = jnp.clip(w - NA_COLS // 2, 0, GRID_W - NA_COLS)
    valid = (kcol >= cs) & (kcol < cs + NA_COLS)
    for d in range(2 * NA_ROWS - 1):
        t = jnp.zeros((GRID_W, LANES), F32)
        for dc in range(2 * NA_COLS - 1):
            t = jnp.where(dcol == dc, val_ref[0, d, dc:dc + 1, :], t)
        o_ref[0, d] = jnp.where(valid, t, NEG)


def _na_bias(val):
    nh, nd, ndc, _ = val.shape
    return pl.pallas_call(
        _bias_kernel,
        grid=(nh,),
        in_specs=[pl.BlockSpec((1, nd, ndc, LANES), lambda h: (h, 0, 0, 0))],
        out_specs=pl.BlockSpec((1, nd, GRID_W, LANES), lambda h: (h, 0, 0, 0)),
        out_shape=jax.ShapeDtypeStruct((nh, nd, GRID_W, LANES), F32),
        name="na_bias",
    )(val)


A_KEYS = 2 * TOK
B_KROWS = 12
B_KEYS = B_KROWS * GRID_W


def _lat_attn_kernel(sink_ref, x_ref, mod_ref, qa_ref, ka_ref, va_ref, qb_ref, kb_ref, vb_ref,
                     ga_ref, gb_ref, cka_ref, cva_ref, ckb_ref, cvb_ref, bt_ref,
                     woa_ref, wob_ref, wo_ref, gf_ref, wg_ref, wu_ref, wd_ref, y_ref):
    t = pl.program_id(1)
    x = x_ref[0]
    mod = mod_ref.at[0]
    lane = lax.broadcasted_iota(jnp.int32, (PAST_LEN, LANES), 1)
    lo_mask = lane < HEAD_DIM

    ks = pl.multiple_of(jnp.clip(TOK * t - A_WINDOW, 0, DEC_SEQ - A_KEYS), LANES)
    qpos = TOK * t + lax.broadcasted_iota(jnp.int32, (TOK, A_KEYS + PAST_LEN), 0)
    kidx = lax.broadcasted_iota(jnp.int32, (TOK, A_KEYS + PAST_LEN), 1)
    a_valid = (jnp.abs(qpos - (ks + kidx)) <= A_WINDOW) | (kidx >= A_KEYS)
    ck_pl = [u.astype(BF16) for u in _place_halves(cka_ref[0], lo_mask)]
    cv_pl = [u.astype(BF16) for u in _place_halves(cva_ref[0], lo_mask)]
    ya = []
    for c in range(A_Q // LANES):
        j = c // 2
        qc = qa_ref[0, :, c * LANES:(c + 1) * LANES]
        acc = None
        for half in range(2):
            i = 2 * j + half
            kk = jnp.concatenate([ka_ref[0, pl.ds(ks, A_KEYS), i * LANES:(i + 1) * LANES], ck_pl[i]], axis=0)
            vv = jnp.concatenate([va_ref[0, pl.ds(ks, A_KEYS), i * LANES:(i + 1) * LANES], cv_pl[i]], axis=0)
            o = _attend(qc, kk, vv, sink=sink_ref[2 * c + half], mask=a_valid)
            acc = o if acc is None else acc + o
        ya.append(acc)
    ya = jnp.concatenate(ya, axis=1)

    kr0 = jnp.clip(4 * t - 4, 0, DEC_SEQ // GRID_W - B_KROWS)
    kst = pl.multiple_of(kr0 * GRID_W, TOK)
    qrow = 4 * t + (lax.broadcasted_iota(jnp.int32, (TOK, B_KEYS), 0) >> 6)
    krow = kr0 + (lax.broadcasted_iota(jnp.int32, (TOK, B_KEYS), 1) >> 6)
    rs = jnp.clip(qrow - NA_ROWS // 2, 0, DEC_SEQ // GRID_W - NA_ROWS)
    row_valid = (krow >= rs) & (krow < rs + NA_ROWS)
    lo_k = lax.broadcasted_iota(jnp.int32, (B_KEYS, LANES), 1) < HEAD_DIM
    yb = []
    for c in range(B_W // LANES):
        sl = slice(c * LANES, (c + 1) * LANES)
        qc = qb_ref[0, :, sl]
        k_loc = kb_ref[0, pl.ds(kst, B_KEYS), sl]
        v_loc = vb_ref[0, pl.ds(kst, B_KEYS), sl]
        k_ctx = ckb_ref[0, :, sl]
        v_ctx = cvb_ref[0, :, sl]
        acc = None
        for half in range(2):
            head = 2 * c + half
            hm_k = lo_k if half == 0 else jnp.logical_not(lo_k)
            hm_c = lo_mask if half == 0 else jnp.logical_not(lo_mask)
            zero = jnp.zeros((), BF16)
            kk = jnp.where(hm_k, k_loc, zero)
            vv = jnp.where(hm_k, v_loc, zero)
            kc = jnp.where(hm_c, k_ctx, 0.0).astype(BF16)
            vc = jnp.where(hm_c, v_ctx, 0.0).astype(BF16)
            bias_rows = []
            for qi in range(TOK // GRID_W):
                blocks = []
                for p in range(B_KROWS // 2):
                    dr = jnp.clip(kr0 + 2 * p - (4 * t + qi) + (NA_ROWS - 1), 0, 2 * NA_ROWS - 2)
                    blocks.append(bt_ref[head, dr])
                bias_rows.append(jnp.concatenate(blocks, axis=1))
            bias = jnp.concatenate(bias_rows, axis=0)
            s_loc = jnp.where(row_valid, _dot_nt(qc, kk) + bias, NEG)
            s_ctx = _dot_nt(qc, kc)
            m = jnp.maximum(jnp.max(s_loc, axis=-1, keepdims=True), jnp.max(s_ctx, axis=-1, keepdims=True))
            p_loc = jnp.exp(s_loc - m)
            p_ctx = jnp.exp(s_ctx - m)
            l = jnp.sum(p_loc, axis=-1, keepdims=True) + jnp.sum(p_ctx, axis=-1, keepdims=True)
            o = (_dot(p_loc.astype(BF16), vv) + _dot(p_ctx.astype(BF16), vc)) * (1.0 / l)
            acc = o if acc is None else acc + o
        yb.append(acc)
    yb = jnp.concatenate(yb, axis=1)

    y_ref[0] = _tail(x, ya, yb, ga_ref[0], gb_ref[0], mod, woa_ref, wob_ref, wo_ref, gf_ref,
                     wg_ref, wu_ref, wd_ref)


def _lat_attn(sink, x, mods, qa, ka4, va4, qb, kb, vb, ga, gb, cka, cva, ckb, cvb, bt,
              woa, wob, wo, gf, wg, wu, wd):
    nb, n, _ = x.shape
    nt = n // TOK
    tok_spec = lambda w: pl.BlockSpec((1, TOK, w), lambda b, t: (b, t, 0))
    seq_spec = lambda rows, w: pl.BlockSpec((1, rows, w), lambda b, t: (b, 0, 0),
                                            pipeline_mode=pl.Buffered(1))
    in_specs = [
        pl.BlockSpec(memory_space=pltpu.SMEM),
        tok_spec(D_MODEL),
        pl.BlockSpec((1, 1, 6 * D_MODEL), lambda b, t: (1 + b, 0, 0)),
        tok_spec(A_Q), seq_spec(n, 4 * LANES), seq_spec(n, 4 * LANES),
        tok_spec(B_W), seq_spec(n, B_W), seq_spec(n, B_W),
        tok_spec(D_MODEL), tok_spec(D_MODEL),
        seq_spec(PAST_LEN, A_KV), seq_spec(PAST_LEN, A_KV), seq_spec(PAST_LEN, B_W), seq_spec(PAST_LEN, B_W),
        _const_spec(bt.shape),
        _const_spec(woa.shape), _const_spec(wob.shape), _const_spec(wo.shape), _const_spec(gf.shape),
        _const_spec(wg.shape), _const_spec(wu.shape), _const_spec(wd.shape),
    ]
    return pl.pallas_call(
        _lat_attn_kernel,
        grid=(nb, nt),
        in_specs=in_specs,
        out_specs=tok_spec(D_MODEL),
        out_shape=jax.ShapeDtypeStruct((nb, n, D_MODEL), F32),
        compiler_params=pltpu.CompilerParams(dimension_semantics=("arbitrary", "arbitrary"),
                                             vmem_limit_bytes=VMEM_LIMIT),
        name="lat_attn",
    )(sink, x, mods, qa, ka4, va4, qb, kb, vb, ga, gb, cka, cva, ckb, cvb, bt,
      woa, wob, wo, gf, wg, wu, wd)


def _rope_tables(n):
    half = HEAD_DIM // 4
    inv_freq = 1.0 / (ROPE_BASE ** (jnp.arange(half, dtype=F32) / half))
    t = jnp.arange(n)
    parts_c, parts_s = [], []
    for pos in (t // GRID_W, t % GRID_W):
        ang = pos.astype(F32)[:, None] * inv_freq[None, :]
        c, s = jnp.cos(ang), jnp.sin(ang)
        parts_c += [c, c]
        parts_s += [-s, s]
    cos = jnp.concatenate(parts_c * 2, axis=1)
    sin = jnp.concatenate(parts_s * 2, axis=1)
    return cos, sin


def kernel(x_prompt, x_sample, cache_a_k, cache_a_v, cache_b_k, cache_b_v, c, c_ctx, w_ada, b_ada, g_attn, w_in, q_norm_a, k_norm_a, q_norm_b, k_norm_b, sink_a, rpb_b, w_out_a, w_out_b, w_o, g_ffn, w_gate, w_up, w_down):
    nb, seq, _ = x_prompt.shape
    nd, n, _ = x_sample.shape
    l = 0

    cvec = jnp.concatenate([c_ctx[None, :], c, jnp.zeros((8 - 1 - nd, D_MODEL), F32)], axis=0)
    mods = _ada(cvec, w_ada[l], b_ada[l][None, :]).reshape(8, 1, 6 * D_MODEL)

    win = w_in[l].astype(BF16)
    woa, wob, wo = w_out_a[l].astype(BF16), w_out_b[l].astype(BF16), w_o[l].astype(BF16)
    wg, wu, wd = w_gate[l].astype(BF16), w_up[l].astype(BF16), w_down[l].astype(BF16)
    gattn, gf = g_attn[l][None, :], g_ffn[l][None, :]
    nrm = jnp.stack([jnp.tile(v[l], A_Q // HEAD_DIM) for v in (q_norm_a, k_norm_a, q_norm_b, k_norm_b)])
    ii = jnp.arange(2 * LANES) // HEAD_DIM
    bd = (ii[:, None] == ii[None, :]).astype(BF16)
    sink = sink_a[l]

    y_prompt, ka, va, kb, vb = _ctx_layer(x_prompt, mods, sink, gattn, win, nrm, bd,
                                          woa, wob, wo, gf, wg, wu, wd)
    new_a_k = ka.reshape(nb, 1, seq, A_KV_HEADS, HEAD_DIM)
    new_a_v = va.reshape(nb, 1, seq, A_KV_HEADS, HEAD_DIM)
    new_b_k = kb.reshape(nb, 1, seq, B_HEADS, HEAD_DIM)
    new_b_v = vb.reshape(nb, 1, seq, B_HEADS, HEAD_DIM)

    cos, sin = _rope_tables(n)
    qa, ka4, va4, qb, lkb, lvb, ga, gb = _lat_proj(x_sample, mods, gattn, win, nrm, bd, cos, sin)

    rp = jnp.concatenate([rpb_b[l], jnp.zeros((B_HEADS, 1, 2 * NA_COLS - 1), F32)], axis=1)
    pair = jnp.stack([rp[:, :-1], rp[:, 1:]], axis=-1)
    val = jnp.repeat(pair, HEAD_DIM, axis=-1)
    bt = _na_bias(val)

    cka = cache_a_k[:, l].reshape(nd, PAST_LEN, A_KV)
    cva = cache_a_v[:, l].reshape(nd, PAST_LEN, A_KV)
    ckb = cache_b_k[:, l].reshape(nd, PAST_LEN, B_W)
    cvb = cache_b_v[:, l].reshape(nd, PAST_LEN, B_W)
    y_sample = _lat_attn(sink, x_sample, mods, qa, ka4, va4, qb, lkb, lvb, ga, gb,
                         cka, cva, ckb, cvb, bt, woa, wob, wo, gf, wg, wu, wd)
    return (y_prompt, y_sample, new_a_k, new_a_v, new_b_k, new_b_v)
```

```python
import numpy as np

import jax
import jax.numpy as jnp
from jax import lax
from jax.experimental import pallas as pl
from jax.experimental.pallas import tpu as pltpu

D_MODEL = 1024
SEQ = 256
DEC_SEQ = 1024
PAST_LEN = 256
GRID_W = 64
HEAD_DIM = 64
A_HEADS = 8
A_KV_HEADS = 2
A_WINDOW = 128
B_HEADS = 8
NA_ROWS = 8
NA_COLS = 16
D_FF = 2816
ROPE_BASE = 10000.0
RMS_EPS = 1e-6
NEG = -1e30

A_Q = A_HEADS * HEAD_DIM
A_KV = A_KV_HEADS * HEAD_DIM
B_W = B_HEADS * HEAD_DIM
O_QA, O_KA, O_VA = 0, A_Q, A_Q + A_KV
O_QB = A_Q + 2 * A_KV
O_KB, O_VB = O_QB + B_W, O_QB + 2 * B_W
O_GA = O_QB + 3 * B_W
O_GB = O_GA + D_MODEL
D_IN = O_GB + D_MODEL

LANES = 128
TOK = 256
VMEM_LIMIT = 58 * 1024 * 1024
N_MOD_ROWS = 8

F32 = jnp.float32
BF16 = jnp.bfloat16


def _dot(a, b):
    return jnp.dot(a, b, preferred_element_type=F32)


def _dot_nt(a, b):
    return lax.dot_general(a, b, (((1,), (1,)), ((), ())), preferred_element_type=F32)


def _split_bf16(x):
    hi = x.astype(BF16)
    lo = (x - hi.astype(F32)).astype(BF16)
    return hi, lo


def _sigmoid(x):
    return 1.0 / (1.0 + jnp.exp(-x))


def _rms_mod(x, g, shift, scale):
    ms = jnp.mean(x * x, axis=-1, keepdims=True)
    return (x * lax.rsqrt(ms + RMS_EPS) * g) * (1.0 + scale) + shift


def _head_norm(u, w, bd):
    width = u.shape[1]
    uu = u * u
    hi, lo = _split_bf16(uu)
    step = min(width, bd.shape[0])
    parts = []
    for c in range(width // step):
        sl = slice(c * step, (c + 1) * step)
        b = bd[:step, :step]
        parts.append(_dot(hi[:, sl], b) + _dot(lo[:, sl], b))
    ss = parts[0] if len(parts) == 1 else jnp.concatenate(parts, axis=1)
    return u * lax.rsqrt(ss * (1.0 / HEAD_DIM) + RMS_EPS) * w


def _project(h, win_ref, nrm_ref, bd_ref):
    bd = bd_ref[...]
    qa = _head_norm(_dot(h, win_ref[:, O_QA:O_KA]), nrm_ref[0:1, :], bd)
    ka = _head_norm(_dot(h, win_ref[:, O_KA:O_VA]), nrm_ref[1:2, :A_KV], bd)
    va = _dot(h, win_ref[:, O_VA:O_QB])
    qb = _head_norm(_dot(h, win_ref[:, O_QB:O_KB]), nrm_ref[2:3, :], bd)
    kb = _head_norm(_dot(h, win_ref[:, O_KB:O_VB]), nrm_ref[3:4, :], bd)
    vb = _dot(h, win_ref[:, O_VB:O_GA])
    ga = _dot(h, win_ref[:, O_GA:O_GB])
    gb = _dot(h, win_ref[:, O_GB:D_IN])
    return qa, ka, va, qb, kb, vb, ga, gb


def _place_halves(x, lo_mask):
    h0_lo = jnp.where(lo_mask, x, 0.0)
    h1_hi = jnp.where(lo_mask, 0.0, x)
    return h0_lo, pltpu.roll(h0_lo, 64, 1), pltpu.roll(h1_hi, 64, 1), h1_hi


def _place_rows(xt, head, half):
    rows = xt[head * HEAD_DIM:(head + 1) * HEAD_DIM]
    zero = jnp.zeros_like(rows)
    return jnp.concatenate([rows, zero] if half == 0 else [zero, rows], axis=0)


def _softmax_pv(parts, sink=None):
    m = None
    for s, _, _ in parts:
        mi = jnp.max(s, axis=-1, keepdims=True)
        m = mi if m is None else jnp.maximum(m, mi)
    if sink is not None:
        m = jnp.maximum(m, sink)
    l = None if sink is None else jnp.exp(sink - m)
    out = None
    for s, v, v_fm in parts:
        p = jnp.exp(s - m)
        li = jnp.sum(p, axis=-1, keepdims=True)
        l = li if l is None else l + li
        o = _dot_nt(p.astype(BF16), v) if v_fm else _dot(p.astype(BF16), v)
        out = o if out is None else out + o
    return out * (1.0 / l)


def _mod(mod_ref, row, i):
    return mod_ref[pl.ds(row, 1), i * D_MODEL:(i + 1) * D_MODEL]


def _tail(x, ya, yb, ga, gb, mod_ref, row, woa_ref, wob_ref, wo_ref, gf_ref, wg_ref, wu_ref, wd_ref):
    gt1, sh2, sc2, gt2 = (_mod(mod_ref, row, i) for i in (2, 3, 4, 5))
    ma = _dot(ya.astype(BF16), woa_ref[...])
    mb = _dot(yb.astype(BF16), wob_ref[...])
    mg = _sigmoid(ga) * ma + _sigmoid(gb) * mb
    x1 = x + gt1 * _dot(mg.astype(BF16), wo_ref[...])
    h2 = _rms_mod(x1, gf_ref[...], sh2, sc2).astype(BF16)
    gate = _dot(h2, wg_ref[...])
    up = _dot(h2, wu_ref[...])
    act = (gate * _sigmoid(gate)) * up
    return x1 + gt2 * _dot(act.astype(BF16), wd_ref[...])


def _ada_kernel(c_ref, w_ref, b_ref, o_ref):
    c = c_ref[...]
    s_hi, s_lo = _split_bf16(c * _sigmoid(c))
    w_hi, w_lo = _split_bf16(w_ref[...])
    o_ref[...] = _dot(s_hi, w_hi) + _dot(s_lo, w_hi) + _dot(s_hi, w_lo) + b_ref[...]


def _ada(cvec, w, b):
    n = w.shape[1]
    tn = 768
    return pl.pallas_call(
        _ada_kernel,
        grid=(n // tn,),
        in_specs=[pl.BlockSpec((N_MOD_ROWS, D_MODEL), lambda j: (0, 0)),
                  pl.BlockSpec((D_MODEL, tn), lambda j: (0, j)),
                  pl.BlockSpec((1, tn), lambda j: (0, j))],
        out_specs=pl.BlockSpec((N_MOD_ROWS, tn), lambda j: (0, j)),
        out_shape=jax.ShapeDtypeStruct((N_MOD_ROWS, n), F32),
        name="ada_mod",
    )(cvec, w, b)


W_CHUNK_ROWS = {"win": 64, "woa": 256, "wob": 256, "wo": 256, "wg": 128, "wu": 128, "wd": 256}
W_NAMES = tuple(W_CHUNK_ROWS)


def _load_cast(src, dst, stage, sem, rows):
    n = src.shape[0] // rows

    def copy(i, slot):
        return pltpu.make_async_copy(src.at[pl.ds(i * rows, rows)], stage.at[slot], sem.at[slot])

    copy(0, 0).start()

    def body(i, carry):
        slot = i & 1

        @pl.when(i + 1 < n)
        def _():
            copy(i + 1, 1 - slot).start()

        copy(i, slot).wait()
        off = pl.multiple_of(i * rows, rows)
        dst[pl.ds(off, rows), :] = stage[slot].astype(BF16)
        return carry

    lax.fori_loop(0, n, body, 0)


def _ctx_kernel(sink_ref, x_ref, mod_ref, gattn_ref, nrm_ref, bd_ref, gf_ref,
                win_hbm, woa_hbm, wob_hbm, wo_hbm, wg_hbm, wu_hbm, wd_hbm,
                y_ref, ka_ref, va_ref, kb_ref, vb_ref,
                win_out, woa_out, wob_out, wo_out, wg_out, wu_out, wd_out,
                win_ref, woa_ref, wob_ref, wo_ref, wg_ref, wu_ref, wd_ref, out_sem):
    srcs = (win_hbm, woa_hbm, wob_hbm, wo_hbm, wg_hbm, wu_hbm, wd_hbm)
    dsts = (win_ref, woa_ref, wob_ref, wo_ref, wg_ref, wu_ref, wd_ref)
    outs = (win_out, woa_out, wob_out, wo_out, wg_out, wu_out, wd_out)
    first = pl.program_id(0) == 0

    def out_copy(i):
        return pltpu.make_async_copy(dsts[i], outs[i], out_sem.at[i])

    @pl.when(first)
    def _():
        for i, name in enumerate(W_NAMES):
            rows = W_CHUNK_ROWS[name]
            cols = srcs[i].shape[1]

            def scoped(stage, sem, i=i, rows=rows):
                _load_cast(srcs[i], dsts[i], stage, sem, rows)

            pl.run_scoped(scoped, pltpu.VMEM((2, rows, cols), F32), pltpu.SemaphoreType.DMA((2,)))
            out_copy(i).start()

    x = x_ref[0]
    h = _rms_mod(x, gattn_ref[...], _mod(mod_ref, 0, 0), _mod(mod_ref, 0, 1)).astype(BF16)
    qa, ka, va, qb, kb, vb, ga, gb = _project(h, win_ref, nrm_ref, bd_ref)
    ka_ref[0] = ka.T
    va_ref[0] = va.T
    kb_ref[0] = kb.T
    vb_ref[0] = vb.T

    lo_mask = lax.broadcasted_iota(jnp.int32, (TOK, LANES), 1) < HEAD_DIM
    scale = HEAD_DIM ** -0.5
    qa = (qa * scale).astype(BF16)
    qb = (qb * scale).astype(BF16)

    k_pl = [t.astype(BF16) for t in _place_halves(ka, lo_mask)]
    v_pl = [t.astype(BF16) for t in _place_halves(va, lo_mask)]
    ya = []
    for c in range(A_Q // LANES):
        j = c // 2
        qc = qa[:, c * LANES:(c + 1) * LANES]
        acc = None
        for half in range(2):
            i = 2 * j + half
            o = _softmax_pv([(_dot_nt(qc, k_pl[i]), v_pl[i], False)], sink=sink_ref[2 * c + half])
            acc = o if acc is None else acc + o
        ya.append(acc)
    ya = jnp.concatenate(ya, axis=1)

    yb = []
    for c in range(B_W // LANES):
        sl = slice(c * LANES, (c + 1) * LANES)
        qc = qb[:, sl]
        acc = None
        for half in range(2):
            hm = lo_mask if half == 0 else jnp.logical_not(lo_mask)
            km = jnp.where(hm, kb[:, sl], 0.0).astype(BF16)
            vm = jnp.where(hm, vb[:, sl], 0.0).astype(BF16)
            o = _softmax_pv([(_dot_nt(qc, km), vm, False)])
            acc = o if acc is None else acc + o
        yb.append(acc)
    yb = jnp.concatenate(yb, axis=1)

    y_ref[0] = _tail(x, ya, yb, ga, gb, mod_ref, 0, woa_ref, wob_ref, wo_ref, gf_ref, wg_ref, wu_ref, wd_ref)

    @pl.when(first)
    def _():
        for i in range(len(W_NAMES)):
            out_copy(i).wait()


def _const_spec(shape):
    nd = len(shape)
    return pl.BlockSpec(shape, lambda *_: (0,) * nd, pipeline_mode=pl.Buffered(1))


def _ctx_layer(x, mods, sink, gattn, nrm, bd, gf, weights):
    nb = x.shape[0]
    tok_spec = lambda w: pl.BlockSpec((1, SEQ, w), lambda b: (b, 0, 0))
    fm_spec = lambda w: pl.BlockSpec((1, w, SEQ), lambda b: (b, 0, 0))
    any_spec = pl.BlockSpec(memory_space=pl.ANY)
    in_specs = [
        pl.BlockSpec(memory_space=pltpu.SMEM),
        tok_spec(D_MODEL),
        _const_spec(mods.shape), _const_spec(gattn.shape), _const_spec(nrm.shape), _const_spec(bd.shape),
        _const_spec(gf.shape),
    ] + [any_spec] * len(weights)
    out_specs = ([tok_spec(D_MODEL), fm_spec(A_KV), fm_spec(A_KV), fm_spec(B_W), fm_spec(B_W)]
                 + [any_spec] * len(weights))
    out_shape = ([jax.ShapeDtypeStruct((nb, SEQ, D_MODEL), F32)]
                 + [jax.ShapeDtypeStruct((nb, w, SEQ), F32) for w in (A_KV, A_KV, B_W, B_W)]
                 + [jax.ShapeDtypeStruct(w.shape, BF16) for w in weights])
    scratch = [pltpu.VMEM(w.shape, BF16) for w in weights] + [pltpu.SemaphoreType.DMA((len(weights),))]
    return pl.pallas_call(
        _ctx_kernel,
        grid=(nb,),
        in_specs=in_specs,
        out_specs=out_specs,
        out_shape=out_shape,
        scratch_shapes=scratch,
        compiler_params=pltpu.CompilerParams(dimension_semantics=("arbitrary",), vmem_limit_bytes=VMEM_LIMIT),
        name="ctx_layer",
    )(sink, x, mods, gattn, nrm, bd, gf, *weights)


def _rope(x, cos, sin, bit16):
    swapped = jnp.where(bit16, pltpu.roll(x, 16, 1), pltpu.roll(x, LANES - 16, 1))
    return x * cos + swapped * sin


def _lat_proj_kernel(x_ref, mod_ref, gattn_ref, win_ref, nrm_ref, bd_ref, cos_ref, sin_ref,
                     qa_ref, ka_ref, va_ref, qb_ref, kb_ref, vb_ref, ga_ref, gb_ref):
    row = 1 + pl.program_id(0)
    x = x_ref[0]
    h = _rms_mod(x, gattn_ref[...], _mod(mod_ref, row, 0), _mod(mod_ref, row, 1)).astype(BF16)
    qa, ka, va, qb, kb, vb, ga, gb = _project(h, win_ref, nrm_ref, bd_ref)
    lane = lax.broadcasted_iota(jnp.int32, (TOK, LANES), 1)
    lo_mask = lane < HEAD_DIM
    bit16 = (lane & 16) != 0
    cos, sin = cos_ref[...], sin_ref[...]
    scale = HEAD_DIM ** -0.5
    for c in range(A_Q // LANES):
        sl = slice(c * LANES, (c + 1) * LANES)
        qa_ref[0, :, sl] = (_rope(qa[:, sl], cos, sin, bit16) * scale).astype(BF16)
    ka = _rope(ka, cos, sin, bit16)
    for i, t in enumerate(_place_halves(ka, lo_mask)):
        ka_ref[0, :, i * LANES:(i + 1) * LANES] = t.astype(BF16)
    for i, t in enumerate(_place_halves(va, lo_mask)):
        va_ref[0, :, i * LANES:(i + 1) * LANES] = t.astype(BF16)
    qb_ref[0] = (qb * scale).astype(BF16)
    kb_ref[0] = kb.astype(BF16)
    vb_ref[0] = vb.astype(BF16)
    ga_ref[0] = ga
    gb_ref[0] = gb


def _lat_proj(x, mods, gattn, win, nrm, bd, cos, sin):
    nb, n, _ = x.shape
    nt = n // TOK
    tok_spec = lambda w: pl.BlockSpec((1, TOK, w), lambda b, t: (b, t, 0))
    in_specs = [
        tok_spec(D_MODEL),
        _const_spec(mods.shape), _const_spec(gattn.shape), _const_spec(win.shape), _const_spec(nrm.shape),
        _const_spec(bd.shape),
        pl.BlockSpec((TOK, LANES), lambda b, t: (t, 0)),
        pl.BlockSpec((TOK, LANES), lambda b, t: (t, 0)),
    ]
    widths = (A_Q, 4 * LANES, 4 * LANES, B_W, B_W, B_W, D_MODEL, D_MODEL)
    dtypes = (BF16,) * 6 + (F32, F32)
    return pl.pallas_call(
        _lat_proj_kernel,
        grid=(nb, nt),
        in_specs=in_specs,
        out_specs=[tok_spec(w) for w in widths],
        out_shape=[jax.ShapeDtypeStruct((nb, n, w), dt) for w, dt in zip(widths, dtypes)],
        compiler_params=pltpu.CompilerParams(dimension_semantics=("arbitrary", "arbitrary"),
                                             vmem_limit_bytes=VMEM_LIMIT),
        name="lat_proj",
    )(x, mods, gattn, win, nrm, bd, cos, sin)


def _bias_kernel(row_ref, o_ref):
    w = lax.broadcasted_iota(jnp.int32, (GRID_W, LANES), 0)
    kcol = lax.broadcasted_iota(jnp.int32, (GRID_W, LANES), 1) & (GRID_W - 1)
    cs = jnp.clip(w - NA_COLS // 2, 0, GRID_W - NA_COLS)
    valid = (kcol >= cs) & (kcol < cs + NA_COLS)
    for d in range(2 * NA_ROWS - 1):
        rows = jnp.broadcast_to(row_ref[0, d:d + 1, :], (GRID_W, LANES))
        t = pltpu.roll(rows, LANES - (NA_COLS - 1), 1, stride=1, stride_axis=0)
        o_ref[0, d] = jnp.where(valid, t, NEG)


def _na_bias(rows):
    nh, nd, _ = rows.shape
    return pl.pallas_call(
        _bias_kernel,
        grid=(nh,),
        in_specs=[pl.BlockSpec((1, nd, LANES), lambda h: (h, 0, 0))],
        out_specs=pl.BlockSpec((1, nd, GRID_W, LANES), lambda h: (h, 0, 0, 0)),
        out_shape=jax.ShapeDtypeStruct((nh, nd, GRID_W, LANES), F32),
        name="na_bias",
    )(rows)


A_KEYS = 2 * TOK
B_KROWS = 12
B_KEYS = B_KROWS * GRID_W


def _lat_attn_kernel(sink_ref, x_ref, mod_ref, qa_ref, ka_ref, va_ref, qb_ref, kb_ref, vb_ref,
                     ga_ref, gb_ref, cka_ref, cva_ref, ckb_ref, cvb_ref, bt_ref,
                     woa_ref, wob_ref, wo_ref, gf_ref, wg_ref, wu_ref, wd_ref, y_ref):
    t = pl.program_id(1)
    row = 1 + pl.program_id(0)
    x = x_ref[0]

    ks = pl.multiple_of(jnp.clip(TOK * t - A_WINDOW, 0, DEC_SEQ - A_KEYS), LANES)
    qpos = TOK * t + lax.broadcasted_iota(jnp.int32, (TOK, A_KEYS), 0)
    kpos = ks + lax.broadcasted_iota(jnp.int32, (TOK, A_KEYS), 1)
    a_valid = jnp.abs(qpos - kpos) <= A_WINDOW
    ck_t, cv_t = cka_ref[0], cva_ref[0]
    ya = []
    for c in range(A_Q // LANES):
        j = c // 2
        qc = qa_ref[0, :, c * LANES:(c + 1) * LANES]
        acc = None
        for half in range(2):
            i = 2 * j + half
            k_loc = ka_ref[0, pl.ds(ks, A_KEYS), i * LANES:(i + 1) * LANES]
            v_loc = va_ref[0, pl.ds(ks, A_KEYS), i * LANES:(i + 1) * LANES]
            s_loc = jnp.where(a_valid, _dot_nt(qc, k_loc), NEG)
            s_ctx = _dot(qc, _place_rows(ck_t, j, half).astype(BF16))
            o = _softmax_pv([(s_loc, v_loc, False), (s_ctx, _place_rows(cv_t, j, half).astype(BF16), True)],
                            sink=sink_ref[2 * c + half])
            acc = o if acc is None else acc + o
        ya.append(acc)
    ya = jnp.concatenate(ya, axis=1)

    kr0 = jnp.clip(4 * t - 4, 0, DEC_SEQ // GRID_W - B_KROWS)
    kst = pl.multiple_of(kr0 * GRID_W, TOK)
    qrow = 4 * t + (lax.broadcasted_iota(jnp.int32, (TOK, B_KEYS), 0) >> 6)
    krow = kr0 + (lax.broadcasted_iota(jnp.int32, (TOK, B_KEYS), 1) >> 6)
    rs = jnp.clip(qrow - NA_ROWS // 2, 0, DEC_SEQ // GRID_W - NA_ROWS)
    row_valid = (krow >= rs) & (krow < rs + NA_ROWS)
    lo_k = lax.broadcasted_iota(jnp.int32, (B_KEYS, LANES), 1) < HEAD_DIM
    lo_f = lax.broadcasted_iota(jnp.int32, (LANES, PAST_LEN), 0) < HEAD_DIM
    yb = []
    for c in range(B_W // LANES):
        sl = slice(c * LANES, (c + 1) * LANES)
        qc = qb_ref[0, :, sl]
        k_loc = kb_ref[0, pl.ds(kst, B_KEYS), sl]
        v_loc = vb_ref[0, pl.ds(kst, B_KEYS), sl]
        k_ctx = ckb_ref[0, sl, :]
        v_ctx = cvb_ref[0, sl, :]
        acc = None
        for half in range(2):
            head = 2 * c + half
            hm_k = lo_k if half == 0 else jnp.logical_not(lo_k)
            hm_f = lo_f if half == 0 else jnp.logical_not(lo_f)
            zero = jnp.zeros((), BF16)
            kk = jnp.where(hm_k, k_loc, zero)
            vv = jnp.where(hm_k, v_loc, zero)
            kc = jnp.where(hm_f, k_ctx, 0.0).astype(BF16)
            vc = jnp.where(hm_f, v_ctx, 0.0).astype(BF16)
            bias_rows = []
            for qi in range(TOK // GRID_W):
                blocks = []
                for p in range(B_KROWS // 2):
                    dr = jnp.clip(kr0 + 2 * p - (4 * t + qi) + (NA_ROWS - 1), 0, 2 * NA_ROWS - 2)
                    blocks.append(bt_ref[head, dr])
                bias_rows.append(jnp.concatenate(blocks, axis=1))
            bias = jnp.concatenate(bias_rows, axis=0)
            s_loc = jnp.where(row_valid, _dot_nt(qc, kk) + bias, NEG)
            s_ctx = _dot(qc, kc)
            o = _softmax_pv([(s_loc, vv, False), (s_ctx, vc, True)])
            acc = o if acc is None else acc + o
        yb.append(acc)
    yb = jnp.concatenate(yb, axis=1)

    y_ref[0] = _tail(x, ya, yb, ga_ref[0], gb_ref[0], mod_ref, row, woa_ref, wob_ref, wo_ref, gf_ref,
                     wg_ref, wu_ref, wd_ref)


def _lat_attn(sink, x, mods, qa, ka4, va4, qb, kb, vb, ga, gb, cka, cva, ckb, cvb, bt,
              woa, wob, wo, gf, wg, wu, wd):
    nb, n, _ = x.shape
    nt = n // TOK
    tok_spec = lambda w: pl.BlockSpec((1, TOK, w), lambda b, t: (b, t, 0))
    seq_spec = lambda rows, w: pl.BlockSpec((1, rows, w), lambda b, t: (b, 0, 0),
                                            pipeline_mode=pl.Buffered(1))
    in_specs = [
        pl.BlockSpec(memory_space=pltpu.SMEM),
        tok_spec(D_MODEL),
        _const_spec(mods.shape),
        tok_spec(A_Q), seq_spec(n, 4 * LANES), seq_spec(n, 4 * LANES),
        tok_spec(B_W), seq_spec(n, B_W), seq_spec(n, B_W),
        tok_spec(D_MODEL), tok_spec(D_MODEL),
        seq_spec(A_KV, PAST_LEN), seq_spec(A_KV, PAST_LEN), seq_spec(B_W, PAST_LEN), seq_spec(B_W, PAST_LEN),
        _const_spec(bt.shape),
        _const_spec(woa.shape), _const_spec(wob.shape), _const_spec(wo.shape), _const_spec(gf.shape),
        _const_spec(wg.shape), _const_spec(wu.shape), _const_spec(wd.shape),
    ]
    return pl.pallas_call(
        _lat_attn_kernel,
        grid=(nb, nt),
        in_specs=in_specs,
        out_specs=tok_spec(D_MODEL),
        out_shape=jax.ShapeDtypeStruct((nb, n, D_MODEL), F32),
        compiler_params=pltpu.CompilerParams(dimension_semantics=("arbitrary", "arbitrary"),
                                             vmem_limit_bytes=VMEM_LIMIT),
        name="lat_attn",
    )(sink, x, mods, qa, ka4, va4, qb, kb, vb, ga, gb, cka, cva, ckb, cvb, bt,
      woa, wob, wo, gf, wg, wu, wd)


def _rope_tables(n):
    half = HEAD_DIM // 4
    inv_freq = 1.0 / (ROPE_BASE ** (np.arange(half, dtype=np.float64) / half))
    t = np.arange(n)
    parts_c, parts_s = [], []
    for pos in (t // GRID_W, t % GRID_W):
        ang = pos.astype(np.float64)[:, None] * inv_freq[None, :]
        c, s = np.cos(ang), np.sin(ang)
        parts_c += [c, c]
        parts_s += [-s, s]
    cos = np.concatenate(parts_c * 2, axis=1).astype(np.float32)
    sin = np.concatenate(parts_s * 2, axis=1).astype(np.float32)
    return jnp.asarray(cos), jnp.asarray(sin)


def _feature_major(cache, l):
    b, _, s, h, d = cache.shape
    return jnp.transpose(cache[:, l], (0, 2, 3, 1)).reshape(b, h * d, s)


def _token_major(x, heads):
    b, _, s = x.shape
    return jnp.transpose(x.reshape(b, 1, heads, HEAD_DIM, s), (0, 1, 4, 2, 3))


def kernel(x_prompt, x_sample, cache_a_k, cache_a_v, cache_b_k, cache_b_v, c, c_ctx, w_ada, b_ada, g_attn, w_in, q_norm_a, k_norm_a, q_norm_b, k_norm_b, sink_a, rpb_b, w_out_a, w_out_b, w_o, g_ffn, w_gate, w_up, w_down):
    nd, n, _ = x_sample.shape
    l = 0

    cvec = jnp.concatenate([c_ctx[None, :], c, jnp.zeros((N_MOD_ROWS - 1 - nd, D_MODEL), F32)], axis=0)
    mods = _ada(cvec, w_ada[l], b_ada[l][None, :])

    gattn, gf = g_attn[l][None, :], g_ffn[l][None, :]
    nrm = jnp.stack([jnp.tile(v[l], A_Q // HEAD_DIM) for v in (q_norm_a, k_norm_a, q_norm_b, k_norm_b)])
    ii = np.arange(2 * LANES) // HEAD_DIM
    bd = jnp.asarray(ii[:, None] == ii[None, :], dtype=BF16)
    sink = sink_a[l]
    weights = (w_in[l], w_out_a[l], w_out_b[l], w_o[l], w_gate[l], w_up[l], w_down[l])

    (y_prompt, ka, va, kb, vb, win, woa, wob, wo, wg, wu, wd) = _ctx_layer(
        x_prompt, mods, sink, gattn, nrm, bd, gf, weights)
    new_a_k = _token_major(ka, A_KV_HEADS)
    new_a_v = _token_major(va, A_KV_HEADS)
    new_b_k = _token_major(kb, B_HEADS)
    new_b_v = _token_major(vb, B_HEADS)

    cos, sin = _rope_tables(n)
    qa, ka4, va4, qb, lkb, lvb, ga, gb = _lat_proj(x_sample, mods, gattn, win, nrm, bd, cos, sin)

    rp = jnp.pad(rpb_b[l], ((0, 0), (0, 1), (0, HEAD_DIM - (2 * NA_COLS - 1))))
    bt = _na_bias(jnp.concatenate([rp[:, :-1], rp[:, 1:]], axis=-1))

    y_sample = _lat_attn(sink, x_sample, mods, qa, ka4, va4, qb, lkb, lvb, ga, gb,
                         _feature_major(cache_a_k, l), _feature_major(cache_a_v, l),
                         _feature_major(cache_b_k, l), _feature_major(cache_b_v, l),
                         bt, woa, wob, wo, gf, wg, wu, wd)
    return (y_prompt, y_sample, new_a_k, new_a_v, new_b_k, new_b_v)
```

```python
import numpy as np

import jax
import jax.numpy as jnp
from jax import lax
from jax.experimental import pallas as pl
from jax.experimental.pallas import tpu as pltpu

D_MODEL = 1024
SEQ = 256
DEC_SEQ = 1024
PAST_LEN = 256
GRID_W = 64
HEAD_DIM = 64
A_HEADS = 8
A_KV_HEADS = 2
A_WINDOW = 128
B_HEADS = 8
NA_ROWS = 8
NA_COLS = 16
D_FF = 2816
ROPE_BASE = 10000.0
RMS_EPS = 1e-6
NEG = -1e30

A_Q = A_HEADS * HEAD_DIM
A_KV = A_KV_HEADS * HEAD_DIM
B_W = B_HEADS * HEAD_DIM
O_QA, O_KA, O_VA = 0, A_Q, A_Q + A_KV
O_QB = A_Q + 2 * A_KV
O_KB, O_VB = O_QB + B_W, O_QB + 2 * B_W
O_GA = O_QB + 3 * B_W
O_GB = O_GA + D_MODEL
D_IN = O_GB + D_MODEL

LANES = 128
TOK = 256
VMEM_LIMIT = 58 * 1024 * 1024
N_MOD_ROWS = 8

F32 = jnp.float32
BF16 = jnp.bfloat16


def _dot(a, b):
    return jnp.dot(a, b, preferred_element_type=F32)


def _dot_nt(a, b):
    return lax.dot_general(a, b, (((1,), (1,)), ((), ())), preferred_element_type=F32)


def _split_bf16(x):
    hi = x.astype(BF16)
    lo = (x - hi.astype(F32)).astype(BF16)
    return hi, lo


def _sigmoid(x):
    return 1.0 / (1.0 + jnp.exp(-x))


def _rms_mod(x, g, shift, scale):
    ms = jnp.mean(x * x, axis=-1, keepdims=True)
    return (x * lax.rsqrt(ms + RMS_EPS) * g) * (1.0 + scale) + shift


def _head_norm(u, w):
    lo_mask = lax.broadcasted_iota(jnp.int32, (u.shape[0], LANES), 1) < HEAD_DIM
    parts = []
    for c in range(u.shape[1] // LANES):
        uc = u[:, c * LANES:(c + 1) * LANES]
        uu = uc * uc
        ss_lo = jnp.sum(jnp.where(lo_mask, uu, 0.0), axis=-1, keepdims=True)
        ss_hi = jnp.sum(jnp.where(lo_mask, 0.0, uu), axis=-1, keepdims=True)
        ss = jnp.where(lo_mask, ss_lo, ss_hi)
        parts.append(uc * lax.rsqrt(ss * (1.0 / HEAD_DIM) + RMS_EPS))
    un = parts[0] if len(parts) == 1 else jnp.concatenate(parts, axis=1)
    return un * w


def _project(h, win_ref, nrm_ref):
    qa = _head_norm(_dot(h, win_ref[:, O_QA:O_KA]), nrm_ref[0:1, :])
    ka = _head_norm(_dot(h, win_ref[:, O_KA:O_VA]), nrm_ref[1:2, :A_KV])
    va = _dot(h, win_ref[:, O_VA:O_QB])
    qb = _head_norm(_dot(h, win_ref[:, O_QB:O_KB]), nrm_ref[2:3, :])
    kb = _head_norm(_dot(h, win_ref[:, O_KB:O_VB]), nrm_ref[3:4, :])
    vb = _dot(h, win_ref[:, O_VB:O_GA])
    ga = _dot(h, win_ref[:, O_GA:O_GB])
    gb = _dot(h, win_ref[:, O_GB:D_IN])
    return qa, ka, va, qb, kb, vb, ga, gb


def _place_halves(x, lo_mask):
    h0_lo = jnp.where(lo_mask, x, 0.0)
    h1_hi = jnp.where(lo_mask, 0.0, x)
    return h0_lo, pltpu.roll(h0_lo, 64, 1), pltpu.roll(h1_hi, 64, 1), h1_hi


def _place_rows(xt, head, half):
    rows = xt[head * HEAD_DIM:(head + 1) * HEAD_DIM]
    zero = jnp.zeros_like(rows)
    return jnp.concatenate([rows, zero] if half == 0 else [zero, rows], axis=0)


def _softmax_pv(parts, sink=None):
    m = None
    for s, _, _ in parts:
        mi = jnp.max(s, axis=-1, keepdims=True)
        m = mi if m is None else jnp.maximum(m, mi)
    if sink is not None:
        m = jnp.maximum(m, sink)
    l = None if sink is None else jnp.exp(sink - m)
    out = None
    for s, v, v_fm in parts:
        p = jnp.exp(s - m)
        li = jnp.sum(p, axis=-1, keepdims=True)
        l = li if l is None else l + li
        o = _dot_nt(p.astype(BF16), v) if v_fm else _dot(p.astype(BF16), v)
        out = o if out is None else out + o
    return out * (1.0 / l)


def _mod(mod_ref, row, i):
    return mod_ref[pl.ds(row, 1), i * D_MODEL:(i + 1) * D_MODEL]


def _tail(x, ya, yb, ga, gb, mod_ref, row, woa_ref, wob_ref, wo_ref, gf_ref, wg_ref, wu_ref, wd_ref):
    gt1, sh2, sc2, gt2 = (_mod(mod_ref, row, i) for i in (2, 3, 4, 5))
    ma = _dot(ya.astype(BF16), woa_ref[...])
    mb = _dot(yb.astype(BF16), wob_ref[...])
    mg = _sigmoid(ga) * ma + _sigmoid(gb) * mb
    x1 = x + gt1 * _dot(mg.astype(BF16), wo_ref[...])
    h2 = _rms_mod(x1, gf_ref[...], sh2, sc2).astype(BF16)
    gate = _dot(h2, wg_ref[...])
    up = _dot(h2, wu_ref[...])
    act = (gate * _sigmoid(gate)) * up
    return x1 + gt2 * _dot(act.astype(BF16), wd_ref[...])


ADA_TK = 128


def _ada_kernel(c_ref, w_ref, b_ref, o_ref):
    @pl.when(pl.program_id(0) == 0)
    def _():
        o_ref[...] = jnp.broadcast_to(b_ref[...], o_ref.shape)

    c = c_ref[...]
    s_hi, s_lo = _split_bf16(c * _sigmoid(c))
    w_hi, w_lo = _split_bf16(w_ref[...])
    r = _dot(jnp.concatenate([s_hi, s_lo], axis=0), w_hi)
    o_ref[...] += r[:N_MOD_ROWS] + r[N_MOD_ROWS:] + _dot(s_hi, w_lo)


def _ada(cvec, w, b):
    k, n = w.shape
    return pl.pallas_call(
        _ada_kernel,
        grid=(k // ADA_TK,),
        in_specs=[pl.BlockSpec((N_MOD_ROWS, ADA_TK), lambda j: (0, j)),
                  pl.BlockSpec((ADA_TK, n), lambda j: (j, 0)),
                  pl.BlockSpec((1, n), lambda j: (0, 0))],
        out_specs=pl.BlockSpec((N_MOD_ROWS, n), lambda j: (0, 0)),
        out_shape=jax.ShapeDtypeStruct((N_MOD_ROWS, n), F32),
        compiler_params=pltpu.CompilerParams(dimension_semantics=("arbitrary",)),
        name="ada_mod",
    )(cvec, w, b)


W_CHUNK_ROWS = {D_IN: 32, D_MODEL: 128, D_FF: 64}
DMA_DEPTH = 6
N_WEIGHTS = 7


def _load_cast_all(srcs, dsts, stages, sems, on_done):
    tasks, ring_pos = [], {}
    for w, (src, dst) in enumerate(zip(srcs, dsts)):
        n_rows, cols = src.shape
        rows = W_CHUNK_ROWS[cols]
        for r0 in range(0, n_rows, rows):
            pos = ring_pos.get(cols, 0)
            ring_pos[cols] = pos + 1
            tasks.append((w, r0, rows, cols, pos % DMA_DEPTH, r0 + rows == n_rows))

    def copy(task):
        w, r0, rows, cols, slot, _ = task
        return pltpu.make_async_copy(srcs[w].at[pl.ds(r0, rows)], stages[cols].at[slot], sems[cols].at[slot])

    for task in tasks[:DMA_DEPTH]:
        copy(task).start()
    for i, task in enumerate(tasks):
        w, r0, rows, cols, slot, last = task
        copy(task).wait()
        dsts[w][pl.ds(r0, rows), :] = stages[cols][slot].astype(BF16)
        if i + DMA_DEPTH < len(tasks):
            copy(tasks[i + DMA_DEPTH]).start()
        if last:
            on_done(w)


def _ctx_kernel(sink_ref, x_ref, mod_ref, gattn_ref, nrm_ref, gf_ref,
                win_hbm, woa_hbm, wob_hbm, wo_hbm, wg_hbm, wu_hbm, wd_hbm,
                y_ref, ka_ref, va_ref, kb_ref, vb_ref,
                win_out, woa_out, wob_out, wo_out, wg_out, wu_out, wd_out,
                win_ref, woa_ref, wob_ref, wo_ref, wg_ref, wu_ref, wd_ref, out_sem):
    srcs = (win_hbm, woa_hbm, wob_hbm, wo_hbm, wg_hbm, wu_hbm, wd_hbm)
    dsts = (win_ref, woa_ref, wob_ref, wo_ref, wg_ref, wu_ref, wd_ref)
    outs = (win_out, woa_out, wob_out, wo_out, wg_out, wu_out, wd_out)
    first = pl.program_id(0) == 0

    def out_copy(i):
        return pltpu.make_async_copy(dsts[i], outs[i], out_sem.at[i])

    @pl.when(first)
    def _():
        widths = tuple(W_CHUNK_ROWS)

        def scoped(*refs):
            stages = dict(zip(widths, refs[:len(widths)]))
            sems = dict(zip(widths, refs[len(widths):]))
            _load_cast_all(srcs, dsts, stages, sems, lambda w: out_copy(w).start())

        pl.run_scoped(scoped,
                      *[pltpu.VMEM((DMA_DEPTH, W_CHUNK_ROWS[c], c), F32) for c in widths],
                      *[pltpu.SemaphoreType.DMA((DMA_DEPTH,)) for _ in widths])

    x = x_ref[0]
    h = _rms_mod(x, gattn_ref[...], _mod(mod_ref, 0, 0), _mod(mod_ref, 0, 1)).astype(BF16)
    qa, ka, va, qb, kb, vb, ga, gb = _project(h, win_ref, nrm_ref)
    ka_ref[0] = ka.T
    va_ref[0] = va.T
    kb_ref[0] = kb.T
    vb_ref[0] = vb.T

    lo_mask = lax.broadcasted_iota(jnp.int32, (TOK, LANES), 1) < HEAD_DIM
    scale = HEAD_DIM ** -0.5
    qa = (qa * scale).astype(BF16)
    qb = (qb * scale).astype(BF16)

    k_pl = [t.astype(BF16) for t in _place_halves(ka, lo_mask)]
    v_pl = [t.astype(BF16) for t in _place_halves(va, lo_mask)]
    ya = []
    for c in range(A_Q // LANES):
        j = c // 2
        qc = qa[:, c * LANES:(c + 1) * LANES]
        acc = None
        for half in range(2):
            i = 2 * j + half
            o = _softmax_pv([(_dot_nt(qc, k_pl[i]), v_pl[i], False)], sink=sink_ref[2 * c + half])
            acc = o if acc is None else acc + o
        ya.append(acc)
    ya = jnp.concatenate(ya, axis=1)

    yb = []
    for c in range(B_W // LANES):
        sl = slice(c * LANES, (c + 1) * LANES)
        qc = qb[:, sl]
        acc = None
        for half in range(2):
            hm = lo_mask if half == 0 else jnp.logical_not(lo_mask)
            km = jnp.where(hm, kb[:, sl], 0.0).astype(BF16)
            vm = jnp.where(hm, vb[:, sl], 0.0).astype(BF16)
            o = _softmax_pv([(_dot_nt(qc, km), vm, False)])
            acc = o if acc is None else acc + o
        yb.append(acc)
    yb = jnp.concatenate(yb, axis=1)

    y_ref[0] = _tail(x, ya, yb, ga, gb, mod_ref, 0, woa_ref, wob_ref, wo_ref, gf_ref, wg_ref, wu_ref, wd_ref)

    @pl.when(first)
    def _():
        for i in range(N_WEIGHTS):
            out_copy(i).wait()


def _const_spec(shape):
    nd = len(shape)
    return pl.BlockSpec(shape, lambda *_: (0,) * nd, pipeline_mode=pl.Buffered(1))


def _ctx_layer(x, mods, sink, gattn, nrm, gf, weights):
    nb = x.shape[0]
    tok_spec = lambda w: pl.BlockSpec((1, SEQ, w), lambda b: (b, 0, 0))
    fm_spec = lambda w: pl.BlockSpec((1, w, SEQ), lambda b: (b, 0, 0))
    any_spec = pl.BlockSpec(memory_space=pl.ANY)
    in_specs = [
        pl.BlockSpec(memory_space=pltpu.SMEM),
        tok_spec(D_MODEL),
        _const_spec(mods.shape), _const_spec(gattn.shape), _const_spec(nrm.shape), _const_spec(gf.shape),
    ] + [any_spec] * len(weights)
    out_specs = ([tok_spec(D_MODEL), fm_spec(A_KV), fm_spec(A_KV), fm_spec(B_W), fm_spec(B_W)]
                 + [any_spec] * len(weights))
    out_shape = ([jax.ShapeDtypeStruct((nb, SEQ, D_MODEL), F32)]
                 + [jax.ShapeDtypeStruct((nb, w, SEQ), F32) for w in (A_KV, A_KV, B_W, B_W)]
                 + [jax.ShapeDtypeStruct(w.shape, BF16) for w in weights])
    scratch = [pltpu.VMEM(w.shape, BF16) for w in weights] + [pltpu.SemaphoreType.DMA((len(weights),))]
    return pl.pallas_call(
        _ctx_kernel,
        grid=(nb,),
        in_specs=in_specs,
        out_specs=out_specs,
        out_shape=out_shape,
        scratch_shapes=scratch,
        compiler_params=pltpu.CompilerParams(dimension_semantics=("arbitrary",), vmem_limit_bytes=VMEM_LIMIT),
        name="ctx_layer",
    )(sink, x, mods, gattn, nrm, gf, *weights)


def _rope(x, cos, sin, bit16):
    swapped = jnp.where(bit16, pltpu.roll(x, 16, 1), pltpu.roll(x, LANES - 16, 1))
    return x * cos + swapped * sin


def _lat_proj_kernel(x_ref, mod_ref, gattn_ref, win_ref, nrm_ref, cos_ref, sin_ref,
                     qa_ref, ka_ref, va_ref, qb_ref, kb_ref, vb_ref, ga_ref, gb_ref):
    row = 1 + pl.program_id(0)
    x = x_ref[0]
    h = _rms_mod(x, gattn_ref[...], _mod(mod_ref, row, 0), _mod(mod_ref, row, 1)).astype(BF16)
    qa, ka, va, qb, kb, vb, ga, gb = _project(h, win_ref, nrm_ref)
    lane = lax.broadcasted_iota(jnp.int32, (TOK, LANES), 1)
    lo_mask = lane < HEAD_DIM
    bit16 = (lane & 16) != 0
    cos, sin = cos_ref[...], sin_ref[...]
    scale = HEAD_DIM ** -0.5
    for c in range(A_Q // LANES):
        sl = slice(c * LANES, (c + 1) * LANES)
        qa_ref[0, :, sl] = (_rope(qa[:, sl], cos, sin, bit16) * scale).astype(BF16)
    ka = _rope(ka, cos, sin, bit16)
    for i, t in enumerate(_place_halves(ka, lo_mask)):
        ka_ref[0, :, i * LANES:(i + 1) * LANES] = t.astype(BF16)
    for i, t in enumerate(_place_halves(va, lo_mask)):
        va_ref[0, :, i * LANES:(i + 1) * LANES] = t.astype(BF16)
    qb_ref[0] = (qb * scale).astype(BF16)
    kb_ref[0] = kb.astype(BF16)
    vb_ref[0] = vb.astype(BF16)
    ga_ref[0] = ga
    gb_ref[0] = gb


def _lat_proj(x, mods, gattn, win, nrm, cos, sin):
    nb, n, _ = x.shape
    nt = n // TOK
    tok_spec = lambda w: pl.BlockSpec((1, TOK, w), lambda b, t: (b, t, 0))
    in_specs = [
        tok_spec(D_MODEL),
        _const_spec(mods.shape), _const_spec(gattn.shape), _const_spec(win.shape), _const_spec(nrm.shape),
        pl.BlockSpec((TOK, LANES), lambda b, t: (t, 0)),
        pl.BlockSpec((TOK, LANES), lambda b, t: (t, 0)),
    ]
    widths = (A_Q, 4 * LANES, 4 * LANES, B_W, B_W, B_W, D_MODEL, D_MODEL)
    dtypes = (BF16,) * 6 + (F32, F32)
    return pl.pallas_call(
        _lat_proj_kernel,
        grid=(nb, nt),
        in_specs=in_specs,
        out_specs=[tok_spec(w) for w in widths],
        out_shape=[jax.ShapeDtypeStruct((nb, n, w), dt) for w, dt in zip(widths, dtypes)],
        compiler_params=pltpu.CompilerParams(dimension_semantics=("arbitrary", "arbitrary"),
                                             vmem_limit_bytes=VMEM_LIMIT),
        name="lat_proj",
    )(x, mods, gattn, win, nrm, cos, sin)


def _build_na_bias(row_ref, bt_ref):
    w = lax.broadcasted_iota(jnp.int32, (GRID_W, LANES), 0)
    kcol = lax.broadcasted_iota(jnp.int32, (GRID_W, LANES), 1) & (GRID_W - 1)
    cs = jnp.clip(w - NA_COLS // 2, 0, GRID_W - NA_COLS)
    valid = (kcol >= cs) & (kcol < cs + NA_COLS)
    for h in range(B_HEADS):
        for d in range(2 * NA_ROWS - 1):
            rows = jnp.broadcast_to(row_ref[h, d:d + 1, :], (GRID_W, LANES))
            t = pltpu.roll(rows, LANES - (NA_COLS - 1), 1, stride=1, stride_axis=0)
            bt_ref[h, d] = jnp.where(valid, t, NEG)


A_KEYS = 2 * TOK
B_KROWS = 12
B_KEYS = B_KROWS * GRID_W


def _lat_attn_kernel(sink_ref, x_ref, mod_ref, qa_ref, ka_ref, va_ref, qb_ref, kb_ref, vb_ref,
                     ga_ref, gb_ref, cka_ref, cva_ref, ckb_ref, cvb_ref, rpb_ref,
                     woa_ref, wob_ref, wo_ref, gf_ref, wg_ref, wu_ref, wd_ref, y_ref, bt_ref):
    t = pl.program_id(1)
    row = 1 + pl.program_id(0)
    x = x_ref[0]

    @pl.when((pl.program_id(0) == 0) & (t == 0))
    def _():
        _build_na_bias(rpb_ref, bt_ref)

    ks = pl.multiple_of(jnp.clip(TOK * t - A_WINDOW, 0, DEC_SEQ - A_KEYS), LANES)
    qpos = TOK * t + lax.broadcasted_iota(jnp.int32, (TOK, A_KEYS), 0)
    kpos = ks + lax.broadcasted_iota(jnp.int32, (TOK, A_KEYS), 1)
    a_valid = jnp.abs(qpos - kpos) <= A_WINDOW
    ck_t, cv_t = cka_ref[0], cva_ref[0]
    ya = []
    for c in range(A_Q // LANES):
        j = c // 2
        qc = qa_ref[0, :, c * LANES:(c + 1) * LANES]
        acc = None
        for half in range(2):
            i = 2 * j + half
            k_loc = ka_ref[0, pl.ds(ks, A_KEYS), i * LANES:(i + 1) * LANES]
            v_loc = va_ref[0, pl.ds(ks, A_KEYS), i * LANES:(i + 1) * LANES]
            s_loc = jnp.where(a_valid, _dot_nt(qc, k_loc), NEG)
            s_ctx = _dot(qc, _place_rows(ck_t, j, half).astype(BF16))
            o = _softmax_pv([(s_loc, v_loc, False), (s_ctx, _place_rows(cv_t, j, half).astype(BF16), True)],
                            sink=sink_ref[2 * c + half])
            acc = o if acc is None else acc + o
        ya.append(acc)
    ya = jnp.concatenate(ya, axis=1)

    kr0 = jnp.clip(4 * t - 4, 0, DEC_SEQ // GRID_W - B_KROWS)
    kst = pl.multiple_of(kr0 * GRID_W, TOK)
    qrow = 4 * t + (lax.broadcasted_iota(jnp.int32, (TOK, B_KEYS), 0) >> 6)
    krow = kr0 + (lax.broadcasted_iota(jnp.int32, (TOK, B_KEYS), 1) >> 6)
    rs = jnp.clip(qrow - NA_ROWS // 2, 0, DEC_SEQ // GRID_W - NA_ROWS)
    row_valid = (krow >= rs) & (krow < rs + NA_ROWS)
    lo_k = lax.broadcasted_iota(jnp.int32, (B_KEYS, LANES), 1) < HEAD_DIM
    lo_f = lax.broadcasted_iota(jnp.int32, (LANES, PAST_LEN), 0) < HEAD_DIM
    yb = []
    for c in range(B_W // LANES):
        sl = slice(c * LANES, (c + 1) * LANES)
        qc = qb_ref[0, :, sl]
        k_loc = kb_ref[0, pl.ds(kst, B_KEYS), sl]
        v_loc = vb_ref[0, pl.ds(kst, B_KEYS), sl]
        k_ctx = ckb_ref[0, sl, :]
        v_ctx = cvb_ref[0, sl, :]
        acc = None
        for half in range(2):
            head = 2 * c + half
            hm_k = lo_k if half == 0 else jnp.logical_not(lo_k)
            hm_f = lo_f if half == 0 else jnp.logical_not(lo_f)
            zero = jnp.zeros((), BF16)
            kk = jnp.where(hm_k, k_loc, zero)
            vv = jnp.where(hm_k, v_loc, zero)
            kc = jnp.where(hm_f, k_ctx, 0.0).astype(BF16)
            vc = jnp.where(hm_f, v_ctx, 0.0).astype(BF16)
            bias_rows = []
            for qi in range(TOK // GRID_W):
                blocks = []
                for p in range(B_KROWS // 2):
                    dr = jnp.clip(kr0 + 2 * p - (4 * t + qi) + (NA_ROWS - 1), 0, 2 * NA_ROWS - 2)
                    blocks.append(bt_ref[head, dr])
                bias_rows.append(jnp.concatenate(blocks, axis=1))
            bias = jnp.concatenate(bias_rows, axis=0)
            s_loc = jnp.where(row_valid, _dot_nt(qc, kk) + bias, NEG)
            s_ctx = _dot(qc, kc)
            o = _softmax_pv([(s_loc, vv, False), (s_ctx, vc, True)])
            acc = o if acc is None else acc + o
        yb.append(acc)
    yb = jnp.concatenate(yb, axis=1)

    y_ref[0] = _tail(x, ya, yb, ga_ref[0], gb_ref[0], mod_ref, row, woa_ref, wob_ref, wo_ref, gf_ref,
                     wg_ref, wu_ref, wd_ref)


def _lat_attn(sink, x, mods, qa, ka4, va4, qb, kb, vb, ga, gb, cka, cva, ckb, cvb, rpb_rows,
              woa, wob, wo, gf, wg, wu, wd):
    nb, n, _ = x.shape
    nt = n // TOK
    tok_spec = lambda w: pl.BlockSpec((1, TOK, w), lambda b, t: (b, t, 0))
    seq_spec = lambda rows, w: pl.BlockSpec((1, rows, w), lambda b, t: (b, 0, 0),
                                            pipeline_mode=pl.Buffered(1))
    in_specs = [
        pl.BlockSpec(memory_space=pltpu.SMEM),
        tok_spec(D_MODEL),
        _const_spec(mods.shape),
        tok_spec(A_Q), seq_spec(n, 4 * LANES), seq_spec(n, 4 * LANES),
        tok_spec(B_W), seq_spec(n, B_W), seq_spec(n, B_W),
        tok_spec(D_MODEL), tok_spec(D_MODEL),
        seq_spec(A_KV, PAST_LEN), seq_spec(A_KV, PAST_LEN), seq_spec(B_W, PAST_LEN), seq_spec(B_W, PAST_LEN),
        _const_spec(rpb_rows.shape),
        _const_spec(woa.shape), _const_spec(wob.shape), _const_spec(wo.shape), _const_spec(gf.shape),
        _const_spec(wg.shape), _const_spec(wu.shape), _const_spec(wd.shape),
    ]
    return pl.pallas_call(
        _lat_attn_kernel,
        grid=(nb, nt),
        in_specs=in_specs,
        out_specs=tok_spec(D_MODEL),
        out_shape=jax.ShapeDtypeStruct((nb, n, D_MODEL), F32),
        scratch_shapes=[pltpu.VMEM((B_HEADS, 2 * NA_ROWS - 1, GRID_W, LANES), F32)],
        compiler_params=pltpu.CompilerParams(dimension_semantics=("arbitrary", "arbitrary"),
                                             vmem_limit_bytes=VMEM_LIMIT),
        name="lat_attn",
    )(sink, x, mods, qa, ka4, va4, qb, kb, vb, ga, gb, cka, cva, ckb, cvb, rpb_rows,
      woa, wob, wo, gf, wg, wu, wd)


def _rope_tables(n):
    half = HEAD_DIM // 4
    inv_freq = 1.0 / (ROPE_BASE ** (np.arange(half, dtype=np.float64) / half))
    t = np.arange(n)
    parts_c, parts_s = [], []
    for pos in (t // GRID_W, t % GRID_W):
        ang = pos.astype(np.float64)[:, None] * inv_freq[None, :]
        c, s = np.cos(ang), np.sin(ang)
        parts_c += [c, c]
        parts_s += [-s, s]
    cos = np.concatenate(parts_c * 2, axis=1).astype(np.float32)
    sin = np.concatenate(parts_s * 2, axis=1).astype(np.float32)
    return jnp.asarray(cos), jnp.asarray(sin)


def _feature_major(cache, l):
    b, _, s, h, d = cache.shape
    return jnp.transpose(cache[:, l], (0, 2, 3, 1)).reshape(b, h * d, s)


def _token_major(x, heads):
    b, _, s = x.shape
    return jnp.transpose(x.reshape(b, 1, heads, HEAD_DIM, s), (0, 1, 4, 2, 3))


def kernel(x_prompt, x_sample, cache_a_k, cache_a_v, cache_b_k, cache_b_v, c, c_ctx, w_ada, b_ada, g_attn, w_in, q_norm_a, k_norm_a, q_norm_b, k_norm_b, sink_a, rpb_b, w_out_a, w_out_b, w_o, g_ffn, w_gate, w_up, w_down):
    nd, n, _ = x_sample.shape
    l = 0

    cvec = jnp.concatenate([c_ctx[None, :], c, jnp.zeros((N_MOD_ROWS - 1 - nd, D_MODEL), F32)], axis=0)
    mods = _ada(cvec, w_ada[l], b_ada[l][None, :])

    gattn, gf = g_attn[l][None, :], g_ffn[l][None, :]
    nrm = jnp.stack([jnp.tile(v[l], A_Q // HEAD_DIM) for v in (q_norm_a, k_norm_a, q_norm_b, k_norm_b)])
    sink = sink_a[l]
    weights = (w_in[l], w_out_a[l], w_out_b[l], w_o[l], w_gate[l], w_up[l], w_down[l])

    (y_prompt, ka, va, kb, vb, win, woa, wob, wo, wg, wu, wd) = _ctx_layer(
        x_prompt, mods, sink, gattn, nrm, gf, weights)
    new_a_k = _token_major(ka, A_KV_HEADS)
    new_a_v = _token_major(va, A_KV_HEADS)
    new_b_k = _token_major(kb, B_HEADS)
    new_b_v = _token_major(vb, B_HEADS)

    cos, sin = _rope_tables(n)
    qa, ka4, va4, qb, lkb, lvb, ga, gb = _lat_proj(x_sample, mods, gattn, win, nrm, cos, sin)

    rp = jnp.pad(rpb_b[l], ((0, 0), (0, 1), (0, HEAD_DIM - (2 * NA_COLS - 1))))
    rpb_rows = jnp.concatenate([rp[:, :-1], rp[:, 1:]], axis=-1)

    y_sample = _lat_attn(sink, x_sample, mods, qa, ka4, va4, qb, lkb, lvb, ga, gb,
                         _feature_major(cache_a_k, l), _feature_major(cache_a_v, l),
                         _feature_major(cache_b_k, l), _feature_major(cache_b_v, l),
                         rpb_rows, woa, wob, wo, gf, wg, wu, wd)
    return (y_prompt, y_sample, new_a_k, new_a_v, new_b_k, new_b_v)
```

```python
import numpy as np

import jax
import jax.numpy as jnp
from jax import lax
from jax.experimental import pallas as pl
from jax.experimental.pallas import tpu as pltpu

D_MODEL = 1024
SEQ = 256
DEC_SEQ = 1024
PAST_LEN = 256
GRID_W = 64
HEAD_DIM = 64
A_HEADS = 8
A_KV_HEADS = 2
A_WINDOW = 128
B_HEADS = 8
NA_ROWS = 8
NA_COLS = 16
D_FF = 2816
ROPE_BASE = 10000.0
RMS_EPS = 1e-6
NEG = -1e30

A_Q = A_HEADS * HEAD_DIM
A_KV = A_KV_HEADS * HEAD_DIM
B_W = B_HEADS * HEAD_DIM
O_QA, O_KA, O_VA = 0, A_Q, A_Q + A_KV
O_QB = A_Q + 2 * A_KV
O_KB, O_VB = O_QB + B_W, O_QB + 2 * B_W
O_GA = O_QB + 3 * B_W
O_GB = O_GA + D_MODEL
D_IN = O_GB + D_MODEL

LANES = 128
TOK = 256
VMEM_LIMIT = 58 * 1024 * 1024
N_MOD_ROWS = 8

F32 = jnp.float32
BF16 = jnp.bfloat16


def _dot(a, b):
    return jnp.dot(a, b, preferred_element_type=F32)


def _dot_nt(a, b):
    return lax.dot_general(a, b, (((1,), (1,)), ((), ())), preferred_element_type=F32)


def _split_bf16(x):
    hi = x.astype(BF16)
    lo = (x - hi.astype(F32)).astype(BF16)
    return hi, lo


def _sigmoid(x):
    return 1.0 / (1.0 + jnp.exp(-x))


def _rms_mod(x, g, shift, scale):
    ms = jnp.mean(x * x, axis=-1, keepdims=True)
    return (x * lax.rsqrt(ms + RMS_EPS) * g) * (1.0 + scale) + shift


def _head_norm(u, w):
    lo_mask = lax.broadcasted_iota(jnp.int32, (u.shape[0], LANES), 1) < HEAD_DIM
    parts = []
    for c in range(u.shape[1] // LANES):
        uc = u[:, c * LANES:(c + 1) * LANES]
        uu = uc * uc
        ss_lo = jnp.sum(jnp.where(lo_mask, uu, 0.0), axis=-1, keepdims=True)
        ss_hi = jnp.sum(jnp.where(lo_mask, 0.0, uu), axis=-1, keepdims=True)
        ss = jnp.where(lo_mask, ss_lo, ss_hi)
        parts.append(uc * lax.rsqrt(ss * (1.0 / HEAD_DIM) + RMS_EPS))
    un = parts[0] if len(parts) == 1 else jnp.concatenate(parts, axis=1)
    return un * w


def _project(h, win_ref, nrm_ref):
    qa = _head_norm(_dot(h, win_ref[:, O_QA:O_KA]), nrm_ref[0:1, :])
    ka = _head_norm(_dot(h, win_ref[:, O_KA:O_VA]), nrm_ref[1:2, :A_KV])
    va = _dot(h, win_ref[:, O_VA:O_QB])
    qb = _head_norm(_dot(h, win_ref[:, O_QB:O_KB]), nrm_ref[2:3, :])
    kb = _head_norm(_dot(h, win_ref[:, O_KB:O_VB]), nrm_ref[3:4, :])
    vb = _dot(h, win_ref[:, O_VB:O_GA])
    ga = _dot(h, win_ref[:, O_GA:O_GB])
    gb = _dot(h, win_ref[:, O_GB:D_IN])
    return qa, ka, va, qb, kb, vb, ga, gb


def _place_halves(x, lo_mask):
    h0_lo = jnp.where(lo_mask, x, 0.0)
    h1_hi = jnp.where(lo_mask, 0.0, x)
    return h0_lo, pltpu.roll(h0_lo, 64, 1), pltpu.roll(h1_hi, 64, 1), h1_hi


def _place_rows(xt, head, half):
    rows = xt[head * HEAD_DIM:(head + 1) * HEAD_DIM]
    zero = jnp.zeros_like(rows)
    return jnp.concatenate([rows, zero] if half == 0 else [zero, rows], axis=0)


def _row_reduce(blocks, combine, lane_reduce):
    acc = None
    for s in blocks:
        for c in range(s.shape[1] // LANES):
            chunk = s[:, c * LANES:(c + 1) * LANES]
            acc = chunk if acc is None else combine(acc, chunk)
    return lane_reduce(acc, axis=-1, keepdims=True)


def _attend_heads(score_fns, values, sinks):
    scores = [fn() for fn in score_fns]
    probs = []
    for blocks, sink in zip(scores, sinks):
        m = _row_reduce(blocks, jnp.maximum, jnp.max)
        if sink is not None:
            m = jnp.maximum(m, sink)
        ps = [jnp.exp(s - m) for s in blocks]
        l = _row_reduce(ps, jnp.add, jnp.sum)
        if sink is not None:
            l = l + jnp.exp(sink - m)
        probs.append(([p.astype(BF16) for p in ps], 1.0 / l))
    outs = []
    for (ps, inv_l), vals in zip(probs, values):
        out = None
        for p, (v, v_fm) in zip(ps, vals):
            o = _dot_nt(p, v) if v_fm else _dot(p, v)
            out = o if out is None else out + o
        outs.append(out * inv_l)
    return outs


def _merge_head_pairs(outs):
    return jnp.concatenate([outs[i] + outs[i + 1] for i in range(0, len(outs), 2)], axis=1)


def _mod(mod_ref, row, i):
    return mod_ref[pl.ds(row, 1), i * D_MODEL:(i + 1) * D_MODEL]


def _tail(x, ya, yb, ga, gb, mod_ref, row, woa_ref, wob_ref, wo_ref, gf_ref, wg_ref, wu_ref, wd_ref):
    gt1, sh2, sc2, gt2 = (_mod(mod_ref, row, i) for i in (2, 3, 4, 5))
    ma = _dot(ya.astype(BF16), woa_ref[...])
    mb = _dot(yb.astype(BF16), wob_ref[...])
    mg = _sigmoid(ga) * ma + _sigmoid(gb) * mb
    x1 = x + gt1 * _dot(mg.astype(BF16), wo_ref[...])
    h2 = _rms_mod(x1, gf_ref[...], sh2, sc2).astype(BF16)
    gate = _dot(h2, wg_ref[...])
    up = _dot(h2, wu_ref[...])
    act = (gate * _sigmoid(gate)) * up
    return x1 + gt2 * _dot(act.astype(BF16), wd_ref[...])


ADA_TK = 128


def _ada_kernel(c_ref, w_ref, b_ref, o_ref):
    @pl.when(pl.program_id(0) == 0)
    def _():
        o_ref[...] = jnp.broadcast_to(b_ref[...], o_ref.shape)

    c = c_ref[...]
    s_hi, s_lo = _split_bf16(c * _sigmoid(c))
    w_hi, w_lo = _split_bf16(w_ref[...])
    r = _dot(jnp.concatenate([s_hi, s_lo], axis=0), w_hi)
    o_ref[...] += r[:N_MOD_ROWS] + r[N_MOD_ROWS:] + _dot(s_hi, w_lo)


def _ada(cvec, w, b):
    k, n = w.shape
    return pl.pallas_call(
        _ada_kernel,
        grid=(k // ADA_TK,),
        in_specs=[pl.BlockSpec((N_MOD_ROWS, ADA_TK), lambda j: (0, j)),
                  pl.BlockSpec((ADA_TK, n), lambda j: (j, 0)),
                  pl.BlockSpec((1, n), lambda j: (0, 0))],
        out_specs=pl.BlockSpec((N_MOD_ROWS, n), lambda j: (0, 0)),
        out_shape=jax.ShapeDtypeStruct((N_MOD_ROWS, n), F32),
        compiler_params=pltpu.CompilerParams(dimension_semantics=("arbitrary",)),
        name="ada_mod",
    )(cvec, w, b)


W_CHUNK_ROWS = {D_IN: 32, D_MODEL: 128, D_FF: 64}
DMA_DEPTH = 6
N_WEIGHTS = 7
CTX_GROUP = 1


def _load_cast_all(srcs, dsts, stages, sems, on_done):
    tasks, ring_pos = [], {}
    for w, (src, dst) in enumerate(zip(srcs, dsts)):
        n_rows, cols = src.shape
        rows = W_CHUNK_ROWS[cols]
        for r0 in range(0, n_rows, rows):
            pos = ring_pos.get(cols, 0)
            ring_pos[cols] = pos + 1
            tasks.append((w, r0, rows, cols, pos % DMA_DEPTH, r0 + rows == n_rows))

    def copy(task):
        w, r0, rows, cols, slot, _ = task
        return pltpu.make_async_copy(srcs[w].at[pl.ds(r0, rows)], stages[cols].at[slot], sems[cols].at[slot])

    for task in tasks[:DMA_DEPTH]:
        copy(task).start()
    for i, task in enumerate(tasks):
        w, r0, rows, cols, slot, last = task
        copy(task).wait()
        dsts[w][pl.ds(r0, rows), :] = stages[cols][slot].astype(BF16)
        if i + DMA_DEPTH < len(tasks):
            copy(tasks[i + DMA_DEPTH]).start()
        if last:
            on_done(w)


def _ctx_kernel(sink_ref, x_ref, mod_ref, gattn_ref, nrm_ref, gf_ref,
                win_hbm, woa_hbm, wob_hbm, wo_hbm, wg_hbm, wu_hbm, wd_hbm,
                y_ref, ka_ref, va_ref, kb_ref, vb_ref,
                win_out, woa_out, wob_out, wo_out, wg_out, wu_out, wd_out,
                win_ref, woa_ref, wob_ref, wo_ref, wg_ref, wu_ref, wd_ref, out_sem):
    srcs = (win_hbm, woa_hbm, wob_hbm, wo_hbm, wg_hbm, wu_hbm, wd_hbm)
    dsts = (win_ref, woa_ref, wob_ref, wo_ref, wg_ref, wu_ref, wd_ref)
    outs = (win_out, woa_out, wob_out, wo_out, wg_out, wu_out, wd_out)
    first = pl.program_id(0) == 0

    def out_copy(i):
        return pltpu.make_async_copy(dsts[i], outs[i], out_sem.at[i])

    @pl.when(first)
    def _():
        widths = tuple(W_CHUNK_ROWS)

        def scoped(*refs):
            stages = dict(zip(widths, refs[:len(widths)]))
            sems = dict(zip(widths, refs[len(widths):]))
            _load_cast_all(srcs, dsts, stages, sems, lambda w: out_copy(w).start())

        pl.run_scoped(scoped,
                      *[pltpu.VMEM((DMA_DEPTH, W_CHUNK_ROWS[c], c), F32) for c in widths],
                      *[pltpu.SemaphoreType.DMA((DMA_DEPTH,)) for _ in widths])

    x = x_ref[0]
    h = _rms_mod(x, gattn_ref[...], _mod(mod_ref, 0, 0), _mod(mod_ref, 0, 1)).astype(BF16)
    qa, ka, va, qb, kb, vb, ga, gb = _project(h, win_ref, nrm_ref)
    ka_ref[0] = ka.T
    va_ref[0] = va.T
    kb_ref[0] = kb.T
    vb_ref[0] = vb.T

    lo_mask = lax.broadcasted_iota(jnp.int32, (TOK, LANES), 1) < HEAD_DIM
    scale = HEAD_DIM ** -0.5
    qa = (qa * scale).astype(BF16)
    qb = (qb * scale).astype(BF16)

    k_pl = [t.astype(BF16) for t in _place_halves(ka, lo_mask)]
    v_pl = [t.astype(BF16) for t in _place_halves(va, lo_mask)]
    ya = []
    for g in range(0, A_HEADS, CTX_GROUP):
        score_fns, values, sinks = [], [], []
        for head in range(g, g + CTX_GROUP):
            c, half = divmod(head, 2)
            i = 2 * (c // 2) + half
            score_fns.append(lambda c=c, i=i: [_dot_nt(qa[:, c * LANES:(c + 1) * LANES], k_pl[i])])
            values.append([(v_pl[i], False)])
            sinks.append(sink_ref[head])
        ya += _attend_heads(score_fns, values, sinks)
    ya = _merge_head_pairs(ya)

    yb = []
    for g in range(0, B_HEADS, CTX_GROUP):
        score_fns, values = [], []
        for head in range(g, g + CTX_GROUP):
            c, half = divmod(head, 2)
            sl = slice(c * LANES, (c + 1) * LANES)
            hm = lo_mask if half == 0 else jnp.logical_not(lo_mask)
            km = jnp.where(hm, kb[:, sl], 0.0).astype(BF16)
            vm = jnp.where(hm, vb[:, sl], 0.0).astype(BF16)
            score_fns.append(lambda sl=sl, km=km: [_dot_nt(qb[:, sl], km)])
            values.append([(vm, False)])
        yb += _attend_heads(score_fns, values, [None] * CTX_GROUP)
    yb = _merge_head_pairs(yb)

    y_ref[0] = _tail(x, ya, yb, ga, gb, mod_ref, 0, woa_ref, wob_ref, wo_ref, gf_ref, wg_ref, wu_ref, wd_ref)

    @pl.when(first)
    def _():
        for i in range(N_WEIGHTS):
            out_copy(i).wait()


def _const_spec(shape):
    nd = len(shape)
    return pl.BlockSpec(shape, lambda *_: (0,) * nd, pipeline_mode=pl.Buffered(1))


def _ctx_layer(x, mods, sink, gattn, nrm, gf, weights):
    nb = x.shape[0]
    tok_spec = lambda w: pl.BlockSpec((1, SEQ, w), lambda b: (b, 0, 0))
    fm_spec = lambda w: pl.BlockSpec((1, w, SEQ), lambda b: (b, 0, 0))
    any_spec = pl.BlockSpec(memory_space=pl.ANY)
    in_specs = [
        pl.BlockSpec(memory_space=pltpu.SMEM),
        tok_spec(D_MODEL),
        _const_spec(mods.shape), _const_spec(gattn.shape), _const_spec(nrm.shape), _const_spec(gf.shape),
    ] + [any_spec] * len(weights)
    out_specs = ([tok_spec(D_MODEL), fm_spec(A_KV), fm_spec(A_KV), fm_spec(B_W), fm_spec(B_W)]
                 + [any_spec] * len(weights))
    out_shape = ([jax.ShapeDtypeStruct((nb, SEQ, D_MODEL), F32)]
                 + [jax.ShapeDtypeStruct((nb, w, SEQ), F32) for w in (A_KV, A_KV, B_W, B_W)]
                 + [jax.ShapeDtypeStruct(w.shape, BF16) for w in weights])
    scratch = [pltpu.VMEM(w.shape, BF16) for w in weights] + [pltpu.SemaphoreType.DMA((len(weights),))]
    return pl.pallas_call(
        _ctx_kernel,
        grid=(nb,),
        in_specs=in_specs,
        out_specs=out_specs,
        out_shape=out_shape,
        scratch_shapes=scratch,
        compiler_params=pltpu.CompilerParams(dimension_semantics=("arbitrary",), vmem_limit_bytes=VMEM_LIMIT),
        name="ctx_layer",
    )(sink, x, mods, gattn, nrm, gf, *weights)


def _rope(x, cos, sin, bit16):
    swapped = jnp.where(bit16, pltpu.roll(x, 16, 1), pltpu.roll(x, LANES - 16, 1))
    return x * cos + swapped * sin


def _lat_proj_kernel(x_ref, mod_ref, gattn_ref, win_ref, nrm_ref, cos_ref, sin_ref,
                     qa_ref, ka_ref, va_ref, qb_ref, kb_ref, vb_ref, ga_ref, gb_ref):
    row = 1 + pl.program_id(0)
    x = x_ref[0]
    h = _rms_mod(x, gattn_ref[...], _mod(mod_ref, row, 0), _mod(mod_ref, row, 1)).astype(BF16)
    qa, ka, va, qb, kb, vb, ga, gb = _project(h, win_ref, nrm_ref)
    lane = lax.broadcasted_iota(jnp.int32, (TOK, LANES), 1)
    lo_mask = lane < HEAD_DIM
    bit16 = (lane & 16) != 0
    cos, sin = cos_ref[...], sin_ref[...]
    scale = HEAD_DIM ** -0.5
    for c in range(A_Q // LANES):
        sl = slice(c * LANES, (c + 1) * LANES)
        qa_ref[0, :, sl] = (_rope(qa[:, sl], cos, sin, bit16) * scale).astype(BF16)
    ka = _rope(ka, cos, sin, bit16)
    for i, t in enumerate(_place_halves(ka, lo_mask)):
        ka_ref[0, :, i * LANES:(i + 1) * LANES] = t.astype(BF16)
    for i, t in enumerate(_place_halves(va, lo_mask)):
        va_ref[0, :, i * LANES:(i + 1) * LANES] = t.astype(BF16)
    qb_ref[0] = (qb * scale).astype(BF16)
    kb_ref[0] = kb.astype(BF16)
    vb_ref[0] = vb.astype(BF16)
    ga_ref[0] = ga
    gb_ref[0] = gb


def _lat_proj(x, mods, gattn, win, nrm, cos, sin):
    nb, n, _ = x.shape
    nt = n // TOK
    tok_spec = lambda w: pl.BlockSpec((1, TOK, w), lambda b, t: (b, t, 0))
    in_specs = [
        tok_spec(D_MODEL),
        _const_spec(mods.shape), _const_spec(gattn.shape), _const_spec(win.shape), _const_spec(nrm.shape),
        pl.BlockSpec((TOK, LANES), lambda b, t: (t, 0)),
        pl.BlockSpec((TOK, LANES), lambda b, t: (t, 0)),
    ]
    widths = (A_Q, 4 * LANES, 4 * LANES, B_W, B_W, B_W, D_MODEL, D_MODEL)
    dtypes = (BF16,) * 6 + (F32, F32)
    return pl.pallas_call(
        _lat_proj_kernel,
        grid=(nb, nt),
        in_specs=in_specs,
        out_specs=[tok_spec(w) for w in widths],
        out_shape=[jax.ShapeDtypeStruct((nb, n, w), dt) for w, dt in zip(widths, dtypes)],
        compiler_params=pltpu.CompilerParams(dimension_semantics=("arbitrary", "arbitrary"),
                                             vmem_limit_bytes=VMEM_LIMIT),
        name="lat_proj",
    )(x, mods, gattn, win, nrm, cos, sin)


def _build_na_bias(row_ref, bt_ref):
    w = lax.broadcasted_iota(jnp.int32, (GRID_W, LANES), 0)
    kcol = lax.broadcasted_iota(jnp.int32, (GRID_W, LANES), 1) & (GRID_W - 1)
    cs = jnp.clip(w - NA_COLS // 2, 0, GRID_W - NA_COLS)
    valid = (kcol >= cs) & (kcol < cs + NA_COLS)
    for h in range(B_HEADS):
        for d in range(2 * NA_ROWS - 1):
            rows = jnp.broadcast_to(row_ref[h, d:d + 1, :], (GRID_W, LANES))
            t = pltpu.roll(rows, LANES - (NA_COLS - 1), 1, stride=1, stride_axis=0)
            bt_ref[h, d] = jnp.where(valid, t, NEG)


A_KEYS = 2 * TOK
B_KROWS = 12
B_KEYS = B_KROWS * GRID_W
ATTN_GROUP = 4


def _lat_attn_kernel(sink_ref, x_ref, mod_ref, qa_ref, ka_ref, va_ref, qb_ref, kb_ref, vb_ref,
                     ga_ref, gb_ref, cka_ref, cva_ref, ckb_ref, cvb_ref, rpb_ref,
                     woa_ref, wob_ref, wo_ref, gf_ref, wg_ref, wu_ref, wd_ref, y_ref, bt_ref):
    t = pl.program_id(1)
    row = 1 + pl.program_id(0)
    x = x_ref[0]

    @pl.when((pl.program_id(0) == 0) & (t == 0))
    def _():
        _build_na_bias(rpb_ref, bt_ref)

    ks = pl.multiple_of(jnp.clip(TOK * t - A_WINDOW, 0, DEC_SEQ - A_KEYS), LANES)
    qpos = TOK * t + lax.broadcasted_iota(jnp.int32, (TOK, A_KEYS), 0)
    kpos = ks + lax.broadcasted_iota(jnp.int32, (TOK, A_KEYS), 1)
    a_valid = jnp.abs(qpos - kpos) <= A_WINDOW
    ck_t, cv_t = cka_ref[0], cva_ref[0]
    ck_pl = [_place_rows(ck_t, i // 2, i % 2).astype(BF16) for i in range(4)]
    cv_pl = [_place_rows(cv_t, i // 2, i % 2).astype(BF16) for i in range(4)]
    ya = []
    for g in range(0, A_HEADS, ATTN_GROUP):
        score_fns, values, sinks = [], [], []
        for head in range(g, g + ATTN_GROUP):
            c, half = divmod(head, 2)
            i = 2 * (c // 2) + half

            def a_scores(c=c, i=i):
                qc = qa_ref[0, :, c * LANES:(c + 1) * LANES]
                k_loc = ka_ref[0, pl.ds(ks, A_KEYS), i * LANES:(i + 1) * LANES]
                return [jnp.where(a_valid, _dot_nt(qc, k_loc), NEG), _dot(qc, ck_pl[i])]

            score_fns.append(a_scores)
            values.append([(va_ref[0, pl.ds(ks, A_KEYS), i * LANES:(i + 1) * LANES], False), (cv_pl[i], True)])
            sinks.append(sink_ref[head])
        ya.append(_merge_head_pairs(_attend_heads(score_fns, values, sinks)))
    ya = jnp.concatenate(ya, axis=1)

    kr0 = jnp.clip(4 * t - 4, 0, DEC_SEQ // GRID_W - B_KROWS)
    kst = pl.multiple_of(kr0 * GRID_W, TOK)
    lane_lo = lax.broadcasted_iota(jnp.int32, (1, LANES), 1) < HEAD_DIM
    b_dr, b_valid = [], []
    for qi in range(TOK // GRID_W):
        r = 4 * t + qi
        rs = jnp.clip(r - NA_ROWS // 2, 0, DEC_SEQ // GRID_W - NA_ROWS)
        drs, valids = [], []
        for p in range(B_KROWS // 2):
            kr = kr0 + 2 * p
            drs.append(jnp.clip(kr - r + (NA_ROWS - 1), 0, 2 * NA_ROWS - 2))
            ok = [((k >= rs) & (k < rs + NA_ROWS)).astype(jnp.int32) for k in (kr, kr + 1)]
            valids.append(jnp.where(lane_lo, ok[0], ok[1]) != 0)
        b_dr.append(drs)
        b_valid.append(valids)
    lo_k = lax.broadcasted_iota(jnp.int32, (B_KEYS, LANES), 1) < HEAD_DIM
    lo_f = lax.broadcasted_iota(jnp.int32, (LANES, PAST_LEN), 0) < HEAD_DIM
    zero = jnp.zeros((), BF16)
    yb = []
    for g in range(0, B_HEADS, ATTN_GROUP):
        score_fns, values = [], []
        for head in range(g, g + ATTN_GROUP):
            c, half = divmod(head, 2)
            sl = slice(c * LANES, (c + 1) * LANES)
            hm_k = lo_k if half == 0 else jnp.logical_not(lo_k)
            hm_f = lo_f if half == 0 else jnp.logical_not(lo_f)

            def b_scores(head=head, sl=sl, hm_k=hm_k, hm_f=hm_f):
                qc = qb_ref[0, :, sl]
                kk = jnp.where(hm_k, kb_ref[0, pl.ds(kst, B_KEYS), sl], zero)
                kc = jnp.where(hm_f, ckb_ref[0, sl, :], 0.0).astype(BF16)
                s = _dot_nt(qc, kk)
                rows = []
                for qi in range(TOK // GRID_W):
                    blocks = []
                    for p in range(B_KROWS // 2):
                        blk = s[qi * GRID_W:(qi + 1) * GRID_W, p * LANES:(p + 1) * LANES] + bt_ref[head, b_dr[qi][p]]
                        blocks.append(jnp.where(b_valid[qi][p], blk, NEG))
                    rows.append(jnp.concatenate(blocks, axis=1))
                return [jnp.concatenate(rows, axis=0), _dot(qc, kc)]

            score_fns.append(b_scores)
            vv = jnp.where(hm_k, vb_ref[0, pl.ds(kst, B_KEYS), sl], zero)
            vc = jnp.where(hm_f, cvb_ref[0, sl, :], 0.0).astype(BF16)
            values.append([(vv, False), (vc, True)])
        yb.append(_merge_head_pairs(_attend_heads(score_fns, values, [None] * ATTN_GROUP)))
    yb = jnp.concatenate(yb, axis=1)

    y_ref[0] = _tail(x, ya, yb, ga_ref[0], gb_ref[0], mod_ref, row, woa_ref, wob_ref, wo_ref, gf_ref,
                     wg_ref, wu_ref, wd_ref)


def _lat_attn(sink, x, mods, qa, ka4, va4, qb, kb, vb, ga, gb, cka, cva, ckb, cvb, rpb_rows,
              woa, wob, wo, gf, wg, wu, wd):
    nb, n, _ = x.shape
    nt = n // TOK
    tok_spec = lambda w: pl.BlockSpec((1, TOK, w), lambda b, t: (b, t, 0))
    seq_spec = lambda rows, w: pl.BlockSpec((1, rows, w), lambda b, t: (b, 0, 0),
                                            pipeline_mode=pl.Buffered(1))
    in_specs = [
        pl.BlockSpec(memory_space=pltpu.SMEM),
        tok_spec(D_MODEL),
        _const_spec(mods.shape),
        tok_spec(A_Q), seq_spec(n, 4 * LANES), seq_spec(n, 4 * LANES),
        tok_spec(B_W), seq_spec(n, B_W), seq_spec(n, B_W),
        tok_spec(D_MODEL), tok_spec(D_MODEL),
        seq_spec(A_KV, PAST_LEN), seq_spec(A_KV, PAST_LEN), seq_spec(B_W, PAST_LEN), seq_spec(B_W, PAST_LEN),
        _const_spec(rpb_rows.shape),
        _const_spec(woa.shape), _const_spec(wob.shape), _const_spec(wo.shape), _const_spec(gf.shape),
        _const_spec(wg.shape), _const_spec(wu.shape), _const_spec(wd.shape),
    ]
    return pl.pallas_call(
        _lat_attn_kernel,
        grid=(nb, nt),
        in_specs=in_specs,
        out_specs=tok_spec(D_MODEL),
        out_shape=jax.ShapeDtypeStruct((nb, n, D_MODEL), F32),
        scratch_shapes=[pltpu.VMEM((B_HEADS, 2 * NA_ROWS - 1, GRID_W, LANES), F32)],
        compiler_params=pltpu.CompilerParams(dimension_semantics=("arbitrary", "arbitrary"),
                                             vmem_limit_bytes=VMEM_LIMIT),
        name="lat_attn",
    )(sink, x, mods, qa, ka4, va4, qb, kb, vb, ga, gb, cka, cva, ckb, cvb, rpb_rows,
      woa, wob, wo, gf, wg, wu, wd)


def _rope_tables(n):
    half = HEAD_DIM // 4
    inv_freq = 1.0 / (ROPE_BASE ** (np.arange(half, dtype=np.float64) / half))
    t = np.arange(n)
    parts_c, parts_s = [], []
    for pos in (t // GRID_W, t % GRID_W):
        ang = pos.astype(np.float64)[:, None] * inv_freq[None, :]
        c, s = np.cos(ang), np.sin(ang)
        parts_c += [c, c]
        parts_s += [-s, s]
    cos = np.concatenate(parts_c * 2, axis=1).astype(np.float32)
    sin = np.concatenate(parts_s * 2, axis=1).astype(np.float32)
    return jnp.asarray(cos), jnp.asarray(sin)


def _feature_major(cache, l):
    b, _, s, h, d = cache.shape
    return jnp.transpose(cache[:, l], (0, 2, 3, 1)).reshape(b, h * d, s)


def _token_major(x, heads):
    b, _, s = x.shape
    return jnp.transpose(x.reshape(b, 1, heads, HEAD_DIM, s), (0, 1, 4, 2, 3))


def kernel(x_prompt, x_sample, cache_a_k, cache_a_v, cache_b_k, cache_b_v, c, c_ctx, w_ada, b_ada, g_attn, w_in, q_norm_a, k_norm_a, q_norm_b, k_norm_b, sink_a, rpb_b, w_out_a, w_out_b, w_o, g_ffn, w_gate, w_up, w_down):
    nd, n, _ = x_sample.shape
    l = 0

    cvec = jnp.concatenate([c_ctx[None, :], c, jnp.zeros((N_MOD_ROWS - 1 - nd, D_MODEL), F32)], axis=0)
    mods = _ada(cvec, w_ada[l], b_ada[l][None, :])

    gattn, gf = g_attn[l][None, :], g_ffn[l][None, :]
    nrm = jnp.stack([jnp.tile(v[l], A_Q // HEAD_DIM) for v in (q_norm_a, k_norm_a, q_norm_b, k_norm_b)])
    sink = sink_a[l]
    weights = (w_in[l], w_out_a[l], w_out_b[l], w_o[l], w_gate[l], w_up[l], w_down[l])

    (y_prompt, ka, va, kb, vb, win, woa, wob, wo, wg, wu, wd) = _ctx_layer(
        x_prompt, mods, sink, gattn, nrm, gf, weights)
    new_a_k = _token_major(ka, A_KV_HEADS)
    new_a_v = _token_major(va, A_KV_HEADS)
    new_b_k = _token_major(kb, B_HEADS)
    new_b_v = _token_major(vb, B_HEADS)

    cos, sin = _rope_tables(n)
    qa, ka4, va4, qb, lkb, lvb, ga, gb = _lat_proj(x_sample, mods, gattn, win, nrm, cos, sin)

    rp = jnp.pad(rpb_b[l], ((0, 0), (0, 1), (0, HEAD_DIM - (2 * NA_COLS - 1))))
    rpb_rows = jnp.concatenate([rp[:, :-1], rp[:, 1:]], axis=-1)

    y_sample = _lat_attn(sink, x_sample, mods, qa, ka4, va4, qb, lkb, lvb, ga, gb,
                         _feature_major(cache_a_k, l), _feature_major(cache_a_v, l),
                         _feature_major(cache_b_k, l), _feature_major(cache_b_v, l),
                         rpb_rows, woa, wob, wo, gf, wg, wu, wd)
    return (y_prompt, y_sample, new_a_k, new_a_v, new_b_k, new_b_v)
```

```python
import numpy as np

import jax
import jax.numpy as jnp
from jax import lax
from jax.experimental import pallas as pl
from jax.experimental.pallas import tpu as pltpu

D_MODEL = 1024
SEQ = 256
DEC_SEQ = 1024
PAST_LEN = 256
GRID_W = 64
HEAD_DIM = 64
A_HEADS = 8
A_KV_HEADS = 2
A_WINDOW = 128
B_HEADS = 8
NA_ROWS = 8
NA_COLS = 16
D_FF = 2816
ROPE_BASE = 10000.0
RMS_EPS = 1e-6
NEG = -1e30

A_Q = A_HEADS * HEAD_DIM
A_KV = A_KV_HEADS * HEAD_DIM
B_W = B_HEADS * HEAD_DIM
O_QA, O_KA, O_VA = 0, A_Q, A_Q + A_KV
O_QB = A_Q + 2 * A_KV
O_KB, O_VB = O_QB + B_W, O_QB + 2 * B_W
O_GA = O_QB + 3 * B_W
O_GB = O_GA + D_MODEL
D_IN = O_GB + D_MODEL

LANES = 128
TOK = 256
VMEM_LIMIT = 58 * 1024 * 1024
N_MOD_ROWS = 8

F32 = jnp.float32
BF16 = jnp.bfloat16


def _dot(a, b):
    return jnp.dot(a, b, preferred_element_type=F32)


def _dot_nt(a, b):
    return lax.dot_general(a, b, (((1,), (1,)), ((), ())), preferred_element_type=F32)


def _split_bf16(x):
    hi = x.astype(BF16)
    lo = (x - hi.astype(F32)).astype(BF16)
    return hi, lo


def _sigmoid(x):
    return 1.0 / (1.0 + jnp.exp(-x))


def _rms_mod(x, g, shift, scale):
    ms = jnp.mean(x * x, axis=-1, keepdims=True)
    return (x * lax.rsqrt(ms + RMS_EPS) * g) * (1.0 + scale) + shift


def _head_norm(u, w):
    lo_mask = lax.broadcasted_iota(jnp.int32, (u.shape[0], LANES), 1) < HEAD_DIM
    parts = []
    for c in range(u.shape[1] // LANES):
        uc = u[:, c * LANES:(c + 1) * LANES]
        uu = uc * uc
        ss_lo = jnp.sum(jnp.where(lo_mask, uu, 0.0), axis=-1, keepdims=True)
        ss_hi = jnp.sum(jnp.where(lo_mask, 0.0, uu), axis=-1, keepdims=True)
        ss = jnp.where(lo_mask, ss_lo, ss_hi)
        parts.append(uc * lax.rsqrt(ss * (1.0 / HEAD_DIM) + RMS_EPS))
    un = parts[0] if len(parts) == 1 else jnp.concatenate(parts, axis=1)
    return un * w


def _project_steps(h, win_ref, nrm_ref, out):
    bounds = (O_QA, O_KA, O_VA, O_QB, O_KB, O_VB, O_GA, O_GB, D_IN)
    norm_rows = {0: 0, 1: 1, 3: 2, 4: 3}
    for i in range(8):
        u = _dot(h, win_ref[:, bounds[i]:bounds[i + 1]])
        if i in norm_rows:
            r = norm_rows[i]
            u = _head_norm(u, nrm_ref[r:r + 1, :u.shape[1]])
        out.append(u)
        yield


def _project(h, win_ref, nrm_ref):
    out = []
    _drain(_project_steps(h, win_ref, nrm_ref, out))
    return out


def _place_halves(x, lo_mask):
    h0_lo = jnp.where(lo_mask, x, 0.0)
    h1_hi = jnp.where(lo_mask, 0.0, x)
    return h0_lo, pltpu.roll(h0_lo, 64, 1), pltpu.roll(h1_hi, 64, 1), h1_hi


def _place_rows(xt, head, half):
    rows = xt[head * HEAD_DIM:(head + 1) * HEAD_DIM]
    zero = jnp.zeros_like(rows)
    return jnp.concatenate([rows, zero] if half == 0 else [zero, rows], axis=0)


def _row_reduce(blocks, combine, lane_reduce):
    acc = None
    for s in blocks:
        for c in range(s.shape[1] // LANES):
            chunk = s[:, c * LANES:(c + 1) * LANES]
            acc = chunk if acc is None else combine(acc, chunk)
    return lane_reduce(acc, axis=-1, keepdims=True)


def _drain(steps):
    for _ in steps:
        pass


def _attend_heads_steps(score_fns, values, sinks, outs):
    scores = []
    for fn in score_fns:
        scores.append(fn())
        yield
    probs = []
    for blocks, sink in zip(scores, sinks):
        m = _row_reduce(blocks, jnp.maximum, jnp.max)
        if sink is not None:
            m = jnp.maximum(m, sink)
        ps = [jnp.exp(s - m) for s in blocks]
        l = _row_reduce(ps, jnp.add, jnp.sum)
        if sink is not None:
            l = l + jnp.exp(sink - m)
        probs.append(([p.astype(BF16) for p in ps], 1.0 / l))
        yield
    for (ps, inv_l), vals in zip(probs, values):
        out = None
        for p, (v, v_fm) in zip(ps, vals):
            o = _dot_nt(p, v) if v_fm else _dot(p, v)
            out = o if out is None else out + o
        outs.append(out * inv_l)
        yield


def _attend_heads(score_fns, values, sinks):
    outs = []
    _drain(_attend_heads_steps(score_fns, values, sinks, outs))
    return outs


def _merge_head_pairs(outs):
    return jnp.concatenate([outs[i] + outs[i + 1] for i in range(0, len(outs), 2)], axis=1)


def _mod(mod_ref, row, i):
    return mod_ref[pl.ds(row, 1), i * D_MODEL:(i + 1) * D_MODEL]


W_IN, W_OUT_A, W_OUT_B, W_O, W_GATE, W_UP, W_DOWN = range(7)
N_WEIGHTS = 7


def _tail_steps(x, ya, yb, ga, gb, mod_ref, row, gf_ref, w, y_ref):
    gt1, sh2, sc2, gt2 = (_mod(mod_ref, row, i) for i in (2, 3, 4, 5))
    yield W_OUT_A
    ma = _dot(ya.astype(BF16), w[W_OUT_A][...])
    yield W_OUT_B
    mb = _dot(yb.astype(BF16), w[W_OUT_B][...])
    mg = _sigmoid(ga) * ma + _sigmoid(gb) * mb
    yield W_O
    x1 = x + gt1 * _dot(mg.astype(BF16), w[W_O][...])
    h2 = _rms_mod(x1, gf_ref[...], sh2, sc2).astype(BF16)
    yield W_GATE
    gate = _dot(h2, w[W_GATE][...])
    yield W_UP
    up = _dot(h2, w[W_UP][...])
    act = (gate * _sigmoid(gate)) * up
    yield W_DOWN
    y_ref[0] = x1 + gt2 * _dot(act.astype(BF16), w[W_DOWN][...])


ADA_TK = 128


def _ada_kernel(c_ref, w_ref, b_ref, o_ref):
    @pl.when(pl.program_id(0) == 0)
    def _():
        o_ref[...] = jnp.broadcast_to(b_ref[...], o_ref.shape)

    c = c_ref[...]
    s_hi, s_lo = _split_bf16(c * _sigmoid(c))
    w_hi, w_lo = _split_bf16(w_ref[...])
    r = _dot(jnp.concatenate([s_hi, s_lo], axis=0), w_hi)
    o_ref[...] += r[:N_MOD_ROWS] + r[N_MOD_ROWS:] + _dot(s_hi, w_lo)


def _ada(cvec, w, b):
    k, n = w.shape
    return pl.pallas_call(
        _ada_kernel,
        grid=(k // ADA_TK,),
        in_specs=[pl.BlockSpec((N_MOD_ROWS, ADA_TK), lambda j: (0, j)),
                  pl.BlockSpec((ADA_TK, n), lambda j: (j, 0)),
                  pl.BlockSpec((1, n), lambda j: (0, 0))],
        out_specs=pl.BlockSpec((N_MOD_ROWS, n), lambda j: (0, 0)),
        out_shape=jax.ShapeDtypeStruct((N_MOD_ROWS, n), F32),
        compiler_params=pltpu.CompilerParams(dimension_semantics=("arbitrary",)),
        name="ada_mod",
    )(cvec, w, b)


W_CHUNK_ROWS = {D_IN: 32, D_MODEL: 128, D_FF: 64}
RING_SLOTS = 6
MAX_IN_FLIGHT = 8
SERVICE_EVERY = 8
CTX_GROUP = 1


class _WeightStream:
    def __init__(self, srcs, dsts, stages, sems, on_done):
        self.srcs, self.dsts, self.stages, self.sems, self.on_done = srcs, dsts, stages, sems, on_done
        self.tasks, ring = [], {}
        for w, src in enumerate(srcs):
            n_rows, cols = src.shape
            rows = W_CHUNK_ROWS[cols]
            for r0 in range(0, n_rows, rows):
                pos = len(ring.setdefault(cols, []))
                prev = ring[cols][pos - RING_SLOTS] if pos >= RING_SLOTS else -1
                ring[cols].append(len(self.tasks))
                self.tasks.append((w, r0, rows, cols, pos % RING_SLOTS, prev, r0 + rows == n_rows))
        self.retired = self.started = 0
        self.done = [False] * len(srcs)
        self._start_ready()

    def _copy(self, task):
        w, r0, rows, cols, slot, _, _ = task
        return pltpu.make_async_copy(self.srcs[w].at[pl.ds(r0, rows)], self.stages[cols].at[slot],
                                     self.sems[cols].at[slot])

    def _start_ready(self):
        while (self.started < len(self.tasks) and self.started < self.retired + MAX_IN_FLIGHT
               and self.tasks[self.started][5] < self.retired):
            self._copy(self.tasks[self.started]).start()
            self.started += 1

    def retire(self, n=1):
        for _ in range(n):
            if self.retired == len(self.tasks):
                return
            task = self.tasks[self.retired]
            w, r0, rows, cols, slot, _, last = task
            self._copy(task).wait()
            self.dsts[w][pl.ds(r0, rows), :] = self.stages[cols][slot].astype(BF16)
            self.retired += 1
            if last:
                self.done[w] = True
                self.on_done(w)
            self._start_ready()

    def retire_through(self, w):
        while not self.done[w]:
            self.retire()


def _ctx_kernel(sink_ref, x_ref, mod_ref, gattn_ref, nrm_ref, gf_ref,
                win_hbm, woa_hbm, wob_hbm, wo_hbm, wg_hbm, wu_hbm, wd_hbm,
                y_ref, ka_ref, va_ref, kb_ref, vb_ref,
                win_out, woa_out, wob_out, wo_out, wg_out, wu_out, wd_out,
                win_ref, woa_ref, wob_ref, wo_ref, wg_ref, wu_ref, wd_ref, out_sem):
    srcs = (win_hbm, woa_hbm, wob_hbm, wo_hbm, wg_hbm, wu_hbm, wd_hbm)
    dsts = (win_ref, woa_ref, wob_ref, wo_ref, wg_ref, wu_ref, wd_ref)
    outs = (win_out, woa_out, wob_out, wo_out, wg_out, wu_out, wd_out)
    first = pl.program_id(0) == 0

    def out_copy(i):
        return pltpu.make_async_copy(dsts[i], outs[i], out_sem.at[i])

    def layer():
        return _ctx_layer_steps(sink_ref, x_ref, mod_ref, gattn_ref, nrm_ref, gf_ref, dsts,
                                y_ref, ka_ref, va_ref, kb_ref, vb_ref)

    @pl.when(first)
    def _():
        widths = tuple(W_CHUNK_ROWS)

        def scoped(*refs):
            stream = _WeightStream(srcs, dsts, dict(zip(widths, refs[:len(widths)])),
                                   dict(zip(widths, refs[len(widths):])), lambda w: out_copy(w).start())
            pieces = 0
            for need in layer():
                if need is not None:
                    stream.retire_through(need)
                else:
                    pieces += 1
                    if pieces % SERVICE_EVERY == 0:
                        stream.retire(SERVICE_EVERY)

        pl.run_scoped(scoped,
                      *[pltpu.VMEM((RING_SLOTS, W_CHUNK_ROWS[c], c), F32) for c in widths],
                      *[pltpu.SemaphoreType.DMA((RING_SLOTS,)) for _ in widths])
        for i in range(N_WEIGHTS):
            out_copy(i).wait()

    @pl.when(jnp.logical_not(first))
    def _():
        _drain(layer())


def _ctx_layer_steps(sink_ref, x_ref, mod_ref, gattn_ref, nrm_ref, gf_ref, w,
                     y_ref, ka_ref, va_ref, kb_ref, vb_ref):
    x = x_ref[0]
    h = _rms_mod(x, gattn_ref[...], _mod(mod_ref, 0, 0), _mod(mod_ref, 0, 1)).astype(BF16)
    yield W_IN
    proj = []
    yield from _project_steps(h, w[W_IN], nrm_ref, proj)
    qa, ka, va, qb, kb, vb, ga, gb = proj
    ka_ref[0] = ka.T
    va_ref[0] = va.T
    kb_ref[0] = kb.T
    vb_ref[0] = vb.T

    lo_mask = lax.broadcasted_iota(jnp.int32, (TOK, LANES), 1) < HEAD_DIM
    scale = HEAD_DIM ** -0.5
    qa = (qa * scale).astype(BF16)
    qb = (qb * scale).astype(BF16)

    k_pl = [t.astype(BF16) for t in _place_halves(ka, lo_mask)]
    v_pl = [t.astype(BF16) for t in _place_halves(va, lo_mask)]
    ya = []
    for g in range(0, A_HEADS, CTX_GROUP):
        score_fns, values, sinks = [], [], []
        for head in range(g, g + CTX_GROUP):
            c, half = divmod(head, 2)
            i = 2 * (c // 2) + half
            score_fns.append(lambda c=c, i=i: [_dot_nt(qa[:, c * LANES:(c + 1) * LANES], k_pl[i])])
            values.append([(v_pl[i], False)])
            sinks.append(sink_ref[head])
        yield from _attend_heads_steps(score_fns, values, sinks, ya)
    ya = _merge_head_pairs(ya)

    yb = []
    for g in range(0, B_HEADS, CTX_GROUP):
        score_fns, values = [], []
        for head in range(g, g + CTX_GROUP):
            c, half = divmod(head, 2)
            sl = slice(c * LANES, (c + 1) * LANES)
            hm = lo_mask if half == 0 else jnp.logical_not(lo_mask)
            km = jnp.where(hm, kb[:, sl], 0.0).astype(BF16)
            vm = jnp.where(hm, vb[:, sl], 0.0).astype(BF16)
            score_fns.append(lambda sl=sl, km=km: [_dot_nt(qb[:, sl], km)])
            values.append([(vm, False)])
        yield from _attend_heads_steps(score_fns, values, [None] * CTX_GROUP, yb)
    yb = _merge_head_pairs(yb)

    yield from _tail_steps(x, ya, yb, ga, gb, mod_ref, 0, gf_ref, w, y_ref)


def _const_spec(shape):
    nd = len(shape)
    return pl.BlockSpec(shape, lambda *_: (0,) * nd, pipeline_mode=pl.Buffered(1))


def _ctx_layer(x, mods, sink, gattn, nrm, gf, weights):
    nb = x.shape[0]
    tok_spec = lambda w: pl.BlockSpec((1, SEQ, w), lambda b: (b, 0, 0))
    fm_spec = lambda w: pl.BlockSpec((1, w, SEQ), lambda b: (b, 0, 0))
    any_spec = pl.BlockSpec(memory_space=pl.ANY)
    in_specs = [
        pl.BlockSpec(memory_space=pltpu.SMEM),
        tok_spec(D_MODEL),
        _const_spec(mods.shape), _const_spec(gattn.shape), _const_spec(nrm.shape), _const_spec(gf.shape),
    ] + [any_spec] * len(weights)
    out_specs = ([tok_spec(D_MODEL), fm_spec(A_KV), fm_spec(A_KV), fm_spec(B_W), fm_spec(B_W)]
                 + [any_spec] * len(weights))
    out_shape = ([jax.ShapeDtypeStruct((nb, SEQ, D_MODEL), F32)]
                 + [jax.ShapeDtypeStruct((nb, w, SEQ), F32) for w in (A_KV, A_KV, B_W, B_W)]
                 + [jax.ShapeDtypeStruct(w.shape, BF16) for w in weights])
    scratch = [pltpu.VMEM(w.shape, BF16) for w in weights] + [pltpu.SemaphoreType.DMA((len(weights),))]
    return pl.pallas_call(
        _ctx_kernel,
        grid=(nb,),
        in_specs=in_specs,
        out_specs=out_specs,
        out_shape=out_shape,
        scratch_shapes=scratch,
        compiler_params=pltpu.CompilerParams(dimension_semantics=("arbitrary",), vmem_limit_bytes=VMEM_LIMIT),
        name="ctx_layer",
    )(sink, x, mods, gattn, nrm, gf, *weights)


def _rope(x, cos, sin, bit16):
    swapped = jnp.where(bit16, pltpu.roll(x, 16, 1), pltpu.roll(x, LANES - 16, 1))
    return x * cos + swapped * sin


def _lat_proj_kernel(x_ref, mod_ref, gattn_ref, win_ref, nrm_ref, cos_ref, sin_ref,
                     qa_ref, ka_ref, va_ref, qb_ref, kb_ref, vb_ref, ga_ref, gb_ref):
    row = 1 + pl.program_id(0)
    x = x_ref[0]
    h = _rms_mod(x, gattn_ref[...], _mod(mod_ref, row, 0), _mod(mod_ref, row, 1)).astype(BF16)
    qa, ka, va, qb, kb, vb, ga, gb = _project(h, win_ref, nrm_ref)
    lane = lax.broadcasted_iota(jnp.int32, (TOK, LANES), 1)
    lo_mask = lane < HEAD_DIM
    bit16 = (lane & 16) != 0
    cos, sin = cos_ref[...], sin_ref[...]
    scale = HEAD_DIM ** -0.5
    for c in range(A_Q // LANES):
        sl = slice(c * LANES, (c + 1) * LANES)
        qa_ref[0, :, sl] = (_rope(qa[:, sl], cos, sin, bit16) * scale).astype(BF16)
    ka = _rope(ka, cos, sin, bit16)
    for i, t in enumerate(_place_halves(ka, lo_mask)):
        ka_ref[0, :, i * LANES:(i + 1) * LANES] = t.astype(BF16)
    for i, t in enumerate(_place_halves(va, lo_mask)):
        va_ref[0, :, i * LANES:(i + 1) * LANES] = t.astype(BF16)
    qb_ref[0] = (qb * scale).astype(BF16)
    kb_ref[0] = kb.astype(BF16)
    vb_ref[0] = vb.astype(BF16)
    ga_ref[0] = ga
    gb_ref[0] = gb


def _lat_proj(x, mods, gattn, win, nrm, cos, sin):
    nb, n, _ = x.shape
    nt = n // TOK
    tok_spec = lambda w: pl.BlockSpec((1, TOK, w), lambda b, t: (b, t, 0))
    in_specs = [
        tok_spec(D_MODEL),
        _const_spec(mods.shape), _const_spec(gattn.shape), _const_spec(win.shape), _const_spec(nrm.shape),
        pl.BlockSpec((TOK, LANES), lambda b, t: (t, 0)),
        pl.BlockSpec((TOK, LANES), lambda b, t: (t, 0)),
    ]
    widths = (A_Q, 4 * LANES, 4 * LANES, B_W, B_W, B_W, D_MODEL, D_MODEL)
    dtypes = (BF16,) * 6 + (F32, F32)
    return pl.pallas_call(
        _lat_proj_kernel,
        grid=(nb, nt),
        in_specs=in_specs,
        out_specs=[tok_spec(w) for w in widths],
        out_shape=[jax.ShapeDtypeStruct((nb, n, w), dt) for w, dt in zip(widths, dtypes)],
        compiler_params=pltpu.CompilerParams(dimension_semantics=("arbitrary", "arbitrary"),
                                             vmem_limit_bytes=VMEM_LIMIT),
        name="lat_proj",
    )(x, mods, gattn, win, nrm, cos, sin)


def _build_na_bias(row_ref, bt_ref):
    w = lax.broadcasted_iota(jnp.int32, (GRID_W, LANES), 0)
    kcol = lax.broadcasted_iota(jnp.int32, (GRID_W, LANES), 1) & (GRID_W - 1)
    cs = jnp.clip(w - NA_COLS // 2, 0, GRID_W - NA_COLS)
    valid = (kcol >= cs) & (kcol < cs + NA_COLS)
    for h in range(B_HEADS):
        for d in range(2 * NA_ROWS - 1):
            rows = jnp.broadcast_to(row_ref[h, d:d + 1, :], (GRID_W, LANES))
            t = pltpu.roll(rows, LANES - (NA_COLS - 1), 1, stride=1, stride_axis=0)
            bt_ref[h, d] = jnp.where(valid, t, NEG)


A_KEYS = 2 * TOK
B_KROWS = 12
B_KEYS = B_KROWS * GRID_W
ATTN_GROUP = 4


def _lat_attention_steps(t, sink_ref, qa_ref, ka_ref, va_ref, qb_ref, kb_ref, vb_ref,
                         cka_ref, cva_ref, ckb_ref, cvb_ref, bt_ref, out):
    ks = pl.multiple_of(jnp.clip(TOK * t - A_WINDOW, 0, DEC_SEQ - A_KEYS), LANES)
    qpos = TOK * t + lax.broadcasted_iota(jnp.int32, (TOK, A_KEYS), 0)
    kpos = ks + lax.broadcasted_iota(jnp.int32, (TOK, A_KEYS), 1)
    a_valid = jnp.abs(qpos - kpos) <= A_WINDOW
    ck_t, cv_t = cka_ref[0], cva_ref[0]
    ck_pl = [_place_rows(ck_t, i // 2, i % 2).astype(BF16) for i in range(4)]
    cv_pl = [_place_rows(cv_t, i // 2, i % 2).astype(BF16) for i in range(4)]
    ya = []
    for g in range(0, A_HEADS, ATTN_GROUP):
        score_fns, values, sinks = [], [], []
        for head in range(g, g + ATTN_GROUP):
            c, half = divmod(head, 2)
            i = 2 * (c // 2) + half

            def a_scores(c=c, i=i):
                qc = qa_ref[0, :, c * LANES:(c + 1) * LANES]
                k_loc = ka_ref[0, pl.ds(ks, A_KEYS), i * LANES:(i + 1) * LANES]
                return [jnp.where(a_valid, _dot_nt(qc, k_loc), NEG), _dot(qc, ck_pl[i])]

            score_fns.append(a_scores)
            values.append([(va_ref[0, pl.ds(ks, A_KEYS), i * LANES:(i + 1) * LANES], False), (cv_pl[i], True)])
            sinks.append(sink_ref[head])
        outs = []
        yield from _attend_heads_steps(score_fns, values, sinks, outs)
        ya.append(_merge_head_pairs(outs))
    ya = jnp.concatenate(ya, axis=1)

    kr0 = jnp.clip(4 * t - 4, 0, DEC_SEQ // GRID_W - B_KROWS)
    kst = pl.multiple_of(kr0 * GRID_W, TOK)
    lane_lo = lax.broadcasted_iota(jnp.int32, (1, LANES), 1) < HEAD_DIM
    b_dr, b_valid = [], []
    for qi in range(TOK // GRID_W):
        r = 4 * t + qi
        rs = jnp.clip(r - NA_ROWS // 2, 0, DEC_SEQ // GRID_W - NA_ROWS)
        drs, valids = [], []
        for p in range(B_KROWS // 2):
            kr = kr0 + 2 * p
            drs.append(jnp.clip(kr - r + (NA_ROWS - 1), 0, 2 * NA_ROWS - 2))
            ok = [((k >= rs) & (k < rs + NA_ROWS)).astype(jnp.int32) for k in (kr, kr + 1)]
            valids.append(jnp.where(lane_lo, ok[0], ok[1]) != 0)
        b_dr.append(drs)
        b_valid.append(valids)
    lo_k = lax.broadcasted_iota(jnp.int32, (B_KEYS, LANES), 1) < HEAD_DIM
    lo_f = lax.broadcasted_iota(jnp.int32, (LANES, PAST_LEN), 0) < HEAD_DIM
    zero = jnp.zeros((), BF16)
    yb = []
    for g in range(0, B_HEADS, ATTN_GROUP):
        score_fns, values = [], []
        for head in range(g, g + ATTN_GROUP):
            c, half = divmod(head, 2)
            sl = slice(c * LANES, (c + 1) * LANES)
            hm_k = lo_k if half == 0 else jnp.logical_not(lo_k)
            hm_f = lo_f if half == 0 else jnp.logical_not(lo_f)

            def b_scores(head=head, sl=sl, hm_k=hm_k, hm_f=hm_f):
                qc = qb_ref[0, :, sl]
                kk = jnp.where(hm_k, kb_ref[0, pl.ds(kst, B_KEYS), sl], zero)
                kc = jnp.where(hm_f, ckb_ref[0, sl, :], 0.0).astype(BF16)
                s = _dot_nt(qc, kk)
                rows = []
                for qi in range(TOK // GRID_W):
                    blocks = []
                    for p in range(B_KROWS // 2):
                        blk = s[qi * GRID_W:(qi + 1) * GRID_W, p * LANES:(p + 1) * LANES] + bt_ref[head, b_dr[qi][p]]
                        blocks.append(jnp.where(b_valid[qi][p], blk, NEG))
                    rows.append(jnp.concatenate(blocks, axis=1))
                return [jnp.concatenate(rows, axis=0), _dot(qc, kc)]

            score_fns.append(b_scores)
            vv = jnp.where(hm_k, vb_ref[0, pl.ds(kst, B_KEYS), sl], zero)
            vc = jnp.where(hm_f, cvb_ref[0, sl, :], 0.0).astype(BF16)
            values.append([(vv, False), (vc, True)])
        outs = []
        yield from _attend_heads_steps(score_fns, values, [None] * ATTN_GROUP, outs)
        yb.append(_merge_head_pairs(outs))
    out += [ya, jnp.concatenate(yb, axis=1)]


def _lat_attn_kernel(sink_ref, x_ref, mod_ref, qa_ref, ka_ref, va_ref, qb_ref, kb_ref, vb_ref,
                     ga_ref, gb_ref, cka_ref, cva_ref, ckb_ref, cvb_ref, rpb_ref,
                     woa_ref, wob_ref, wo_ref, gf_ref, wg_ref, wu_ref, wd_ref, y_ref, bt_ref):
    i = pl.program_id(0)
    tiles_per_seq = DEC_SEQ // TOK

    @pl.when(i == 0)
    def _():
        _build_na_bias(rpb_ref, bt_ref)

    mixed = []
    _drain(_lat_attention_steps(i % tiles_per_seq, sink_ref, qa_ref, ka_ref, va_ref, qb_ref, kb_ref, vb_ref,
                                cka_ref, cva_ref, ckb_ref, cvb_ref, bt_ref, mixed))
    ya, yb = mixed
    weights = (None, woa_ref, wob_ref, wo_ref, wg_ref, wu_ref, wd_ref)
    _drain(_tail_steps(x_ref[0], ya, yb, ga_ref[0], gb_ref[0], mod_ref, 1 + i // tiles_per_seq, gf_ref,
                       weights, y_ref))


def _lat_attn(sink, x, mods, qa, ka4, va4, qb, kb, vb, ga, gb, cka, cva, ckb, cvb, rpb_rows,
              woa, wob, wo, gf, wg, wu, wd):
    nb, n, _ = x.shape
    nt = n // TOK
    n_tiles = nb * nt
    tok_spec = lambda w: pl.BlockSpec((1, TOK, w), lambda i: (i // nt, i % nt, 0))
    seq_spec = lambda rows, w: pl.BlockSpec((1, rows, w), lambda i: (i // nt, 0, 0),
                                            pipeline_mode=pl.Buffered(1))
    in_specs = [
        pl.BlockSpec(memory_space=pltpu.SMEM),
        tok_spec(D_MODEL),
        _const_spec(mods.shape),
        tok_spec(A_Q), seq_spec(n, 4 * LANES), seq_spec(n, 4 * LANES),
        tok_spec(B_W), seq_spec(n, B_W), seq_spec(n, B_W),
        tok_spec(D_MODEL), tok_spec(D_MODEL),
        seq_spec(A_KV, PAST_LEN), seq_spec(A_KV, PAST_LEN), seq_spec(B_W, PAST_LEN), seq_spec(B_W, PAST_LEN),
        _const_spec(rpb_rows.shape),
        _const_spec(woa.shape), _const_spec(wob.shape), _const_spec(wo.shape), _const_spec(gf.shape),
        _const_spec(wg.shape), _const_spec(wu.shape), _const_spec(wd.shape),
    ]
    return pl.pallas_call(
        _lat_attn_kernel,
        grid=(n_tiles,),
        in_specs=in_specs,
        out_specs=tok_spec(D_MODEL),
        out_shape=jax.ShapeDtypeStruct((nb, n, D_MODEL), F32),
        scratch_shapes=[pltpu.VMEM((B_HEADS, 2 * NA_ROWS - 1, GRID_W, LANES), F32)],
        compiler_params=pltpu.CompilerParams(dimension_semantics=("arbitrary",),
                                             vmem_limit_bytes=VMEM_LIMIT),
        name="lat_attn",
    )(sink, x, mods, qa, ka4, va4, qb, kb, vb, ga, gb, cka, cva, ckb, cvb, rpb_rows,
      woa, wob, wo, gf, wg, wu, wd)


def _rope_tables(n):
    half = HEAD_DIM // 4
    inv_freq = 1.0 / (ROPE_BASE ** (np.arange(half, dtype=np.float64) / half))
    t = np.arange(n)
    parts_c, parts_s = [], []
    for pos in (t // GRID_W, t % GRID_W):
        ang = pos.astype(np.float64)[:, None] * inv_freq[None, :]
        c, s = np.cos(ang), np.sin(ang)
        parts_c += [c, c]
        parts_s += [-s, s]
    cos = np.concatenate(parts_c * 2, axis=1).astype(np.float32)
    sin = np.concatenate(parts_s * 2, axis=1).astype(np.float32)
    return jnp.asarray(cos), jnp.asarray(sin)


def _feature_major(cache, l):
    b, _, s, h, d = cache.shape
    return jnp.transpose(cache[:, l], (0, 2, 3, 1)).reshape(b, h * d, s)


def _token_major(x, heads):
    b, _, s = x.shape
    return jnp.transpose(x.reshape(b, 1, heads, HEAD_DIM, s), (0, 1, 4, 2, 3))


def kernel(x_prompt, x_sample, cache_a_k, cache_a_v, cache_b_k, cache_b_v, c, c_ctx, w_ada, b_ada, g_attn, w_in, q_norm_a, k_norm_a, q_norm_b, k_norm_b, sink_a, rpb_b, w_out_a, w_out_b, w_o, g_ffn, w_gate, w_up, w_down):
    nd, n, _ = x_sample.shape
    l = 0

    cvec = jnp.concatenate([c_ctx[None, :], c, jnp.zeros((N_MOD_ROWS - 1 - nd, D_MODEL), F32)], axis=0)
    mods = _ada(cvec, w_ada[l], b_ada[l][None, :])

    gattn, gf = g_attn[l][None, :], g_ffn[l][None, :]
    nrm = jnp.stack([jnp.tile(v[l], A_Q // HEAD_DIM) for v in (q_norm_a, k_norm_a, q_norm_b, k_norm_b)])
    sink = sink_a[l]
    weights = (w_in[l], w_out_a[l], w_out_b[l], w_o[l], w_gate[l], w_up[l], w_down[l])

    (y_prompt, ka, va, kb, vb, win, woa, wob, wo, wg, wu, wd) = _ctx_layer(
        x_prompt, mods, sink, gattn, nrm, gf, weights)
    new_a_k = _token_major(ka, A_KV_HEADS)
    new_a_v = _token_major(va, A_KV_HEADS)
    new_b_k = _token_major(kb, B_HEADS)
    new_b_v = _token_major(vb, B_HEADS)

    cos, sin = _rope_tables(n)
    qa, ka4, va4, qb, lkb, lvb, ga, gb = _lat_proj(x_sample, mods, gattn, win, nrm, cos, sin)

    rp = jnp.pad(rpb_b[l], ((0, 0), (0, 1), (0, HEAD_DIM - (2 * NA_COLS - 1))))
    rpb_rows = jnp.concatenate([rp[:, :-1], rp[:, 1:]], axis=-1)

    y_sample = _lat_attn(sink, x_sample, mods, qa, ka4, va4, qb, lkb, lvb, ga, gb,
                         _feature_major(cache_a_k, l), _feature_major(cache_a_v, l),
                         _feature_major(cache_b_k, l), _feature_major(cache_b_v, l),
                         rpb_rows, woa, wob, wo, gf, wg, wu, wd)
    return (y_prompt, y_sample, new_a_k, new_a_v, new_b_k, new_b_v)
```

```python
import functools

import numpy as np

import jax
import jax.numpy as jnp
from jax import lax
from jax.experimental import pallas as pl
from jax.experimental.pallas import tpu as pltpu

D_MODEL = 1024
SEQ = 256
DEC_SEQ = 1024
PAST_LEN = 256
GRID_W = 64
HEAD_DIM = 64
A_HEADS = 8
A_KV_HEADS = 2
A_WINDOW = 128
B_HEADS = 8
NA_ROWS = 8
NA_COLS = 16
D_FF = 2816
ROPE_BASE = 10000.0
RMS_EPS = 1e-6
NEG = -1e30

A_Q = A_HEADS * HEAD_DIM
A_KV = A_KV_HEADS * HEAD_DIM
B_W = B_HEADS * HEAD_DIM
O_QA, O_KA, O_VA = 0, A_Q, A_Q + A_KV
O_QB = A_Q + 2 * A_KV
O_KB, O_VB = O_QB + B_W, O_QB + 2 * B_W
O_GA = O_QB + 3 * B_W
O_GB = O_GA + D_MODEL
D_IN = O_GB + D_MODEL

LANES = 128
TOK = 256
VMEM_LIMIT = 58 * 1024 * 1024
N_MOD_ROWS = 8

F32 = jnp.float32
BF16 = jnp.bfloat16


def _dot(a, b):
    return jnp.dot(a, b, preferred_element_type=F32)


def _dot_nt(a, b):
    return lax.dot_general(a, b, (((1,), (1,)), ((), ())), preferred_element_type=F32)


def _split_bf16(x):
    hi = x.astype(BF16)
    lo = (x - hi.astype(F32)).astype(BF16)
    return hi, lo


def _sigmoid(x):
    return 1.0 / (1.0 + jnp.exp(-x))


def _rms_mod(x, g, shift, scale):
    ms = jnp.mean(x * x, axis=-1, keepdims=True)
    return (x * lax.rsqrt(ms + RMS_EPS) * g) * (1.0 + scale) + shift


def _head_norm(u, w):
    lo_mask = lax.broadcasted_iota(jnp.int32, (u.shape[0], LANES), 1) < HEAD_DIM
    parts = []
    for c in range(u.shape[1] // LANES):
        uc = u[:, c * LANES:(c + 1) * LANES]
        uu = uc * uc
        ss_lo = jnp.sum(jnp.where(lo_mask, uu, 0.0), axis=-1, keepdims=True)
        ss_hi = jnp.sum(jnp.where(lo_mask, 0.0, uu), axis=-1, keepdims=True)
        ss = jnp.where(lo_mask, ss_lo, ss_hi)
        parts.append(uc * lax.rsqrt(ss * (1.0 / HEAD_DIM) + RMS_EPS))
    un = parts[0] if len(parts) == 1 else jnp.concatenate(parts, axis=1)
    return un * w


def _project_steps(h, win_ref, nrm_ref, out):
    bounds = (O_QA, O_KA, O_VA, O_QB, O_KB, O_VB, O_GA, O_GB, D_IN)
    norm_rows = {0: 0, 1: 1, 3: 2, 4: 3}
    for i in range(8):
        u = _dot(h, win_ref[:, bounds[i]:bounds[i + 1]])
        if i in norm_rows:
            r = norm_rows[i]
            u = _head_norm(u, nrm_ref[r:r + 1, :u.shape[1]])
        out.append(u)
        yield


def _project(h, win_ref, nrm_ref):
    out = []
    _drain(_project_steps(h, win_ref, nrm_ref, out))
    return out


def _place_halves(x, lo_mask):
    h0_lo = jnp.where(lo_mask, x, 0.0)
    h1_hi = jnp.where(lo_mask, 0.0, x)
    return h0_lo, pltpu.roll(h0_lo, 64, 1), pltpu.roll(h1_hi, 64, 1), h1_hi


def _place_rows(xt, head, half):
    rows = xt[head * HEAD_DIM:(head + 1) * HEAD_DIM]
    zero = jnp.zeros_like(rows)
    return jnp.concatenate([rows, zero] if half == 0 else [zero, rows], axis=0)


def _row_reduce(blocks, combine, lane_reduce):
    acc = None
    for s in blocks:
        for c in range(s.shape[1] // LANES):
            chunk = s[:, c * LANES:(c + 1) * LANES]
            acc = chunk if acc is None else combine(acc, chunk)
    return lane_reduce(acc, axis=-1, keepdims=True)


def _drain(steps):
    for _ in steps:
        pass


def _attend_heads_steps(score_fns, values, sinks, outs):
    scores = []
    for fn in score_fns:
        scores.append(fn())
        yield
    probs = []
    for blocks, sink in zip(scores, sinks):
        m = _row_reduce(blocks, jnp.maximum, jnp.max)
        if sink is not None:
            m = jnp.maximum(m, sink)
        ps = [jnp.exp(s - m) for s in blocks]
        l = _row_reduce(ps, jnp.add, jnp.sum)
        if sink is not None:
            l = l + jnp.exp(sink - m)
        probs.append(([p.astype(BF16) for p in ps], 1.0 / l))
        yield
    for (ps, inv_l), vals in zip(probs, values):
        out = None
        for p, (v, v_fm) in zip(ps, vals):
            o = _dot_nt(p, v) if v_fm else _dot(p, v)
            out = o if out is None else out + o
        outs.append(out * inv_l)
        yield


def _attend_heads(score_fns, values, sinks):
    outs = []
    _drain(_attend_heads_steps(score_fns, values, sinks, outs))
    return outs


def _merge_head_pairs(outs):
    return jnp.concatenate([outs[i] + outs[i + 1] for i in range(0, len(outs), 2)], axis=1)


def _mod(mod_ref, row, i):
    return mod_ref[pl.ds(row, 1), i * D_MODEL:(i + 1) * D_MODEL]


W_IN, W_OUT_A, W_OUT_B, W_O, W_GATE, W_UP, W_DOWN = range(7)
N_WEIGHTS = 7


def _tail_steps(x, ya, yb, ga, gb, mod_ref, row, gf_ref, w, y_ref):
    gt1, sh2, sc2, gt2 = (_mod(mod_ref, row, i) for i in (2, 3, 4, 5))
    yield W_OUT_A
    ma = _dot(ya.astype(BF16), w[W_OUT_A][...])
    yield W_OUT_B
    mb = _dot(yb.astype(BF16), w[W_OUT_B][...])
    mg = _sigmoid(ga) * ma + _sigmoid(gb) * mb
    yield W_O
    x1 = x + gt1 * _dot(mg.astype(BF16), w[W_O][...])
    h2 = _rms_mod(x1, gf_ref[...], sh2, sc2).astype(BF16)
    yield W_GATE
    gate = _dot(h2, w[W_GATE][...])
    yield W_UP
    up = _dot(h2, w[W_UP][...])
    act = (gate * _sigmoid(gate)) * up
    yield W_DOWN
    y_ref[0] = x1 + gt2 * _dot(act.astype(BF16), w[W_DOWN][...])


ADA_TK = 128


def _ada_kernel(c_ref, w_ref, b_ref, o_ref):
    @pl.when(pl.program_id(0) == 0)
    def _():
        o_ref[...] = jnp.broadcast_to(b_ref[...], o_ref.shape)

    c = c_ref[...]
    s_hi, s_lo = _split_bf16(c * _sigmoid(c))
    w_hi, w_lo = _split_bf16(w_ref[...])
    r = _dot(jnp.concatenate([s_hi, s_lo], axis=0), w_hi)
    o_ref[...] += r[:N_MOD_ROWS] + r[N_MOD_ROWS:] + _dot(s_hi, w_lo)


def _ada(cvec, w, b):
    k, n = w.shape
    return pl.pallas_call(
        _ada_kernel,
        grid=(k // ADA_TK,),
        in_specs=[pl.BlockSpec((N_MOD_ROWS, ADA_TK), lambda j: (0, j)),
                  pl.BlockSpec((ADA_TK, n), lambda j: (j, 0)),
                  pl.BlockSpec((1, n), lambda j: (0, 0))],
        out_specs=pl.BlockSpec((N_MOD_ROWS, n), lambda j: (0, 0)),
        out_shape=jax.ShapeDtypeStruct((N_MOD_ROWS, n), F32),
        compiler_params=pltpu.CompilerParams(dimension_semantics=("arbitrary",)),
        name="ada_mod",
    )(cvec, w, b)


W_CHUNK_ROWS = {D_IN: 32, D_MODEL: 128, D_FF: 64}
RING_SLOTS = 6
MAX_IN_FLIGHT = 6
CTX_GROUP = 1


class _WeightStream:
    def __init__(self, srcs, stages, sems, vmem_dsts=None, hbm_dsts=None, out_stages=None, out_sems=None):
        self.srcs, self.stages, self.sems = srcs, stages, sems
        self.vmem_dsts, self.hbm_dsts, self.out_stages, self.out_sems = vmem_dsts, hbm_dsts, out_stages, out_sems
        self.tasks, self.rings = [], {}
        for w, src in enumerate(srcs):
            n_rows, cols = src.shape
            rows = W_CHUNK_ROWS[cols]
            for r0 in range(0, n_rows, rows):
                ring = self.rings.setdefault(cols, [])
                prev = ring[len(ring) - RING_SLOTS] if len(ring) >= RING_SLOTS else -1
                self.tasks.append((w, r0, rows, cols, len(ring) % RING_SLOTS, prev))
                ring.append(len(self.tasks) - 1)

    def plan(self, visits):
        plans, retired, started = [], 0, 0
        for n in visits:
            ops = []
            todo = len(self.tasks) - retired if n is None else min(n, len(self.tasks) - retired)
            for step in range(todo + 1):
                while (started < len(self.tasks) and started < retired + MAX_IN_FLIGHT
                       and self.tasks[started][5] < retired):
                    ops.append(("start", started))
                    started += 1
                if step < todo:
                    ops.append(("retire", retired))
                    retired += 1
            plans.append(ops)
        assert retired == len(self.tasks) and started == len(self.tasks)
        return plans

    def _in_copy(self, t):
        w, r0, rows, cols, slot, _ = self.tasks[t]
        return pltpu.make_async_copy(self.srcs[w].at[pl.ds(r0, rows)], self.stages[cols].at[slot],
                                     self.sems[cols].at[slot])

    def _out_copy(self, t):
        w, r0, rows, cols, slot, _ = self.tasks[t]
        return pltpu.make_async_copy(self.out_stages[cols].at[slot], self.hbm_dsts[w].at[pl.ds(r0, rows)],
                                     self.out_sems[cols].at[slot])

    def emit(self, ops):
        for kind, t in ops:
            if kind == "start":
                self._in_copy(t).start()
                continue
            w, r0, rows, cols, slot, prev = self.tasks[t]
            self._in_copy(t).wait()
            chunk = self.stages[cols][slot].astype(BF16)
            if self.out_stages is None:
                self.vmem_dsts[w][pl.ds(r0, rows), :] = chunk
            else:
                if prev >= 0:
                    self._out_copy(prev).wait()
                self.out_stages[cols][slot] = chunk
                self._out_copy(t).start()

    def emit_flush(self):
        for ring in self.rings.values():
            for t in ring[-RING_SLOTS:]:
                self._out_copy(t).wait()


def _stream_scratch(widths, dtype):
    return ([pltpu.VMEM((RING_SLOTS, W_CHUNK_ROWS[c], c), dtype) for c in widths]
            + [pltpu.SemaphoreType.DMA((RING_SLOTS,)) for _ in widths])


def _trunk_kernel(n_ctx, sink_ref, x_ref, mod_ref, gattn_ref, nrm_ref, gf_ref, *refs):
    w_hbm, refs = refs[:N_WEIGHTS], refs[N_WEIGHTS:]
    xs_ref, ya_in, yb_in, ga_in, gb_in, y_ref, ka_ref, va_ref, kb_ref, vb_ref, ys_ref = refs[:11]
    w, w_sem = refs[11:11 + N_WEIGHTS], refs[11 + N_WEIGHTS]
    i = pl.program_id(0)

    def w_copy(k):
        return pltpu.make_async_copy(w_hbm[k], w[k], w_sem.at[k])

    def layer():
        return _ctx_layer_steps(sink_ref, x_ref, mod_ref, gattn_ref, nrm_ref, gf_ref, w,
                                y_ref, ka_ref, va_ref, kb_ref, vb_ref)

    @pl.when(i == 0)
    def _():
        for k in range(N_WEIGHTS):
            w_copy(k).start()
        for need in layer():
            if need is not None:
                w_copy(need).wait()

    @pl.when((i > 0) & (i < n_ctx))
    def _():
        _drain(layer())

    @pl.when(i >= n_ctx)
    def _():
        row = 1 + (i - n_ctx) // (DEC_SEQ // TOK)
        _drain(_tail_steps(xs_ref[0], ya_in[0], yb_in[0], ga_in[0], gb_in[0], mod_ref, row, gf_ref, w, ys_ref))


def _ctx_layer_steps(sink_ref, x_ref, mod_ref, gattn_ref, nrm_ref, gf_ref, w,
                     y_ref, ka_ref, va_ref, kb_ref, vb_ref):
    x = x_ref[0]
    h = _rms_mod(x, gattn_ref[...], _mod(mod_ref, 0, 0), _mod(mod_ref, 0, 1)).astype(BF16)
    yield W_IN
    proj = []
    yield from _project_steps(h, w[W_IN], nrm_ref, proj)
    qa, ka, va, qb, kb, vb, ga, gb = proj
    ka_ref[0] = ka.T
    va_ref[0] = va.T
    kb_ref[0] = kb.T
    vb_ref[0] = vb.T

    lo_mask = lax.broadcasted_iota(jnp.int32, (TOK, LANES), 1) < HEAD_DIM
    scale = HEAD_DIM ** -0.5
    qa = (qa * scale).astype(BF16)
    qb = (qb * scale).astype(BF16)

    k_pl = [t.astype(BF16) for t in _place_halves(ka, lo_mask)]
    v_pl = [t.astype(BF16) for t in _place_halves(va, lo_mask)]
    ya = []
    for g in range(0, A_HEADS, CTX_GROUP):
        score_fns, values, sinks = [], [], []
        for head in range(g, g + CTX_GROUP):
            c, half = divmod(head, 2)
            i = 2 * (c // 2) + half
            score_fns.append(lambda c=c, i=i: [_dot_nt(qa[:, c * LANES:(c + 1) * LANES], k_pl[i])])
            values.append([(v_pl[i], False)])
            sinks.append(sink_ref[head])
        yield from _attend_heads_steps(score_fns, values, sinks, ya)
    ya = _merge_head_pairs(ya)

    yb = []
    for g in range(0, B_HEADS, CTX_GROUP):
        score_fns, values = [], []
        for head in range(g, g + CTX_GROUP):
            c, half = divmod(head, 2)
            sl = slice(c * LANES, (c + 1) * LANES)
            hm = lo_mask if half == 0 else jnp.logical_not(lo_mask)
            km = jnp.where(hm, kb[:, sl], 0.0).astype(BF16)
            vm = jnp.where(hm, vb[:, sl], 0.0).astype(BF16)
            score_fns.append(lambda sl=sl, km=km: [_dot_nt(qb[:, sl], km)])
            values.append([(vm, False)])
        yield from _attend_heads_steps(score_fns, values, [None] * CTX_GROUP, yb)
    yb = _merge_head_pairs(yb)

    yield from _tail_steps(x, ya, yb, ga, gb, mod_ref, 0, gf_ref, w, y_ref)


def _const_spec(shape):
    nd = len(shape)
    return pl.BlockSpec(shape, lambda *_: (0,) * nd, pipeline_mode=pl.Buffered(1))


def _trunk(x, mods, sink, gattn, nrm, gf, weights, xs, ya, yb, ga, gb):
    nb = x.shape[0]
    nd, n, _ = xs.shape
    nt = n // TOK
    ctx = lambda i: jnp.minimum(i, nb - 1)
    lat = lambda i: divmod(jnp.maximum(i - nb, 0), nt)
    tok_spec = lambda w: pl.BlockSpec((1, SEQ, w), lambda i: (ctx(i), 0, 0))
    fm_spec = lambda w: pl.BlockSpec((1, w, SEQ), lambda i: (ctx(i), 0, 0))
    lat_spec = lambda w: pl.BlockSpec((1, TOK, w), lambda i: (*lat(i), 0))
    in_specs = ([pl.BlockSpec(memory_space=pltpu.SMEM), tok_spec(D_MODEL),
                 _const_spec(mods.shape), _const_spec(gattn.shape), _const_spec(nrm.shape), _const_spec(gf.shape)]
                + [pl.BlockSpec(memory_space=pl.ANY)] * len(weights)
                + [lat_spec(D_MODEL), lat_spec(A_Q), lat_spec(B_W), lat_spec(D_MODEL), lat_spec(D_MODEL)])
    out_specs = [tok_spec(D_MODEL), fm_spec(A_KV), fm_spec(A_KV), fm_spec(B_W), fm_spec(B_W), lat_spec(D_MODEL)]
    out_shape = ([jax.ShapeDtypeStruct((nb, SEQ, D_MODEL), F32)]
                 + [jax.ShapeDtypeStruct((nb, w, SEQ), F32) for w in (A_KV, A_KV, B_W, B_W)]
                 + [jax.ShapeDtypeStruct((nd, n, D_MODEL), F32)])
    scratch = [pltpu.VMEM(w.shape, BF16) for w in weights] + [pltpu.SemaphoreType.DMA((len(weights),))]
    return pl.pallas_call(
        functools.partial(_trunk_kernel, nb),
        grid=(nb + nd * nt,),
        in_specs=in_specs,
        out_specs=out_specs,
        out_shape=out_shape,
        scratch_shapes=scratch,
        compiler_params=pltpu.CompilerParams(dimension_semantics=("arbitrary",), vmem_limit_bytes=VMEM_LIMIT),
        name="trunk",
    )(sink, x, mods, gattn, nrm, gf, *weights, xs, ya, yb, ga, gb)


def _rope(x, cos, sin, bit16):
    swapped = jnp.where(bit16, pltpu.roll(x, 16, 1), pltpu.roll(x, LANES - 16, 1))
    return x * cos + swapped * sin


def _lat_proj_kernel(x_ref, mod_ref, gattn_ref, win_hbm, nrm_ref, cos_ref, sin_ref,
                     qa_ref, ka_ref, va_ref, qb_ref, kb_ref, vb_ref, ga_ref, gb_ref, win_out,
                     win_ref, stage_ref, stage_sem, out_sem):
    first = (pl.program_id(0) == 0) & (pl.program_id(1) == 0)
    win_copy = pltpu.make_async_copy(win_ref, win_out, out_sem.at[0])

    @pl.when(first)
    def _():
        stream = _WeightStream((win_hbm,), {D_IN: stage_ref}, {D_IN: stage_sem}, vmem_dsts=(win_ref,))
        stream.emit(stream.plan([None])[0])
        win_copy.start()

    row = 1 + pl.program_id(0)
    x = x_ref[0]
    h = _rms_mod(x, gattn_ref[...], _mod(mod_ref, row, 0), _mod(mod_ref, row, 1)).astype(BF16)
    qa, ka, va, qb, kb, vb, ga, gb = _project(h, win_ref, nrm_ref)
    lane = lax.broadcasted_iota(jnp.int32, (TOK, LANES), 1)
    lo_mask = lane < HEAD_DIM
    bit16 = (lane & 16) != 0
    cos, sin = cos_ref[...], sin_ref[...]
    scale = HEAD_DIM ** -0.5
    for c in range(A_Q // LANES):
        sl = slice(c * LANES, (c + 1) * LANES)
        qa_ref[0, :, sl] = (_rope(qa[:, sl], cos, sin, bit16) * scale).astype(BF16)
    ka = _rope(ka, cos, sin, bit16)
    for i, t in enumerate(_place_halves(ka, lo_mask)):
        ka_ref[0, :, i * LANES:(i + 1) * LANES] = t.astype(BF16)
    for i, t in enumerate(_place_halves(va, lo_mask)):
        va_ref[0, :, i * LANES:(i + 1) * LANES] = t.astype(BF16)
    qb_ref[0] = (qb * scale).astype(BF16)
    kb_ref[0] = kb.astype(BF16)
    vb_ref[0] = vb.astype(BF16)
    ga_ref[0] = ga
    gb_ref[0] = gb

    @pl.when(first)
    def _():
        win_copy.wait()


def _lat_proj(x, mods, gattn, win, nrm, cos, sin):
    nb, n, _ = x.shape
    nt = n // TOK
    tok_spec = lambda w: pl.BlockSpec((1, TOK, w), lambda b, t: (b, t, 0))
    any_spec = pl.BlockSpec(memory_space=pl.ANY)
    in_specs = [
        tok_spec(D_MODEL),
        _const_spec(mods.shape), _const_spec(gattn.shape), any_spec, _const_spec(nrm.shape),
        pl.BlockSpec((TOK, LANES), lambda b, t: (t, 0)),
        pl.BlockSpec((TOK, LANES), lambda b, t: (t, 0)),
    ]
    widths = (A_Q, 4 * LANES, 4 * LANES, B_W, B_W, B_W, D_MODEL, D_MODEL)
    dtypes = (BF16,) * 6 + (F32, F32)
    return pl.pallas_call(
        _lat_proj_kernel,
        grid=(nb, nt),
        in_specs=in_specs,
        out_specs=[tok_spec(w) for w in widths] + [any_spec],
        out_shape=([jax.ShapeDtypeStruct((nb, n, w), dt) for w, dt in zip(widths, dtypes)]
                   + [jax.ShapeDtypeStruct(win.shape, BF16)]),
        scratch_shapes=[pltpu.VMEM(win.shape, BF16)] + _stream_scratch((D_IN,), F32)
                       + [pltpu.SemaphoreType.DMA((1,))],
        compiler_params=pltpu.CompilerParams(dimension_semantics=("arbitrary", "arbitrary"),
                                             vmem_limit_bytes=VMEM_LIMIT),
        name="lat_proj",
    )(x, mods, gattn, win, nrm, cos, sin)


def _build_na_bias(row_ref, bt_ref):
    w = lax.broadcasted_iota(jnp.int32, (GRID_W, LANES), 0)
    kcol = lax.broadcasted_iota(jnp.int32, (GRID_W, LANES), 1) & (GRID_W - 1)
    cs = jnp.clip(w - NA_COLS // 2, 0, GRID_W - NA_COLS)
    valid = (kcol >= cs) & (kcol < cs + NA_COLS)
    for h in range(B_HEADS):
        for d in range(2 * NA_ROWS - 1):
            rows = jnp.broadcast_to(row_ref[h, d:d + 1, :], (GRID_W, LANES))
            t = pltpu.roll(rows, LANES - (NA_COLS - 1), 1, stride=1, stride_axis=0)
            bt_ref[h, d] = jnp.where(valid, t, NEG)


A_KEYS = 2 * TOK
B_KROWS = 12
B_KEYS = B_KROWS * GRID_W
ATTN_GROUP = 4


def _lat_attention_steps(t, sink_ref, qa_ref, ka_ref, va_ref, qb_ref, kb_ref, vb_ref,
                         cka_ref, cva_ref, ckb_ref, cvb_ref, bt_ref, out):
    ks = pl.multiple_of(jnp.clip(TOK * t - A_WINDOW, 0, DEC_SEQ - A_KEYS), LANES)
    qpos = TOK * t + lax.broadcasted_iota(jnp.int32, (TOK, A_KEYS), 0)
    kpos = ks + lax.broadcasted_iota(jnp.int32, (TOK, A_KEYS), 1)
    a_valid = jnp.abs(qpos - kpos) <= A_WINDOW
    ck_t, cv_t = cka_ref[0], cva_ref[0]
    ck_pl = [_place_rows(ck_t, i // 2, i % 2).astype(BF16) for i in range(4)]
    cv_pl = [_place_rows(cv_t, i // 2, i % 2).astype(BF16) for i in range(4)]
    ya = []
    for g in range(0, A_HEADS, ATTN_GROUP):
        score_fns, values, sinks = [], [], []
        for head in range(g, g + ATTN_GROUP):
            c, half = divmod(head, 2)
            i = 2 * (c // 2) + half

            def a_scores(c=c, i=i):
                qc = qa_ref[0, :, c * LANES:(c + 1) * LANES]
                k_loc = ka_ref[0, pl.ds(ks, A_KEYS), i * LANES:(i + 1) * LANES]
                return [jnp.where(a_valid, _dot_nt(qc, k_loc), NEG), _dot(qc, ck_pl[i])]

            score_fns.append(a_scores)
            values.append([(va_ref[0, pl.ds(ks, A_KEYS), i * LANES:(i + 1) * LANES], False), (cv_pl[i], True)])
            sinks.append(sink_ref[head])
        outs = []
        yield from _attend_heads_steps(score_fns, values, sinks, outs)
        ya.append(_merge_head_pairs(outs))
    ya = jnp.concatenate(ya, axis=1)

    kr0 = jnp.clip(4 * t - 4, 0, DEC_SEQ // GRID_W - B_KROWS)
    kst = pl.multiple_of(kr0 * GRID_W, TOK)
    lane_lo = lax.broadcasted_iota(jnp.int32, (1, LANES), 1) < HEAD_DIM
    b_dr, b_valid = [], []
    for qi in range(TOK // GRID_W):
        r = 4 * t + qi
        rs = jnp.clip(r - NA_ROWS // 2, 0, DEC_SEQ // GRID_W - NA_ROWS)
        drs, valids = [], []
        for p in range(B_KROWS // 2):
            kr = kr0 + 2 * p
            drs.append(jnp.clip(kr - r + (NA_ROWS - 1), 0, 2 * NA_ROWS - 2))
            ok = [((k >= rs) & (k < rs + NA_ROWS)).astype(jnp.int32) for k in (kr, kr + 1)]
            valids.append(jnp.where(lane_lo, ok[0], ok[1]) != 0)
        b_dr.append(drs)
        b_valid.append(valids)
    lo_k = lax.broadcasted_iota(jnp.int32, (B_KEYS, LANES), 1) < HEAD_DIM
    lo_f = lax.broadcasted_iota(jnp.int32, (LANES, PAST_LEN), 0) < HEAD_DIM
    zero = jnp.zeros((), BF16)
    yb = []
    for g in range(0, B_HEADS, ATTN_GROUP):
        score_fns, values = [], []
        for head in range(g, g + ATTN_GROUP):
            c, half = divmod(head, 2)
            sl = slice(c * LANES, (c + 1) * LANES)
            hm_k = lo_k if half == 0 else jnp.logical_not(lo_k)
            hm_f = lo_f if half == 0 else jnp.logical_not(lo_f)

            def b_scores(head=head, sl=sl, hm_k=hm_k, hm_f=hm_f):
                qc = qb_ref[0, :, sl]
                kk = jnp.where(hm_k, kb_ref[0, pl.ds(kst, B_KEYS), sl], zero)
                kc = jnp.where(hm_f, ckb_ref[0, sl, :], 0.0).astype(BF16)
                s = _dot_nt(qc, kk)
                rows = []
                for qi in range(TOK // GRID_W):
                    blocks = []
                    for p in range(B_KROWS // 2):
                        blk = s[qi * GRID_W:(qi + 1) * GRID_W, p * LANES:(p + 1) * LANES] + bt_ref[head, b_dr[qi][p]]
                        blocks.append(jnp.where(b_valid[qi][p], blk, NEG))
                    rows.append(jnp.concatenate(blocks, axis=1))
                return [jnp.concatenate(rows, axis=0), _dot(qc, kc)]

            score_fns.append(b_scores)
            vv = jnp.where(hm_k, vb_ref[0, pl.ds(kst, B_KEYS), sl], zero)
            vc = jnp.where(hm_f, cvb_ref[0, sl, :], 0.0).astype(BF16)
            values.append([(vv, False), (vc, True)])
        outs = []
        yield from _attend_heads_steps(score_fns, values, [None] * ATTN_GROUP, outs)
        yb.append(_merge_head_pairs(outs))
    out += [ya, jnp.concatenate(yb, axis=1)]


MIX_VISITS = 3
MIX_WIDTHS = (D_MODEL, D_FF)


def _lat_mix_kernel(n_tiles, sink_ref, qa_ref, ka_ref, va_ref, qb_ref, kb_ref, vb_ref,
                    cka_ref, cva_ref, ckb_ref, cvb_ref, rpb_ref, *refs):
    n_w = N_WEIGHTS - 1
    w_hbm, ya_ref, yb_ref, w_out = refs[:n_w], refs[n_w], refs[n_w + 1], refs[n_w + 2:2 * n_w + 2]
    bt_ref = refs[2 * n_w + 2]
    nwid = len(MIX_WIDTHS)
    ring = refs[2 * n_w + 3:]
    stages, sems = dict(zip(MIX_WIDTHS, ring[:nwid])), dict(zip(MIX_WIDTHS, ring[nwid:2 * nwid]))
    out_stages = dict(zip(MIX_WIDTHS, ring[2 * nwid:3 * nwid]))
    out_sems = dict(zip(MIX_WIDTHS, ring[3 * nwid:4 * nwid]))
    i = pl.program_id(0)

    stream = _WeightStream(w_hbm, stages, sems, hbm_dsts=w_out, out_stages=out_stages, out_sems=out_sems)
    n_visits = n_tiles * MIX_VISITS
    per_visit = -(-len(stream.tasks) // (n_visits - 1))
    plans = stream.plan([per_visit] * (n_visits - 1) + [None])

    def visit(v):
        for step in range(n_tiles):
            @pl.when(i == step)
            def _():
                stream.emit(plans[step * MIX_VISITS + v])
                if step * MIX_VISITS + v == n_visits - 1:
                    stream.emit_flush()

    @pl.when(i == 0)
    def _():
        _build_na_bias(rpb_ref, bt_ref)

    mixed = []
    steps = _lat_attention_steps(i % (DEC_SEQ // TOK), sink_ref, qa_ref, ka_ref, va_ref, qb_ref, kb_ref, vb_ref,
                                 cka_ref, cva_ref, ckb_ref, cvb_ref, bt_ref, mixed)
    pieces = 3 * (A_HEADS + B_HEADS)
    for n, _ in enumerate(steps, 1):
        if n % (pieces // (MIX_VISITS + 1)) == 0 and n < pieces:
            visit(n // (pieces // (MIX_VISITS + 1)) - 1)
    ya_ref[0] = mixed[0].astype(BF16)
    yb_ref[0] = mixed[1].astype(BF16)


def _lat_mix(sink, qa, ka4, va4, qb, kb, vb, cka, cva, ckb, cvb, rpb_rows, weights):
    nb, n, _ = qa.shape
    nt = n // TOK
    n_tiles = nb * nt
    tok_spec = lambda w: pl.BlockSpec((1, TOK, w), lambda i: (i // nt, i % nt, 0))
    seq_spec = lambda rows, w: pl.BlockSpec((1, rows, w), lambda i: (i // nt, 0, 0),
                                            pipeline_mode=pl.Buffered(1))
    any_spec = pl.BlockSpec(memory_space=pl.ANY)
    in_specs = [
        pl.BlockSpec(memory_space=pltpu.SMEM),
        tok_spec(A_Q), seq_spec(n, 4 * LANES), seq_spec(n, 4 * LANES),
        tok_spec(B_W), seq_spec(n, B_W), seq_spec(n, B_W),
        seq_spec(A_KV, PAST_LEN), seq_spec(A_KV, PAST_LEN), seq_spec(B_W, PAST_LEN), seq_spec(B_W, PAST_LEN),
        _const_spec(rpb_rows.shape),
    ] + [any_spec] * len(weights)
    return pl.pallas_call(
        functools.partial(_lat_mix_kernel, n_tiles),
        grid=(n_tiles,),
        in_specs=in_specs,
        out_specs=[tok_spec(A_Q), tok_spec(B_W)] + [any_spec] * len(weights),
        out_shape=([jax.ShapeDtypeStruct((nb, n, A_Q), BF16), jax.ShapeDtypeStruct((nb, n, B_W), BF16)]
                   + [jax.ShapeDtypeStruct(w.shape, BF16) for w in weights]),
        scratch_shapes=([pltpu.VMEM((B_HEADS, 2 * NA_ROWS - 1, GRID_W, LANES), F32)]
                        + _stream_scratch(MIX_WIDTHS, F32) + _stream_scratch(MIX_WIDTHS, BF16)),
        compiler_params=pltpu.CompilerParams(dimension_semantics=("arbitrary",),
                                             vmem_limit_bytes=VMEM_LIMIT),
        name="lat_mix",
    )(sink, qa, ka4, va4, qb, kb, vb, cka, cva, ckb, cvb, rpb_rows, *weights)


def _rope_tables(n):
    half = HEAD_DIM // 4
    inv_freq = 1.0 / (ROPE_BASE ** (np.arange(half, dtype=np.float64) / half))
    t = np.arange(n)
    parts_c, parts_s = [], []
    for pos in (t // GRID_W, t % GRID_W):
        ang = pos.astype(np.float64)[:, None] * inv_freq[None, :]
        c, s = np.cos(ang), np.sin(ang)
        parts_c += [c, c]
        parts_s += [-s, s]
    cos = np.concatenate(parts_c * 2, axis=1).astype(np.float32)
    sin = np.concatenate(parts_s * 2, axis=1).astype(np.float32)
    return jnp.asarray(cos), jnp.asarray(sin)


def _feature_major(cache, l):
    b, _, s, h, d = cache.shape
    return jnp.transpose(cache[:, l], (0, 2, 3, 1)).reshape(b, h * d, s)


def _token_major(x, heads):
    b, _, s = x.shape
    return jnp.transpose(x.reshape(b, 1, heads, HEAD_DIM, s), (0, 1, 4, 2, 3))


def kernel(x_prompt, x_sample, cache_a_k, cache_a_v, cache_b_k, cache_b_v, c, c_ctx, w_ada, b_ada, g_attn, w_in, q_norm_a, k_norm_a, q_norm_b, k_norm_b, sink_a, rpb_b, w_out_a, w_out_b, w_o, g_ffn, w_gate, w_up, w_down):
    nd, n, _ = x_sample.shape
    l = 0

    cvec = jnp.concatenate([c_ctx[None, :], c, jnp.zeros((N_MOD_ROWS - 1 - nd, D_MODEL), F32)], axis=0)
    mods = _ada(cvec, w_ada[l], b_ada[l][None, :])

    gattn, gf = g_attn[l][None, :], g_ffn[l][None, :]
    nrm = jnp.stack([jnp.tile(v[l], A_Q // HEAD_DIM) for v in (q_norm_a, k_norm_a, q_norm_b, k_norm_b)])
    sink = sink_a[l]
    weights = (w_in[l], w_out_a[l], w_out_b[l], w_o[l], w_gate[l], w_up[l], w_down[l])

    cos, sin = _rope_tables(n)
    qa, ka4, va4, qb, lkb, lvb, ga, gb, win = _lat_proj(x_sample, mods, gattn, weights[W_IN], nrm, cos, sin)

    rp = jnp.pad(rpb_b[l], ((0, 0), (0, 1), (0, HEAD_DIM - (2 * NA_COLS - 1))))
    rpb_rows = jnp.concatenate([rp[:, :-1], rp[:, 1:]], axis=-1)
    ya, yb, *w_rest = _lat_mix(sink, qa, ka4, va4, qb, lkb, lvb,
                               _feature_major(cache_a_k, l), _feature_major(cache_a_v, l),
                               _feature_major(cache_b_k, l), _feature_major(cache_b_v, l),
                               rpb_rows, weights[1:])

    y_prompt, ka, va, kb, vb, y_sample = _trunk(x_prompt, mods, sink, gattn, nrm, gf, (win, *w_rest),
                                                x_sample, ya, yb, ga, gb)
    new_a_k = _token_major(ka, A_KV_HEADS)
    new_a_v = _token_major(va, A_KV_HEADS)
    new_b_k = _token_major(kb, B_HEADS)
    new_b_v = _token_major(vb, B_HEADS)
    return (y_prompt, y_sample, new_a_k, new_a_v, new_b_k, new_b_v)
```

```python
import numpy as np

import jax
import jax.numpy as jnp
from jax import lax
from jax.experimental import pallas as pl
from jax.experimental.pallas import tpu as pltpu

D_MODEL = 1024
SEQ = 256
DEC_SEQ = 1024
PAST_LEN = 256
GRID_W = 64
HEAD_DIM = 64
A_HEADS = 8
A_KV_HEADS = 2
A_WINDOW = 128
B_HEADS = 8
NA_ROWS = 8
NA_COLS = 16
D_FF = 2816
ROPE_BASE = 10000.0
RMS_EPS = 1e-6
NEG = -1e30

A_Q = A_HEADS * HEAD_DIM
A_KV = A_KV_HEADS * HEAD_DIM
B_W = B_HEADS * HEAD_DIM
O_QA, O_KA, O_VA = 0, A_Q, A_Q + A_KV
O_QB = A_Q + 2 * A_KV
O_KB, O_VB = O_QB + B_W, O_QB + 2 * B_W
O_GA = O_QB + 3 * B_W
O_GB = O_GA + D_MODEL
D_IN = O_GB + D_MODEL

LANES = 128
TOK = 256
VMEM_LIMIT = 58 * 1024 * 1024
N_MOD_ROWS = 8

F32 = jnp.float32
BF16 = jnp.bfloat16


def _dot(a, b):
    return jnp.dot(a, b, preferred_element_type=F32)


def _dot_nt(a, b):
    return lax.dot_general(a, b, (((1,), (1,)), ((), ())), preferred_element_type=F32)


def _split_bf16(x):
    hi = x.astype(BF16)
    lo = (x - hi.astype(F32)).astype(BF16)
    return hi, lo


def _sigmoid(x):
    return 1.0 / (1.0 + jnp.exp(-x))


def _rms_mod(x, g, shift, scale):
    ms = jnp.mean(x * x, axis=-1, keepdims=True)
    return (x * lax.rsqrt(ms + RMS_EPS) * g) * (1.0 + scale) + shift


def _head_norm(u, w):
    lo_mask = lax.broadcasted_iota(jnp.int32, (u.shape[0], LANES), 1) < HEAD_DIM
    parts = []
    for c in range(u.shape[1] // LANES):
        uc = u[:, c * LANES:(c + 1) * LANES]
        uu = uc * uc
        ss_lo = jnp.sum(jnp.where(lo_mask, uu, 0.0), axis=-1, keepdims=True)
        ss_hi = jnp.sum(jnp.where(lo_mask, 0.0, uu), axis=-1, keepdims=True)
        ss = jnp.where(lo_mask, ss_lo, ss_hi)
        parts.append(uc * lax.rsqrt(ss * (1.0 / HEAD_DIM) + RMS_EPS))
    un = parts[0] if len(parts) == 1 else jnp.concatenate(parts, axis=1)
    return un * w


def _project(h, win_ref, nrm_ref):
    qa = _head_norm(_dot(h, win_ref[:, O_QA:O_KA]), nrm_ref[0:1, :])
    ka = _head_norm(_dot(h, win_ref[:, O_KA:O_VA]), nrm_ref[1:2, :A_KV])
    va = _dot(h, win_ref[:, O_VA:O_QB])
    qb = _head_norm(_dot(h, win_ref[:, O_QB:O_KB]), nrm_ref[2:3, :])
    kb = _head_norm(_dot(h, win_ref[:, O_KB:O_VB]), nrm_ref[3:4, :])
    vb = _dot(h, win_ref[:, O_VB:O_GA])
    ga = _dot(h, win_ref[:, O_GA:O_GB])
    gb = _dot(h, win_ref[:, O_GB:D_IN])
    return qa, ka, va, qb, kb, vb, ga, gb


def _place_halves(x, lo_mask):
    h0_lo = jnp.where(lo_mask, x, 0.0)
    h1_hi = jnp.where(lo_mask, 0.0, x)
    return h0_lo, pltpu.roll(h0_lo, 64, 1), pltpu.roll(h1_hi, 64, 1), h1_hi


def _place_rows(xt, head, half):
    rows = xt[head * HEAD_DIM:(head + 1) * HEAD_DIM]
    zero = jnp.zeros_like(rows)
    return jnp.concatenate([rows, zero] if half == 0 else [zero, rows], axis=0)


def _row_reduce(blocks, combine, lane_reduce):
    acc = None
    for s in blocks:
        for c in range(s.shape[1] // LANES):
            chunk = s[:, c * LANES:(c + 1) * LANES]
            acc = chunk if acc is None else combine(acc, chunk)
    return lane_reduce(acc, axis=-1, keepdims=True)


def _attend_heads(score_fns, values, sinks):
    scores = [fn() for fn in score_fns]
    probs = []
    for blocks, sink in zip(scores, sinks):
        m = _row_reduce(blocks, jnp.maximum, jnp.max)
        if sink is not None:
            m = jnp.maximum(m, sink)
        ps = [jnp.exp(s - m) for s in blocks]
        l = _row_reduce(ps, jnp.add, jnp.sum)
        if sink is not None:
            l = l + jnp.exp(sink - m)
        probs.append(([p.astype(BF16) for p in ps], 1.0 / l))
    outs = []
    for (ps, inv_l), vals in zip(probs, values):
        out = None
        for p, (v, v_fm) in zip(ps, vals):
            o = _dot_nt(p, v) if v_fm else _dot(p, v)
            out = o if out is None else out + o
        outs.append(out * inv_l)
    return outs


def _merge_head_pairs(outs):
    return jnp.concatenate([outs[i] + outs[i + 1] for i in range(0, len(outs), 2)], axis=1)


def _mod(mod_ref, row, i):
    return mod_ref[pl.ds(row, 1), i * D_MODEL:(i + 1) * D_MODEL]


def _tail(x, ya, yb, ga, gb, mod_ref, row, woa_ref, wob_ref, wo_ref, gf_ref, wg_ref, wu_ref, wd_ref):
    gt1, sh2, sc2, gt2 = (_mod(mod_ref, row, i) for i in (2, 3, 4, 5))
    ma = _dot(ya.astype(BF16), woa_ref[...])
    mb = _dot(yb.astype(BF16), wob_ref[...])
    mg = _sigmoid(ga) * ma + _sigmoid(gb) * mb
    x1 = x + gt1 * _dot(mg.astype(BF16), wo_ref[...])
    h2 = _rms_mod(x1, gf_ref[...], sh2, sc2).astype(BF16)
    gate = _dot(h2, wg_ref[...])
    up = _dot(h2, wu_ref[...])
    act = (gate * _sigmoid(gate)) * up
    return x1 + gt2 * _dot(act.astype(BF16), wd_ref[...])


ADA_TK = 128
ADA_SLOTS = 4


def _ada_kernel(c_ref, b_ref, w_hbm, o_ref, stage_ref, sem):
    n_chunks = w_hbm.shape[0] // ADA_TK

    def copy(k):
        return pltpu.make_async_copy(w_hbm.at[pl.ds(k * ADA_TK, ADA_TK)], stage_ref.at[k % ADA_SLOTS],
                                     sem.at[k % ADA_SLOTS])

    for k in range(min(ADA_SLOTS, n_chunks)):
        copy(k).start()
    c = c_ref[...]
    s = c * _sigmoid(c)
    acc = jnp.broadcast_to(b_ref[...], o_ref.shape)
    for k in range(n_chunks):
        s_hi, s_lo = _split_bf16(s[:, k * ADA_TK:(k + 1) * ADA_TK])
        copy(k).wait()
        w_hi, w_lo = _split_bf16(stage_ref[k % ADA_SLOTS])
        if k + ADA_SLOTS < n_chunks:
            copy(k + ADA_SLOTS).start()
        r = _dot(jnp.concatenate([s_hi, s_lo], axis=0), w_hi)
        acc = acc + (r[:N_MOD_ROWS] + r[N_MOD_ROWS:] + _dot(s_hi, w_lo))
    o_ref[...] = acc


def _ada(cvec, w, b):
    k, n = w.shape
    return pl.pallas_call(
        _ada_kernel,
        in_specs=[pl.BlockSpec(memory_space=pltpu.VMEM), pl.BlockSpec(memory_space=pltpu.VMEM),
                  pl.BlockSpec(memory_space=pl.ANY)],
        out_specs=pl.BlockSpec(memory_space=pltpu.VMEM),
        out_shape=jax.ShapeDtypeStruct((N_MOD_ROWS, n), F32),
        scratch_shapes=[pltpu.VMEM((ADA_SLOTS, ADA_TK, n), F32), pltpu.SemaphoreType.DMA((ADA_SLOTS,))],
        name="ada_mod",
    )(cvec, b, w)


W_CHUNK_ROWS = {D_IN: 32, D_MODEL: 128, D_FF: 64}
DMA_DEPTH = 6
N_WEIGHTS = 7
CTX_GROUP = 1


def _load_cast_all(srcs, dsts, stages, sems, on_done):
    tasks, ring_pos = [], {}
    for w, (src, dst) in enumerate(zip(srcs, dsts)):
        n_rows, cols = src.shape
        rows = W_CHUNK_ROWS[cols]
        for r0 in range(0, n_rows, rows):
            pos = ring_pos.get(cols, 0)
            ring_pos[cols] = pos + 1
            tasks.append((w, r0, rows, cols, pos % DMA_DEPTH, r0 + rows == n_rows))

    def copy(task):
        w, r0, rows, cols, slot, _ = task
        return pltpu.make_async_copy(srcs[w].at[pl.ds(r0, rows)], stages[cols].at[slot], sems[cols].at[slot])

    for task in tasks[:DMA_DEPTH]:
        copy(task).start()
    for i, task in enumerate(tasks):
        w, r0, rows, cols, slot, last = task
        copy(task).wait()
        dsts[w][pl.ds(r0, rows), :] = stages[cols][slot].astype(BF16)
        if i + DMA_DEPTH < len(tasks):
            copy(tasks[i + DMA_DEPTH]).start()
        if last:
            on_done(w)


def _ctx_kernel(sink_ref, x_ref, mod_ref, gattn_ref, nrm_ref, gf_ref,
                win_hbm, woa_hbm, wob_hbm, wo_hbm, wg_hbm, wu_hbm, wd_hbm,
                y_ref, ka_ref, va_ref, kb_ref, vb_ref,
                win_out, woa_out, wob_out, wo_out, wg_out, wu_out, wd_out,
                win_ref, woa_ref, wob_ref, wo_ref, wg_ref, wu_ref, wd_ref, out_sem):
    srcs = (win_hbm, woa_hbm, wob_hbm, wo_hbm, wg_hbm, wu_hbm, wd_hbm)
    dsts = (win_ref, woa_ref, wob_ref, wo_ref, wg_ref, wu_ref, wd_ref)
    outs = (win_out, woa_out, wob_out, wo_out, wg_out, wu_out, wd_out)
    first = pl.program_id(0) == 0

    def out_copy(i):
        return pltpu.make_async_copy(dsts[i], outs[i], out_sem.at[i])

    @pl.when(first)
    def _():
        widths = tuple(W_CHUNK_ROWS)

        def scoped(*refs):
            stages = dict(zip(widths, refs[:len(widths)]))
            sems = dict(zip(widths, refs[len(widths):]))
            _load_cast_all(srcs, dsts, stages, sems, lambda w: out_copy(w).start())

        pl.run_scoped(scoped,
                      *[pltpu.VMEM((DMA_DEPTH, W_CHUNK_ROWS[c], c), F32) for c in widths],
                      *[pltpu.SemaphoreType.DMA((DMA_DEPTH,)) for _ in widths])

    x = x_ref[0]
    h = _rms_mod(x, gattn_ref[...], _mod(mod_ref, 0, 0), _mod(mod_ref, 0, 1)).astype(BF16)
    qa, ka, va, qb, kb, vb, ga, gb = _project(h, win_ref, nrm_ref)
    ka_ref[0] = ka.T
    va_ref[0] = va.T
    kb_ref[0] = kb.T
    vb_ref[0] = vb.T

    lo_mask = lax.broadcasted_iota(jnp.int32, (TOK, LANES), 1) < HEAD_DIM
    scale = HEAD_DIM ** -0.5
    qa = (qa * scale).astype(BF16)
    qb = (qb * scale).astype(BF16)

    k_pl = [t.astype(BF16) for t in _place_halves(ka, lo_mask)]
    v_pl = [t.astype(BF16) for t in _place_halves(va, lo_mask)]
    ya = []
    for g in range(0, A_HEADS, CTX_GROUP):
        score_fns, values, sinks = [], [], []
        for head in range(g, g + CTX_GROUP):
            c, half = divmod(head, 2)
            i = 2 * (c // 2) + half
            score_fns.append(lambda c=c, i=i: [_dot_nt(qa[:, c * LANES:(c + 1) * LANES], k_pl[i])])
            values.append([(v_pl[i], False)])
            sinks.append(sink_ref[head])
        ya += _attend_heads(score_fns, values, sinks)
    ya = _merge_head_pairs(ya)

    yb = []
    for g in range(0, B_HEADS, CTX_GROUP):
        score_fns, values = [], []
        for head in range(g, g + CTX_GROUP):
            c, half = divmod(head, 2)
            sl = slice(c * LANES, (c + 1) * LANES)
            hm = lo_mask if half == 0 else jnp.logical_not(lo_mask)
            km = jnp.where(hm, kb[:, sl], 0.0).astype(BF16)
            vm = jnp.where(hm, vb[:, sl], 0.0).astype(BF16)
            score_fns.append(lambda sl=sl, km=km: [_dot_nt(qb[:, sl], km)])
            values.append([(vm, False)])
        yb += _attend_heads(score_fns, values, [None] * CTX_GROUP)
    yb = _merge_head_pairs(yb)

    y_ref[0] = _tail(x, ya, yb, ga, gb, mod_ref, 0, woa_ref, wob_ref, wo_ref, gf_ref, wg_ref, wu_ref, wd_ref)

    @pl.when(first)
    def _():
        for i in range(N_WEIGHTS):
            out_copy(i).wait()


def _const_spec(shape):
    nd = len(shape)
    return pl.BlockSpec(shape, lambda *_: (0,) * nd, pipeline_mode=pl.Buffered(1))


def _ctx_layer(x, mods, sink, gattn, nrm, gf, weights):
    nb = x.shape[0]
    tok_spec = lambda w: pl.BlockSpec((1, SEQ, w), lambda b: (b, 0, 0))
    fm_spec = lambda w: pl.BlockSpec((1, w, SEQ), lambda b: (b, 0, 0))
    any_spec = pl.BlockSpec(memory_space=pl.ANY)
    in_specs = [
        pl.BlockSpec(memory_space=pltpu.SMEM),
        tok_spec(D_MODEL),
        _const_spec(mods.shape), _const_spec(gattn.shape), _const_spec(nrm.shape), _const_spec(gf.shape),
    ] + [any_spec] * len(weights)
    out_specs = ([tok_spec(D_MODEL), fm_spec(A_KV), fm_spec(A_KV), fm_spec(B_W), fm_spec(B_W)]
                 + [any_spec] * len(weights))
    out_shape = ([jax.ShapeDtypeStruct((nb, SEQ, D_MODEL), F32)]
                 + [jax.ShapeDtypeStruct((nb, w, SEQ), F32) for w in (A_KV, A_KV, B_W, B_W)]
                 + [jax.ShapeDtypeStruct(w.shape, BF16) for w in weights])
    scratch = [pltpu.VMEM(w.shape, BF16) for w in weights] + [pltpu.SemaphoreType.DMA((len(weights),))]
    return pl.pallas_call(
        _ctx_kernel,
        grid=(nb,),
        in_specs=in_specs,
        out_specs=out_specs,
        out_shape=out_shape,
        scratch_shapes=scratch,
        compiler_params=pltpu.CompilerParams(dimension_semantics=("arbitrary",), vmem_limit_bytes=VMEM_LIMIT),
        name="ctx_layer",
    )(sink, x, mods, gattn, nrm, gf, *weights)


def _rope(x, cos, sin, bit16):
    swapped = jnp.where(bit16, pltpu.roll(x, 16, 1), pltpu.roll(x, LANES - 16, 1))
    return x * cos + swapped * sin


def _lat_proj_kernel(x_ref, mod_ref, gattn_ref, win_ref, nrm_ref, cos_ref, sin_ref,
                     qa_ref, ka_ref, va_ref, qb_ref, kb_ref, vb_ref, ga_ref, gb_ref):
    row = 1 + pl.program_id(0)
    x = x_ref[0]
    h = _rms_mod(x, gattn_ref[...], _mod(mod_ref, row, 0), _mod(mod_ref, row, 1)).astype(BF16)
    qa, ka, va, qb, kb, vb, ga, gb = _project(h, win_ref, nrm_ref)
    lane = lax.broadcasted_iota(jnp.int32, (TOK, LANES), 1)
    lo_mask = lane < HEAD_DIM
    bit16 = (lane & 16) != 0
    cos, sin = cos_ref[...], sin_ref[...]
    scale = HEAD_DIM ** -0.5
    for c in range(A_Q // LANES):
        sl = slice(c * LANES, (c + 1) * LANES)
        qa_ref[0, :, sl] = (_rope(qa[:, sl], cos, sin, bit16) * scale).astype(BF16)
    ka = _rope(ka, cos, sin, bit16)
    for i, t in enumerate(_place_halves(ka, lo_mask)):
        ka_ref[0, :, i * LANES:(i + 1) * LANES] = t.astype(BF16)
    for i, t in enumerate(_place_halves(va, lo_mask)):
        va_ref[0, :, i * LANES:(i + 1) * LANES] = t.astype(BF16)
    qb_ref[0] = (qb * scale).astype(BF16)
    kb_ref[0] = kb.astype(BF16)
    vb_ref[0] = vb.astype(BF16)
    ga_ref[0] = ga
    gb_ref[0] = gb


def _lat_proj(x, mods, gattn, win, nrm, cos, sin):
    nb, n, _ = x.shape
    nt = n // TOK
    tok_spec = lambda w: pl.BlockSpec((1, TOK, w), lambda b, t: (b, t, 0))
    in_specs = [
        tok_spec(D_MODEL),
        _const_spec(mods.shape), _const_spec(gattn.shape), _const_spec(win.shape), _const_spec(nrm.shape),
        pl.BlockSpec((TOK, LANES), lambda b, t: (t, 0)),
        pl.BlockSpec((TOK, LANES), lambda b, t: (t, 0)),
    ]
    widths = (A_Q, 4 * LANES, 4 * LANES, B_W, B_W, B_W, D_MODEL, D_MODEL)
    dtypes = (BF16,) * 6 + (F32, F32)
    return pl.pallas_call(
        _lat_proj_kernel,
        grid=(nb, nt),
        in_specs=in_specs,
        out_specs=[tok_spec(w) for w in widths],
        out_shape=[jax.ShapeDtypeStruct((nb, n, w), dt) for w, dt in zip(widths, dtypes)],
        compiler_params=pltpu.CompilerParams(dimension_semantics=("arbitrary", "arbitrary"),
                                             vmem_limit_bytes=VMEM_LIMIT),
        name="lat_proj",
    )(x, mods, gattn, win, nrm, cos, sin)


def _build_na_bias(row_ref, bt_ref):
    w = lax.broadcasted_iota(jnp.int32, (GRID_W, LANES), 0)
    kcol = lax.broadcasted_iota(jnp.int32, (GRID_W, LANES), 1) & (GRID_W - 1)
    cs = jnp.clip(w - NA_COLS // 2, 0, GRID_W - NA_COLS)
    valid = (kcol >= cs) & (kcol < cs + NA_COLS)
    for h in range(B_HEADS):
        for d in range(2 * NA_ROWS - 1):
            rows = jnp.broadcast_to(row_ref[h, d:d + 1, :], (GRID_W, LANES))
            t = pltpu.roll(rows, LANES - (NA_COLS - 1), 1, stride=1, stride_axis=0)
            bt_ref[h, d] = jnp.where(valid, t, NEG)


A_KEYS = 2 * TOK
B_KROWS = 12
B_KEYS = B_KROWS * GRID_W
ATTN_GROUP = 4
N_TAIL_WEIGHTS = 6


def _lat_attn_kernel(sink_ref, x_ref, mod_ref, qa_ref, ka_ref, va_ref, qb_ref, kb_ref, vb_ref,
                     ga_ref, gb_ref, cka_ref, cva_ref, ckb_ref, cvb_ref, rpb_ref, gf_ref,
                     woa_hbm, wob_hbm, wo_hbm, wg_hbm, wu_hbm, wd_hbm, y_ref,
                     bt_ref, woa_ref, wob_ref, wo_ref, wg_ref, wu_ref, wd_ref, w_sem):
    t = pl.program_id(1)
    row = 1 + pl.program_id(0)
    first = (pl.program_id(0) == 0) & (t == 0)
    x = x_ref[0]
    w_hbm = (woa_hbm, wob_hbm, wo_hbm, wg_hbm, wu_hbm, wd_hbm)
    w_vmem = (woa_ref, wob_ref, wo_ref, wg_ref, wu_ref, wd_ref)

    def w_copy(k):
        return pltpu.make_async_copy(w_hbm[k], w_vmem[k], w_sem.at[k])

    @pl.when(first)
    def _():
        for k in range(N_TAIL_WEIGHTS):
            w_copy(k).start()
        _build_na_bias(rpb_ref, bt_ref)

    ks = pl.multiple_of(jnp.clip(TOK * t - A_WINDOW, 0, DEC_SEQ - A_KEYS), LANES)
    qpos = TOK * t + lax.broadcasted_iota(jnp.int32, (TOK, A_KEYS), 0)
    kpos = ks + lax.broadcasted_iota(jnp.int32, (TOK, A_KEYS), 1)
    a_valid = jnp.abs(qpos - kpos) <= A_WINDOW
    ck_t, cv_t = cka_ref[0], cva_ref[0]
    ck_pl = [_place_rows(ck_t, i // 2, i % 2).astype(BF16) for i in range(4)]
    cv_pl = [_place_rows(cv_t, i // 2, i % 2).astype(BF16) for i in range(4)]
    ya = []
    for g in range(0, A_HEADS, ATTN_GROUP):
        score_fns, values, sinks = [], [], []
        for head in range(g, g + ATTN_GROUP):
            c, half = divmod(head, 2)
            i = 2 * (c // 2) + half

            def a_scores(c=c, i=i):
                qc = qa_ref[0, :, c * LANES:(c + 1) * LANES]
                k_loc = ka_ref[0, pl.ds(ks, A_KEYS), i * LANES:(i + 1) * LANES]
                return [jnp.where(a_valid, _dot_nt(qc, k_loc), NEG), _dot(qc, ck_pl[i])]

            score_fns.append(a_scores)
            values.append([(va_ref[0, pl.ds(ks, A_KEYS), i * LANES:(i + 1) * LANES], False), (cv_pl[i], True)])
            sinks.append(sink_ref[head])
        ya.append(_merge_head_pairs(_attend_heads(score_fns, values, sinks)))
    ya = jnp.concatenate(ya, axis=1)

    kr0 = jnp.clip(4 * t - 4, 0, DEC_SEQ // GRID_W - B_KROWS)
    kst = pl.multiple_of(kr0 * GRID_W, TOK)
    lane_lo = lax.broadcasted_iota(jnp.int32, (1, LANES), 1) < HEAD_DIM
    b_dr, b_valid = [], []
    for qi in range(TOK // GRID_W):
        r = 4 * t + qi
        rs = jnp.clip(r - NA_ROWS // 2, 0, DEC_SEQ // GRID_W - NA_ROWS)
        drs, valids = [], []
        for p in range(B_KROWS // 2):
            kr = kr0 + 2 * p
            drs.append(jnp.clip(kr - r + (NA_ROWS - 1), 0, 2 * NA_ROWS - 2))
            ok = [((k >= rs) & (k < rs + NA_ROWS)).astype(jnp.int32) for k in (kr, kr + 1)]
            valids.append(jnp.where(lane_lo, ok[0], ok[1]) != 0)
        b_dr.append(drs)
        b_valid.append(valids)
    lo_k = lax.broadcasted_iota(jnp.int32, (B_KEYS, LANES), 1) < HEAD_DIM
    lo_f = lax.broadcasted_iota(jnp.int32, (LANES, PAST_LEN), 0) < HEAD_DIM
    zero = jnp.zeros((), BF16)
    yb = []
    for g in range(0, B_HEADS, ATTN_GROUP):
        score_fns, values = [], []
        for head in range(g, g + ATTN_GROUP):
            c, half = divmod(head, 2)
            sl = slice(c * LANES, (c + 1) * LANES)
            hm_k = lo_k if half == 0 else jnp.logical_not(lo_k)
            hm_f = lo_f if half == 0 else jnp.logical_not(lo_f)

            def b_scores(head=head, sl=sl, hm_k=hm_k, hm_f=hm_f):
                qc = qb_ref[0, :, sl]
                kk = jnp.where(hm_k, kb_ref[0, pl.ds(kst, B_KEYS), sl], zero)
                kc = jnp.where(hm_f, ckb_ref[0, sl, :], 0.0).astype(BF16)
                s = _dot_nt(qc, kk)
                rows = []
                for qi in range(TOK // GRID_W):
                    blocks = []
                    for p in range(B_KROWS // 2):
                        blk = s[qi * GRID_W:(qi + 1) * GRID_W, p * LANES:(p + 1) * LANES] + bt_ref[head, b_dr[qi][p]]
                        blocks.append(jnp.where(b_valid[qi][p], blk, NEG))
                    rows.append(jnp.concatenate(blocks, axis=1))
                return [jnp.concatenate(rows, axis=0), _dot(qc, kc)]

            score_fns.append(b_scores)
            vv = jnp.where(hm_k, vb_ref[0, pl.ds(kst, B_KEYS), sl], zero)
            vc = jnp.where(hm_f, cvb_ref[0, sl, :], 0.0).astype(BF16)
            values.append([(vv, False), (vc, True)])
        yb.append(_merge_head_pairs(_attend_heads(score_fns, values, [None] * ATTN_GROUP)))
    yb = jnp.concatenate(yb, axis=1)

    @pl.when(first)
    def _():
        for k in range(N_TAIL_WEIGHTS):
            w_copy(k).wait()

    y_ref[0] = _tail(x, ya, yb, ga_ref[0], gb_ref[0], mod_ref, row, woa_ref, wob_ref, wo_ref, gf_ref,
                     wg_ref, wu_ref, wd_ref)


def _lat_attn(sink, x, mods, qa, ka4, va4, qb, kb, vb, ga, gb, cka, cva, ckb, cvb, rpb_rows, gf, weights):
    nb, n, _ = x.shape
    nt = n // TOK
    tok_spec = lambda w: pl.BlockSpec((1, TOK, w), lambda b, t: (b, t, 0))
    seq_spec = lambda rows, w: pl.BlockSpec((1, rows, w), lambda b, t: (b, 0, 0),
                                            pipeline_mode=pl.Buffered(1))
    in_specs = [
        pl.BlockSpec(memory_space=pltpu.SMEM),
        tok_spec(D_MODEL),
        _const_spec(mods.shape),
        tok_spec(A_Q), seq_spec(n, 4 * LANES), seq_spec(n, 4 * LANES),
        tok_spec(B_W), seq_spec(n, B_W), seq_spec(n, B_W),
        tok_spec(D_MODEL), tok_spec(D_MODEL),
        seq_spec(A_KV, PAST_LEN), seq_spec(A_KV, PAST_LEN), seq_spec(B_W, PAST_LEN), seq_spec(B_W, PAST_LEN),
        _const_spec(rpb_rows.shape), _const_spec(gf.shape),
    ] + [pl.BlockSpec(memory_space=pl.ANY)] * len(weights)
    scratch = ([pltpu.VMEM((B_HEADS, 2 * NA_ROWS - 1, GRID_W, LANES), F32)]
               + [pltpu.VMEM(w.shape, BF16) for w in weights] + [pltpu.SemaphoreType.DMA((len(weights),))])
    return pl.pallas_call(
        _lat_attn_kernel,
        grid=(nb, nt),
        in_specs=in_specs,
        out_specs=tok_spec(D_MODEL),
        out_shape=jax.ShapeDtypeStruct((nb, n, D_MODEL), F32),
        scratch_shapes=scratch,
        compiler_params=pltpu.CompilerParams(dimension_semantics=("arbitrary", "arbitrary"),
                                             vmem_limit_bytes=VMEM_LIMIT),
        name="lat_attn",
    )(sink, x, mods, qa, ka4, va4, qb, kb, vb, ga, gb, cka, cva, ckb, cvb, rpb_rows, gf, *weights)


def _rope_tables(n):
    half = HEAD_DIM // 4
    inv_freq = 1.0 / (ROPE_BASE ** (np.arange(half, dtype=np.float64) / half))
    t = np.arange(n)
    parts_c, parts_s = [], []
    for pos in (t // GRID_W, t % GRID_W):
        ang = pos.astype(np.float64)[:, None] * inv_freq[None, :]
        c, s = np.cos(ang), np.sin(ang)
        parts_c += [c, c]
        parts_s += [-s, s]
    cos = np.concatenate(parts_c * 2, axis=1).astype(np.float32)
    sin = np.concatenate(parts_s * 2, axis=1).astype(np.float32)
    return jnp.asarray(cos), jnp.asarray(sin)


def _feature_major(cache, l):
    b, _, s, h, d = cache.shape
    return jnp.transpose(cache[:, l], (0, 2, 3, 1)).reshape(b, h * d, s)


def _token_major(x, heads):
    b, _, s = x.shape
    return jnp.transpose(x.reshape(b, 1, heads, HEAD_DIM, s), (0, 1, 4, 2, 3))


def kernel(x_prompt, x_sample, cache_a_k, cache_a_v, cache_b_k, cache_b_v, c, c_ctx, w_ada, b_ada, g_attn, w_in, q_norm_a, k_norm_a, q_norm_b, k_norm_b, sink_a, rpb_b, w_out_a, w_out_b, w_o, g_ffn, w_gate, w_up, w_down):
    nd, n, _ = x_sample.shape
    l = 0

    cvec = jnp.concatenate([c_ctx[None, :], c, jnp.zeros((N_MOD_ROWS - 1 - nd, D_MODEL), F32)], axis=0)
    mods = _ada(cvec, w_ada[l], b_ada[l][None, :])

    gattn, gf = g_attn[l][None, :], g_ffn[l][None, :]
    nrm = jnp.stack([jnp.tile(v[l], A_Q // HEAD_DIM) for v in (q_norm_a, k_norm_a, q_norm_b, k_norm_b)])
    sink = sink_a[l]
    weights = (w_in[l], w_out_a[l], w_out_b[l], w_o[l], w_gate[l], w_up[l], w_down[l])

    (y_prompt, ka, va, kb, vb, win, woa, wob, wo, wg, wu, wd) = _ctx_layer(
        x_prompt, mods, sink, gattn, nrm, gf, weights)
    new_a_k = _token_major(ka, A_KV_HEADS)
    new_a_v = _token_major(va, A_KV_HEADS)
    new_b_k = _token_major(kb, B_HEADS)
    new_b_v = _token_major(vb, B_HEADS)

    cos, sin = _rope_tables(n)
    qa, ka4, va4, qb, lkb, lvb, ga, gb = _lat_proj(x_sample, mods, gattn, win, nrm, cos, sin)

    rp = jnp.pad(rpb_b[l], ((0, 0), (0, 1), (0, HEAD_DIM - (2 * NA_COLS - 1))))
    rpb_rows = jnp.concatenate([rp[:, :-1], rp[:, 1:]], axis=-1)

    y_sample = _lat_attn(sink, x_sample, mods, qa, ka4, va4, qb, lkb, lvb, ga, gb,
                         _feature_major(cache_a_k, l), _feature_major(cache_a_v, l),
                         _feature_major(cache_b_k, l), _feature_major(cache_b_v, l),
                         rpb_rows, gf, (woa, wob, wo, wg, wu, wd))
    return (y_prompt, y_sample, new_a_k, new_a_v, new_b_k, new_b_v)
```

```python
import numpy as np

import jax
import jax.numpy as jnp
from jax import lax
from jax.experimental import pallas as pl
from jax.experimental.pallas import tpu as pltpu

D_MODEL = 1024
SEQ = 256
DEC_SEQ = 1024
PAST_LEN = 256
GRID_W = 64
HEAD_DIM = 64
A_HEADS = 8
A_KV_HEADS = 2
A_WINDOW = 128
B_HEADS = 8
NA_ROWS = 8
NA_COLS = 16
D_FF = 2816
ROPE_BASE = 10000.0
RMS_EPS = 1e-6
NEG = -1e30
LOG2E = float(np.log2(np.e))

A_Q = A_HEADS * HEAD_DIM
A_KV = A_KV_HEADS * HEAD_DIM
B_W = B_HEADS * HEAD_DIM
O_QA, O_KA, O_VA = 0, A_Q, A_Q + A_KV
O_QB = A_Q + 2 * A_KV
O_KB, O_VB = O_QB + B_W, O_QB + 2 * B_W
O_GA = O_QB + 3 * B_W
O_GB = O_GA + D_MODEL
D_IN = O_GB + D_MODEL

LANES = 128
TOK = 256
VMEM_LIMIT = 58 * 1024 * 1024
N_MOD_ROWS = 8

F32 = jnp.float32
BF16 = jnp.bfloat16


def _dot(a, b):
    return jnp.dot(a, b, preferred_element_type=F32)


def _dot_nt(a, b):
    return lax.dot_general(a, b, (((1,), (1,)), ((), ())), preferred_element_type=F32)


def _split_bf16(x):
    hi = x.astype(BF16)
    lo = (x - hi.astype(F32)).astype(BF16)
    return hi, lo


def _sigmoid(x):
    return 1.0 / (1.0 + jnp.exp(-x))


def _rms_mod(x, g, shift, scale):
    ms = jnp.mean(x * x, axis=-1, keepdims=True)
    return (x * lax.rsqrt(ms + RMS_EPS) * g) * (1.0 + scale) + shift


def _head_norm(u, w):
    lo_mask = lax.broadcasted_iota(jnp.int32, (u.shape[0], LANES), 1) < HEAD_DIM
    parts = []
    for c in range(u.shape[1] // LANES):
        uc = u[:, c * LANES:(c + 1) * LANES]
        uu = uc * uc
        ss_lo = jnp.sum(jnp.where(lo_mask, uu, 0.0), axis=-1, keepdims=True)
        ss_hi = jnp.sum(jnp.where(lo_mask, 0.0, uu), axis=-1, keepdims=True)
        ss = jnp.where(lo_mask, ss_lo, ss_hi)
        parts.append(uc * lax.rsqrt(ss * (1.0 / HEAD_DIM) + RMS_EPS))
    un = parts[0] if len(parts) == 1 else jnp.concatenate(parts, axis=1)
    return un * w


def _project(h, win_ref, nrm_ref):
    qa = _head_norm(_dot(h, win_ref[:, O_QA:O_KA]), nrm_ref[0:1, :])
    ka = _head_norm(_dot(h, win_ref[:, O_KA:O_VA]), nrm_ref[1:2, :A_KV])
    va = _dot(h, win_ref[:, O_VA:O_QB])
    qb = _head_norm(_dot(h, win_ref[:, O_QB:O_KB]), nrm_ref[2:3, :])
    kb = _head_norm(_dot(h, win_ref[:, O_KB:O_VB]), nrm_ref[3:4, :])
    vb = _dot(h, win_ref[:, O_VB:O_GA])
    ga = _dot(h, win_ref[:, O_GA:O_GB])
    gb = _dot(h, win_ref[:, O_GB:D_IN])
    return qa, ka, va, qb, kb, vb, ga, gb


def _place_halves(x, lo_mask):
    h0_lo = jnp.where(lo_mask, x, 0.0)
    h1_hi = jnp.where(lo_mask, 0.0, x)
    return h0_lo, pltpu.roll(h0_lo, HEAD_DIM, 1), pltpu.roll(h1_hi, HEAD_DIM, 1), h1_hi


def _place_rows(xt, head, half):
    rows = xt[head * HEAD_DIM:(head + 1) * HEAD_DIM]
    zero = jnp.zeros_like(rows)
    return jnp.concatenate([rows, zero] if half == 0 else [zero, rows], axis=0)


def _row_reduce(blocks, combine, lane_reduce):
    acc = None
    for s in blocks:
        for c in range(s.shape[1] // LANES):
            chunk = s[:, c * LANES:(c + 1) * LANES]
            acc = chunk if acc is None else combine(acc, chunk)
    return lane_reduce(acc, axis=-1, keepdims=True)


def _attend_heads(score_fns, values, sinks):
    scores = [fn() for fn in score_fns]
    probs = []
    for blocks, sink in zip(scores, sinks):
        m = _row_reduce(blocks, jnp.maximum, jnp.max)
        if sink is not None:
            m = jnp.maximum(m, sink)
        ps = [jnp.exp2(s - m) for s in blocks]
        l = _row_reduce(ps, jnp.add, jnp.sum)
        if sink is not None:
            l = l + jnp.exp2(sink - m)
        probs.append(([p.astype(BF16) for p in ps], 1.0 / l))
    outs = []
    for (ps, inv_l), vals in zip(probs, values):
        out = None
        for p, (v, v_fm) in zip(ps, vals):
            o = _dot_nt(p, v) if v_fm else _dot(p, v)
            out = o if out is None else out + o
        outs.append(out * inv_l)
    return outs


def _merge_head_pairs(outs):
    return jnp.concatenate([outs[i] + outs[i + 1] for i in range(0, len(outs), 2)], axis=1)


def _mod(mod_ref, row, i):
    return mod_ref[pl.ds(row, 1), i * D_MODEL:(i + 1) * D_MODEL]


def _tail(x, ya, yb, ga, gb, mod_ref, row, woa_ref, wob_ref, wo_ref, gf_ref, wg_ref, wu_ref, wd_ref):
    gt1, sh2, sc2, gt2 = (_mod(mod_ref, row, i) for i in (2, 3, 4, 5))
    ma = _dot(ya.astype(BF16), woa_ref[...])
    mb = _dot(yb.astype(BF16), wob_ref[...])
    mg = _sigmoid(ga) * ma + _sigmoid(gb) * mb
    x1 = x + gt1 * _dot(mg.astype(BF16), wo_ref[...])
    h2 = _rms_mod(x1, gf_ref[...], sh2, sc2).astype(BF16)
    gate = _dot(h2, wg_ref[...])
    up = _dot(h2, wu_ref[...])
    act = (gate * _sigmoid(gate)) * up
    return x1 + gt2 * _dot(act.astype(BF16), wd_ref[...])


ADA_TK = 128
ADA_SLOTS = 4


def _ada_kernel(c_ref, b_ref, w_hbm, o_ref, stage_ref, sem):
    n_chunks = w_hbm.shape[0] // ADA_TK

    def copy(k):
        return pltpu.make_async_copy(w_hbm.at[pl.ds(k * ADA_TK, ADA_TK)], stage_ref.at[k % ADA_SLOTS],
                                     sem.at[k % ADA_SLOTS])

    for k in range(min(ADA_SLOTS, n_chunks)):
        copy(k).start()
    c = c_ref[...]
    s = c * _sigmoid(c)
    acc = jnp.broadcast_to(b_ref[...], o_ref.shape)
    for k in range(n_chunks):
        s_hi, s_lo = _split_bf16(s[:, k * ADA_TK:(k + 1) * ADA_TK])
        copy(k).wait()
        w_hi, w_lo = _split_bf16(stage_ref[k % ADA_SLOTS])
        if k + ADA_SLOTS < n_chunks:
            copy(k + ADA_SLOTS).start()
        r = _dot(jnp.concatenate([s_hi, s_lo], axis=0), w_hi)
        acc = acc + (r[:N_MOD_ROWS] + r[N_MOD_ROWS:] + _dot(s_hi, w_lo))
    o_ref[...] = acc


def _ada(cvec, w, b):
    k, n = w.shape
    return pl.pallas_call(
        _ada_kernel,
        in_specs=[pl.BlockSpec(memory_space=pltpu.VMEM), pl.BlockSpec(memory_space=pltpu.VMEM),
                  pl.BlockSpec(memory_space=pl.ANY)],
        out_specs=pl.BlockSpec(memory_space=pltpu.VMEM),
        out_shape=jax.ShapeDtypeStruct((N_MOD_ROWS, n), F32),
        scratch_shapes=[pltpu.VMEM((ADA_SLOTS, ADA_TK, n), F32), pltpu.SemaphoreType.DMA((ADA_SLOTS,))],
        name="ada_mod",
    )(cvec, b, w)


W_CHUNK_ROWS = {D_IN: 32, D_MODEL: 128, D_FF: 64}
DMA_DEPTH = 6
N_WEIGHTS = 7
CTX_GROUP = 1


def _load_cast_all(srcs, dsts, stages, sems, on_done):
    tasks, ring_pos = [], {}
    for w, (src, dst) in enumerate(zip(srcs, dsts)):
        n_rows, cols = src.shape
        rows = W_CHUNK_ROWS[cols]
        for r0 in range(0, n_rows, rows):
            pos = ring_pos.get(cols, 0)
            ring_pos[cols] = pos + 1
            tasks.append((w, r0, rows, cols, pos % DMA_DEPTH, r0 + rows == n_rows))

    def copy(task):
        w, r0, rows, cols, slot, _ = task
        return pltpu.make_async_copy(srcs[w].at[pl.ds(r0, rows)], stages[cols].at[slot], sems[cols].at[slot])

    for task in tasks[:DMA_DEPTH]:
        copy(task).start()
    for i, task in enumerate(tasks):
        w, r0, rows, cols, slot, last = task
        copy(task).wait()
        dsts[w][pl.ds(r0, rows), :] = stages[cols][slot].astype(BF16)
        if i + DMA_DEPTH < len(tasks):
            copy(tasks[i + DMA_DEPTH]).start()
        if last:
            on_done(w)


def _ctx_kernel(sink_ref, x_ref, mod_ref, gattn_ref, nrm_ref, gf_ref,
                win_hbm, woa_hbm, wob_hbm, wo_hbm, wg_hbm, wu_hbm, wd_hbm,
                y_ref, ka_ref, va_ref, kb_ref, vb_ref,
                win_out, woa_out, wob_out, wo_out, wg_out, wu_out, wd_out,
                win_ref, woa_ref, wob_ref, wo_ref, wg_ref, wu_ref, wd_ref, out_sem):
    srcs = (win_hbm, woa_hbm, wob_hbm, wo_hbm, wg_hbm, wu_hbm, wd_hbm)
    dsts = (win_ref, woa_ref, wob_ref, wo_ref, wg_ref, wu_ref, wd_ref)
    outs = (win_out, woa_out, wob_out, wo_out, wg_out, wu_out, wd_out)
    first = pl.program_id(0) == 0

    def out_copy(i):
        return pltpu.make_async_copy(dsts[i], outs[i], out_sem.at[i])

    @pl.when(first)
    def _():
        widths = tuple(W_CHUNK_ROWS)

        def scoped(*refs):
            stages = dict(zip(widths, refs[:len(widths)]))
            sems = dict(zip(widths, refs[len(widths):]))
            _load_cast_all(srcs, dsts, stages, sems, lambda w: out_copy(w).start())

        pl.run_scoped(scoped,
                      *[pltpu.VMEM((DMA_DEPTH, W_CHUNK_ROWS[c], c), F32) for c in widths],
                      *[pltpu.SemaphoreType.DMA((DMA_DEPTH,)) for _ in widths])

    x = x_ref[0]
    h = _rms_mod(x, gattn_ref[...], _mod(mod_ref, 0, 0), _mod(mod_ref, 0, 1)).astype(BF16)
    qa, ka, va, qb, kb, vb, ga, gb = _project(h, win_ref, nrm_ref)
    ka_ref[0] = ka.T
    va_ref[0] = va.T
    kb_ref[0] = kb.T
    vb_ref[0] = vb.T

    lo_mask = lax.broadcasted_iota(jnp.int32, (TOK, LANES), 1) < HEAD_DIM
    scale = HEAD_DIM ** -0.5 * LOG2E
    qa = (qa * scale).astype(BF16)
    qb = (qb * scale).astype(BF16)

    k_pl = [t.astype(BF16) for t in _place_halves(ka, lo_mask)]
    v_pl = [t.astype(BF16) for t in _place_halves(va, lo_mask)]
    ya = []
    for g in range(0, A_HEADS, CTX_GROUP):
        score_fns, values, sinks = [], [], []
        for head in range(g, g + CTX_GROUP):
            c, half = divmod(head, 2)
            i = 2 * (c // 2) + half
            score_fns.append(lambda c=c, i=i: [_dot_nt(qa[:, c * LANES:(c + 1) * LANES], k_pl[i])])
            values.append([(v_pl[i], False)])
            sinks.append(sink_ref[head] * LOG2E)
        ya += _attend_heads(score_fns, values, sinks)
    ya = _merge_head_pairs(ya)

    yb = []
    for g in range(0, B_HEADS, CTX_GROUP):
        score_fns, values = [], []
        for head in range(g, g + CTX_GROUP):
            c, half = divmod(head, 2)
            sl = slice(c * LANES, (c + 1) * LANES)
            hm = lo_mask if half == 0 else jnp.logical_not(lo_mask)
            km = jnp.where(hm, kb[:, sl], 0.0).astype(BF16)
            vm = jnp.where(hm, vb[:, sl], 0.0).astype(BF16)
            score_fns.append(lambda sl=sl, km=km: [_dot_nt(qb[:, sl], km)])
            values.append([(vm, False)])
        yb += _attend_heads(score_fns, values, [None] * CTX_GROUP)
    yb = _merge_head_pairs(yb)

    y_ref[0] = _tail(x, ya, yb, ga, gb, mod_ref, 0, woa_ref, wob_ref, wo_ref, gf_ref, wg_ref, wu_ref, wd_ref)

    @pl.when(first)
    def _():
        for i in range(N_WEIGHTS):
            out_copy(i).wait()


def _const_spec(shape):
    nd = len(shape)
    return pl.BlockSpec(shape, lambda *_: (0,) * nd, pipeline_mode=pl.Buffered(1))


def _ctx_layer(x, mods, sink, gattn, nrm, gf, weights):
    nb = x.shape[0]
    tok_spec = lambda w: pl.BlockSpec((1, SEQ, w), lambda b: (b, 0, 0))
    fm_spec = lambda w: pl.BlockSpec((1, w, SEQ), lambda b: (b, 0, 0))
    any_spec = pl.BlockSpec(memory_space=pl.ANY)
    in_specs = [
        pl.BlockSpec(memory_space=pltpu.SMEM),
        tok_spec(D_MODEL),
        _const_spec(mods.shape), _const_spec(gattn.shape), _const_spec(nrm.shape), _const_spec(gf.shape),
    ] + [any_spec] * len(weights)
    out_specs = ([tok_spec(D_MODEL), fm_spec(A_KV), fm_spec(A_KV), fm_spec(B_W), fm_spec(B_W)]
                 + [any_spec] * len(weights))
    out_shape = ([jax.ShapeDtypeStruct((nb, SEQ, D_MODEL), F32)]
                 + [jax.ShapeDtypeStruct((nb, w, SEQ), F32) for w in (A_KV, A_KV, B_W, B_W)]
                 + [jax.ShapeDtypeStruct(w.shape, BF16) for w in weights])
    scratch = [pltpu.VMEM(w.shape, BF16) for w in weights] + [pltpu.SemaphoreType.DMA((len(weights),))]
    return pl.pallas_call(
        _ctx_kernel,
        grid=(nb,),
        in_specs=in_specs,
        out_specs=out_specs,
        out_shape=out_shape,
        scratch_shapes=scratch,
        compiler_params=pltpu.CompilerParams(dimension_semantics=("arbitrary",), vmem_limit_bytes=VMEM_LIMIT),
        name="ctx_layer",
    )(sink, x, mods, gattn, nrm, gf, *weights)


ROPE_HALF = HEAD_DIM // 4


def _rope(x, cos, sin, upper_half):
    swapped = jnp.where(upper_half, pltpu.roll(x, ROPE_HALF, 1), pltpu.roll(x, LANES - ROPE_HALF, 1))
    return x * cos + swapped * sin


def _lat_proj_kernel(x_ref, mod_ref, gattn_ref, win_ref, nrm_ref, cos_ref, sin_ref,
                     qa_ref, ka_ref, va_ref, qb_ref, kb_ref, vb_ref, ga_ref, gb_ref):
    row = 1 + pl.program_id(0)
    x = x_ref[0]
    h = _rms_mod(x, gattn_ref[...], _mod(mod_ref, row, 0), _mod(mod_ref, row, 1)).astype(BF16)
    qa, ka, va, qb, kb, vb, ga, gb = _project(h, win_ref, nrm_ref)
    lane = lax.broadcasted_iota(jnp.int32, (TOK, LANES), 1)
    lo_mask = lane < HEAD_DIM
    upper_half = (lane & ROPE_HALF) != 0
    cos, sin = cos_ref[...], sin_ref[...]
    scale = HEAD_DIM ** -0.5 * LOG2E
    for c in range(A_Q // LANES):
        sl = slice(c * LANES, (c + 1) * LANES)
        qa_ref[0, :, sl] = (_rope(qa[:, sl], cos, sin, upper_half) * scale).astype(BF16)
    ka = _rope(ka, cos, sin, upper_half)
    for i, t in enumerate(_place_halves(ka, lo_mask)):
        ka_ref[0, :, i * LANES:(i + 1) * LANES] = t.astype(BF16)
    for i, t in enumerate(_place_halves(va, lo_mask)):
        va_ref[0, :, i * LANES:(i + 1) * LANES] = t.astype(BF16)
    qb_ref[0] = (qb * scale).astype(BF16)
    kb_ref[0] = kb.astype(BF16)
    vb_ref[0] = vb.astype(BF16)
    ga_ref[0] = ga
    gb_ref[0] = gb


def _lat_proj(x, mods, gattn, win, nrm, cos, sin):
    nb, n, _ = x.shape
    nt = n // TOK
    tok_spec = lambda w: pl.BlockSpec((1, TOK, w), lambda b, t: (b, t, 0))
    in_specs = [
        tok_spec(D_MODEL),
        _const_spec(mods.shape), _const_spec(gattn.shape), _const_spec(win.shape), _const_spec(nrm.shape),
        pl.BlockSpec((TOK, LANES), lambda b, t: (t, 0)),
        pl.BlockSpec((TOK, LANES), lambda b, t: (t, 0)),
    ]
    widths = (A_Q, 4 * LANES, 4 * LANES, B_W, B_W, B_W, D_MODEL, D_MODEL)
    dtypes = (BF16,) * 6 + (F32, F32)
    return pl.pallas_call(
        _lat_proj_kernel,
        grid=(nb, nt),
        in_specs=in_specs,
        out_specs=[tok_spec(w) for w in widths],
        out_shape=[jax.ShapeDtypeStruct((nb, n, w), dt) for w, dt in zip(widths, dtypes)],
        compiler_params=pltpu.CompilerParams(dimension_semantics=("arbitrary", "arbitrary"),
                                             vmem_limit_bytes=VMEM_LIMIT),
        name="lat_proj",
    )(x, mods, gattn, win, nrm, cos, sin)


def _build_na_bias(row_ref, bt_ref):
    w = lax.broadcasted_iota(jnp.int32, (GRID_W, LANES), 0)
    kcol = lax.broadcasted_iota(jnp.int32, (GRID_W, LANES), 1) & (GRID_W - 1)
    cs = jnp.clip(w - NA_COLS // 2, 0, GRID_W - NA_COLS)
    valid = (kcol >= cs) & (kcol < cs + NA_COLS)
    for h in range(B_HEADS):
        for d in range(2 * NA_ROWS - 1):
            rows = jnp.broadcast_to(row_ref[h, d:d + 1, :] * LOG2E, (GRID_W, LANES))
            t = pltpu.roll(rows, LANES - (NA_COLS - 1), 1, stride=1, stride_axis=0)
            bt_ref[h, d] = jnp.where(valid, t, NEG)


A_KEYS = 2 * TOK
B_KROWS = 12
B_KEYS = B_KROWS * GRID_W
ATTN_GROUP = 4
N_TAIL_WEIGHTS = 6


def _lat_attn_kernel(sink_ref, x_ref, mod_ref, qa_ref, ka_ref, va_ref, qb_ref, kb_ref, vb_ref,
                     ga_ref, gb_ref, cka_ref, cva_ref, ckb_ref, cvb_ref, rpb_ref, gf_ref,
                     woa_hbm, wob_hbm, wo_hbm, wg_hbm, wu_hbm, wd_hbm, y_ref,
                     bt_ref, woa_ref, wob_ref, wo_ref, wg_ref, wu_ref, wd_ref, w_sem):
    t = pl.program_id(1)
    row = 1 + pl.program_id(0)
    first = (pl.program_id(0) == 0) & (t == 0)
    x = x_ref[0]
    w_hbm = (woa_hbm, wob_hbm, wo_hbm, wg_hbm, wu_hbm, wd_hbm)
    w_vmem = (woa_ref, wob_ref, wo_ref, wg_ref, wu_ref, wd_ref)

    def w_copy(k):
        return pltpu.make_async_copy(w_hbm[k], w_vmem[k], w_sem.at[k])

    @pl.when(first)
    def _():
        for k in range(N_TAIL_WEIGHTS):
            w_copy(k).start()
        _build_na_bias(rpb_ref, bt_ref)

    ks = pl.multiple_of(jnp.clip(TOK * t - A_WINDOW, 0, DEC_SEQ - A_KEYS), LANES)
    qpos = TOK * t + lax.broadcasted_iota(jnp.int32, (TOK, A_KEYS), 0)
    kpos = ks + lax.broadcasted_iota(jnp.int32, (TOK, A_KEYS), 1)
    a_valid = jnp.abs(qpos - kpos) <= A_WINDOW
    ck_t, cv_t = cka_ref[0], cva_ref[0]
    ck_pl = [_place_rows(ck_t, i // 2, i % 2).astype(BF16) for i in range(4)]
    cv_pl = [_place_rows(cv_t, i // 2, i % 2).astype(BF16) for i in range(4)]
    ya = []
    for g in range(0, A_HEADS, ATTN_GROUP):
        score_fns, values, sinks = [], [], []
        for head in range(g, g + ATTN_GROUP):
            c, half = divmod(head, 2)
            i = 2 * (c // 2) + half

            def a_scores(c=c, i=i):
                qc = qa_ref[0, :, c * LANES:(c + 1) * LANES]
                k_loc = ka_ref[0, pl.ds(ks, A_KEYS), i * LANES:(i + 1) * LANES]
                return [jnp.where(a_valid, _dot_nt(qc, k_loc), NEG), _dot(qc, ck_pl[i])]

            score_fns.append(a_scores)
            values.append([(va_ref[0, pl.ds(ks, A_KEYS), i * LANES:(i + 1) * LANES], False), (cv_pl[i], True)])
            sinks.append(sink_ref[head] * LOG2E)
        ya.append(_merge_head_pairs(_attend_heads(score_fns, values, sinks)))
    ya = jnp.concatenate(ya, axis=1)

    kr0 = jnp.clip(4 * t - 4, 0, DEC_SEQ // GRID_W - B_KROWS)
    kst = pl.multiple_of(kr0 * GRID_W, TOK)
    lane_lo = lax.broadcasted_iota(jnp.int32, (1, LANES), 1) < HEAD_DIM
    b_dr, b_valid = [], []
    for qi in range(TOK // GRID_W):
        r = 4 * t + qi
        rs = jnp.clip(r - NA_ROWS // 2, 0, DEC_SEQ // GRID_W - NA_ROWS)
        drs, valids = [], []
        for p in range(B_KROWS // 2):
            kr = kr0 + 2 * p
            drs.append(jnp.clip(kr - r + (NA_ROWS - 1), 0, 2 * NA_ROWS - 2))
            ok = [((k >= rs) & (k < rs + NA_ROWS)).astype(jnp.int32) for k in (kr, kr + 1)]
            valids.append(jnp.where(lane_lo, ok[0], ok[1]) != 0)
        b_dr.append(drs)
        b_valid.append(valids)
    lo_k = lax.broadcasted_iota(jnp.int32, (B_KEYS, LANES), 1) < HEAD_DIM
    lo_f = lax.broadcasted_iota(jnp.int32, (LANES, PAST_LEN), 0) < HEAD_DIM
    zero = jnp.zeros((), BF16)
    yb = []
    for g in range(0, B_HEADS, ATTN_GROUP):
        score_fns, values = [], []
        for head in range(g, g + ATTN_GROUP):
            c, half = divmod(head, 2)
            sl = slice(c * LANES, (c + 1) * LANES)
            hm_k = lo_k if half == 0 else jnp.logical_not(lo_k)
            hm_f = lo_f if half == 0 else jnp.logical_not(lo_f)

            def b_scores(head=head, sl=sl, hm_k=hm_k, hm_f=hm_f):
                qc = qb_ref[0, :, sl]
                kk = jnp.where(hm_k, kb_ref[0, pl.ds(kst, B_KEYS), sl], zero)
                kc = jnp.where(hm_f, ckb_ref[0, sl, :], 0.0).astype(BF16)
                s = _dot_nt(qc, kk)
                rows = []
                for qi in range(TOK // GRID_W):
                    blocks = []
                    for p in range(B_KROWS // 2):
                        blk = s[qi * GRID_W:(qi + 1) * GRID_W, p * LANES:(p + 1) * LANES] + bt_ref[head, b_dr[qi][p]]
                        blocks.append(jnp.where(b_valid[qi][p], blk, NEG))
                    rows.append(jnp.concatenate(blocks, axis=1))
                return [jnp.concatenate(rows, axis=0), _dot(qc, kc)]

            score_fns.append(b_scores)
            vv = jnp.where(hm_k, vb_ref[0, pl.ds(kst, B_KEYS), sl], zero)
            vc = jnp.where(hm_f, cvb_ref[0, sl, :], 0.0).astype(BF16)
            values.append([(vv, False), (vc, True)])
        yb.append(_merge_head_pairs(_attend_heads(score_fns, values, [None] * ATTN_GROUP)))
    yb = jnp.concatenate(yb, axis=1)

    @pl.when(first)
    def _():
        for k in range(N_TAIL_WEIGHTS):
            w_copy(k).wait()

    y_ref[0] = _tail(x, ya, yb, ga_ref[0], gb_ref[0], mod_ref, row, woa_ref, wob_ref, wo_ref, gf_ref,
                     wg_ref, wu_ref, wd_ref)


def _lat_attn(sink, x, mods, qa, ka4, va4, qb, kb, vb, ga, gb, cka, cva, ckb, cvb, rpb_rows, gf, weights):
    nb, n, _ = x.shape
    nt = n // TOK
    tok_spec = lambda w: pl.BlockSpec((1, TOK, w), lambda b, t: (b, t, 0))
    seq_spec = lambda rows, w: pl.BlockSpec((1, rows, w), lambda b, t: (b, 0, 0),
                                            pipeline_mode=pl.Buffered(1))
    in_specs = [
        pl.BlockSpec(memory_space=pltpu.SMEM),
        tok_spec(D_MODEL),
        _const_spec(mods.shape),
        tok_spec(A_Q), seq_spec(n, 4 * LANES), seq_spec(n, 4 * LANES),
        tok_spec(B_W), seq_spec(n, B_W), seq_spec(n, B_W),
        tok_spec(D_MODEL), tok_spec(D_MODEL),
        seq_spec(A_KV, PAST_LEN), seq_spec(A_KV, PAST_LEN), seq_spec(B_W, PAST_LEN), seq_spec(B_W, PAST_LEN),
        _const_spec(rpb_rows.shape), _const_spec(gf.shape),
    ] + [pl.BlockSpec(memory_space=pl.ANY)] * len(weights)
    scratch = ([pltpu.VMEM((B_HEADS, 2 * NA_ROWS - 1, GRID_W, LANES), F32)]
               + [pltpu.VMEM(w.shape, BF16) for w in weights] + [pltpu.SemaphoreType.DMA((len(weights),))])
    return pl.pallas_call(
        _lat_attn_kernel,
        grid=(nb, nt),
        in_specs=in_specs,
        out_specs=tok_spec(D_MODEL),
        out_shape=jax.ShapeDtypeStruct((nb, n, D_MODEL), F32),
        scratch_shapes=scratch,
        compiler_params=pltpu.CompilerParams(dimension_semantics=("arbitrary", "arbitrary"),
                                             vmem_limit_bytes=VMEM_LIMIT),
        name="lat_attn",
    )(sink, x, mods, qa, ka4, va4, qb, kb, vb, ga, gb, cka, cva, ckb, cvb, rpb_rows, gf, *weights)


def _rope_tables(n):
    inv_freq = 1.0 / (ROPE_BASE ** (np.arange(ROPE_HALF, dtype=np.float64) / ROPE_HALF))
    t = np.arange(n)
    parts_c, parts_s = [], []
    for pos in (t // GRID_W, t % GRID_W):
        ang = pos.astype(np.float64)[:, None] * inv_freq[None, :]
        c, s = np.cos(ang), np.sin(ang)
        parts_c += [c, c]
        parts_s += [-s, s]
    cos = np.concatenate(parts_c * 2, axis=1).astype(np.float32)
    sin = np.concatenate(parts_s * 2, axis=1).astype(np.float32)
    return jnp.asarray(cos), jnp.asarray(sin)


def _feature_major(cache, l):
    b, _, s, h, d = cache.shape
    return jnp.transpose(cache[:, l], (0, 2, 3, 1)).reshape(b, h * d, s)


def _token_major(x, heads):
    b, _, s = x.shape
    return jnp.transpose(x.reshape(b, 1, heads, HEAD_DIM, s), (0, 1, 4, 2, 3))


def kernel(x_prompt, x_sample, cache_a_k, cache_a_v, cache_b_k, cache_b_v, c, c_ctx, w_ada, b_ada, g_attn, w_in, q_norm_a, k_norm_a, q_norm_b, k_norm_b, sink_a, rpb_b, w_out_a, w_out_b, w_o, g_ffn, w_gate, w_up, w_down):
    nd, n, _ = x_sample.shape
    assert w_in.shape == (1, D_MODEL, D_IN), "single-layer trunk with the stated widths only"
    assert x_prompt.shape[1:] == (SEQ, D_MODEL) and x_sample.shape[1:] == (DEC_SEQ, D_MODEL)
    assert cache_a_k.shape[2] == PAST_LEN and nd <= N_MOD_ROWS - 1
    l = 0

    cvec = jnp.concatenate([c_ctx[None, :], c, jnp.zeros((N_MOD_ROWS - 1 - nd, D_MODEL), F32)], axis=0)
    mods = _ada(cvec, w_ada[l], b_ada[l][None, :])

    gattn, gf = g_attn[l][None, :], g_ffn[l][None, :]
    nrm = jnp.stack([jnp.tile(v[l], A_Q // HEAD_DIM) for v in (q_norm_a, k_norm_a, q_norm_b, k_norm_b)])
    sink = sink_a[l]
    weights = (w_in[l], w_out_a[l], w_out_b[l], w_o[l], w_gate[l], w_up[l], w_down[l])

    (y_prompt, ka, va, kb, vb, win, woa, wob, wo, wg, wu, wd) = _ctx_layer(
        x_prompt, mods, sink, gattn, nrm, gf, weights)
    new_a_k = _token_major(ka, A_KV_HEADS)
    new_a_v = _token_major(va, A_KV_HEADS)
    new_b_k = _token_major(kb, B_HEADS)
    new_b_v = _token_major(vb, B_HEADS)

    cos, sin = _rope_tables(n)
    qa, ka4, va4, qb, lkb, lvb, ga, gb = _lat_proj(x_sample, mods, gattn, win, nrm, cos, sin)

    rp = jnp.pad(rpb_b[l], ((0, 0), (0, 1), (0, HEAD_DIM - (2 * NA_COLS - 1))))
    rpb_rows = jnp.concatenate([rp[:, :-1], rp[:, 1:]], axis=-1)

    y_sample = _lat_attn(sink, x_sample, mods, qa, ka4, va4, qb, lkb, lvb, ga, gb,
                         _feature_major(cache_a_k, l), _feature_major(cache_a_v, l),
                         _feature_major(cache_b_k, l), _feature_major(cache_b_v, l),
                         rpb_rows, gf, (woa, wob, wo, wg, wu, wd))
    return (y_prompt, y_sample, new_a_k, new_a_v, new_b_k, new_b_v)
```

```python
import numpy as np

import jax
import jax.numpy as jnp
from jax import lax
from jax.experimental import pallas as pl
from jax.experimental.pallas import tpu as pltpu

D_MODEL = 1024
SEQ = 256
DEC_SEQ = 1024
PAST_LEN = 256
GRID_W = 64
HEAD_DIM = 64
A_HEADS = 8
A_KV_HEADS = 2
A_WINDOW = 128
B_HEADS = 8
NA_ROWS = 8
NA_COLS = 16
D_FF = 2816
ROPE_BASE = 10000.0
RMS_EPS = 1e-6
NEG = -1e30
LOG2E = float(np.log2(np.e))

A_Q = A_HEADS * HEAD_DIM
A_KV = A_KV_HEADS * HEAD_DIM
B_W = B_HEADS * HEAD_DIM
O_QA, O_KA, O_VA = 0, A_Q, A_Q + A_KV
O_QB = A_Q + 2 * A_KV
O_KB, O_VB = O_QB + B_W, O_QB + 2 * B_W
O_GA = O_QB + 3 * B_W
O_GB = O_GA + D_MODEL
D_IN = O_GB + D_MODEL

LANES = 128
TOK = 256
VMEM_LIMIT = 58 * 1024 * 1024
N_MOD_ROWS = 8

F32 = jnp.float32
BF16 = jnp.bfloat16


def _dot(a, b):
    return jnp.dot(a, b, preferred_element_type=F32)


def _dot_nt(a, b):
    return lax.dot_general(a, b, (((1,), (1,)), ((), ())), preferred_element_type=F32)


def _split_bf16(x):
    hi = x.astype(BF16)
    lo = (x - hi.astype(F32)).astype(BF16)
    return hi, lo


def _sigmoid(x):
    return 1.0 / (1.0 + jnp.exp(-x))


def _rms_mod(x, g, shift, scale):
    ms = jnp.mean(x * x, axis=-1, keepdims=True)
    return (x * lax.rsqrt(ms + RMS_EPS) * g) * (1.0 + scale) + shift


def _head_norm(u, w):
    lo_mask = lax.broadcasted_iota(jnp.int32, (u.shape[0], LANES), 1) < HEAD_DIM
    parts = []
    for c in range(u.shape[1] // LANES):
        uc = u[:, c * LANES:(c + 1) * LANES]
        uu = uc * uc
        ss_lo = jnp.sum(jnp.where(lo_mask, uu, 0.0), axis=-1, keepdims=True)
        ss_hi = jnp.sum(jnp.where(lo_mask, 0.0, uu), axis=-1, keepdims=True)
        ss = jnp.where(lo_mask, ss_lo, ss_hi)
        parts.append(uc * lax.rsqrt(ss * (1.0 / HEAD_DIM) + RMS_EPS))
    un = parts[0] if len(parts) == 1 else jnp.concatenate(parts, axis=1)
    return un * w


def _project(h, win_ref, nrm_ref):
    qa = _head_norm(_dot(h, win_ref[:, O_QA:O_KA]), nrm_ref[0:1, :])
    ka = _head_norm(_dot(h, win_ref[:, O_KA:O_VA]), nrm_ref[1:2, :A_KV])
    va = _dot(h, win_ref[:, O_VA:O_QB])
    qb = _head_norm(_dot(h, win_ref[:, O_QB:O_KB]), nrm_ref[2:3, :])
    kb = _head_norm(_dot(h, win_ref[:, O_KB:O_VB]), nrm_ref[3:4, :])
    vb = _dot(h, win_ref[:, O_VB:O_GA])
    ga = _dot(h, win_ref[:, O_GA:O_GB])
    gb = _dot(h, win_ref[:, O_GB:D_IN])
    return qa, ka, va, qb, kb, vb, ga, gb


def _place_halves(x, lo_mask):
    h0_lo = jnp.where(lo_mask, x, 0.0)
    h1_hi = jnp.where(lo_mask, 0.0, x)
    return h0_lo, pltpu.roll(h0_lo, HEAD_DIM, 1), pltpu.roll(h1_hi, HEAD_DIM, 1), h1_hi


def _place_rows(xt, head, half):
    rows = xt[head * HEAD_DIM:(head + 1) * HEAD_DIM]
    zero = jnp.zeros_like(rows)
    return jnp.concatenate([rows, zero] if half == 0 else [zero, rows], axis=0)


def _row_reduce(blocks, combine, lane_reduce):
    acc = None
    for s in blocks:
        for c in range(s.shape[1] // LANES):
            chunk = s[:, c * LANES:(c + 1) * LANES]
            acc = chunk if acc is None else combine(acc, chunk)
    return lane_reduce(acc, axis=-1, keepdims=True)


def _attend_heads(score_fns, values, sinks):
    scores = [fn() for fn in score_fns]
    probs = []
    for blocks, sink in zip(scores, sinks):
        m = _row_reduce(blocks, jnp.maximum, jnp.max)
        if sink is not None:
            m = jnp.maximum(m, sink)
        ps = [jnp.exp2(s - m) for s in blocks]
        l = _row_reduce(ps, jnp.add, jnp.sum)
        if sink is not None:
            l = l + jnp.exp2(sink - m)
        probs.append(([p.astype(BF16) for p in ps], 1.0 / l))
    outs = []
    for (ps, inv_l), vals in zip(probs, values):
        out = None
        for p, (v, v_fm) in zip(ps, vals):
            o = _dot_nt(p, v) if v_fm else _dot(p, v)
            out = o if out is None else out + o
        outs.append(out * inv_l)
    return outs


def _merge_head_pairs(outs):
    return jnp.concatenate([outs[i] + outs[i + 1] for i in range(0, len(outs), 2)], axis=1)


def _mod(mod_ref, row, i):
    return mod_ref[pl.ds(row, 1), i * D_MODEL:(i + 1) * D_MODEL]


def _tail(x, ya, yb, ga, gb, mod_ref, row, woa_ref, wob_ref, wo_ref, gf_ref, wg_ref, wu_ref, wd_ref):
    gt1, sh2, sc2, gt2 = (_mod(mod_ref, row, i) for i in (2, 3, 4, 5))
    ma = _dot(ya.astype(BF16), woa_ref[...])
    mb = _dot(yb.astype(BF16), wob_ref[...])
    mg = _sigmoid(ga) * ma + _sigmoid(gb) * mb
    x1 = x + gt1 * _dot(mg.astype(BF16), wo_ref[...])
    h2 = _rms_mod(x1, gf_ref[...], sh2, sc2).astype(BF16)
    gate = _dot(h2, wg_ref[...])
    up = _dot(h2, wu_ref[...])
    act = (gate * _sigmoid(gate)) * up
    return x1 + gt2 * _dot(act.astype(BF16), wd_ref[...])


ADA_TK = 128
ADA_SLOTS = 4


def _ada_kernel(c_ref, b_ref, w_hbm, o_ref, stage_ref, sem):
    n_chunks = w_hbm.shape[0] // ADA_TK

    def copy(k):
        return pltpu.make_async_copy(w_hbm.at[pl.ds(k * ADA_TK, ADA_TK)], stage_ref.at[k % ADA_SLOTS],
                                     sem.at[k % ADA_SLOTS])

    for k in range(min(ADA_SLOTS, n_chunks)):
        copy(k).start(priority=k % 2)
    c = c_ref[...]
    s = c * _sigmoid(c)
    acc = jnp.broadcast_to(b_ref[...], o_ref.shape)
    for k in range(n_chunks):
        s_hi, s_lo = _split_bf16(s[:, k * ADA_TK:(k + 1) * ADA_TK])
        copy(k).wait()
        w_hi, w_lo = _split_bf16(stage_ref[k % ADA_SLOTS])
        if k + ADA_SLOTS < n_chunks:
            copy(k + ADA_SLOTS).start(priority=(k + ADA_SLOTS) % 2)
        r = _dot(jnp.concatenate([s_hi, s_lo], axis=0), w_hi)
        acc = acc + (r[:N_MOD_ROWS] + r[N_MOD_ROWS:] + _dot(s_hi, w_lo))
    o_ref[...] = acc


def _ada(cvec, w, b):
    k, n = w.shape
    return pl.pallas_call(
        _ada_kernel,
        in_specs=[pl.BlockSpec(memory_space=pltpu.VMEM), pl.BlockSpec(memory_space=pltpu.VMEM),
                  pl.BlockSpec(memory_space=pl.ANY)],
        out_specs=pl.BlockSpec(memory_space=pltpu.VMEM),
        out_shape=jax.ShapeDtypeStruct((N_MOD_ROWS, n), F32),
        scratch_shapes=[pltpu.VMEM((ADA_SLOTS, ADA_TK, n), F32), pltpu.SemaphoreType.DMA((ADA_SLOTS,))],
        name="ada_mod",
    )(cvec, b, w)


W_CHUNK_ROWS = {D_IN: 32, D_MODEL: 128, D_FF: 64}
DMA_DEPTH = 6
N_WEIGHTS = 7
CTX_GROUP = 1


def _load_cast_all(srcs, dsts, stages, sems, on_done):
    tasks, ring_pos = [], {}
    for w, (src, dst) in enumerate(zip(srcs, dsts)):
        n_rows, cols = src.shape
        rows = W_CHUNK_ROWS[cols]
        for r0 in range(0, n_rows, rows):
            pos = ring_pos.get(cols, 0)
            ring_pos[cols] = pos + 1
            tasks.append((w, r0, rows, cols, pos % DMA_DEPTH, r0 + rows == n_rows))

    def copy(task):
        w, r0, rows, cols, slot, _ = task
        return pltpu.make_async_copy(srcs[w].at[pl.ds(r0, rows)], stages[cols].at[slot], sems[cols].at[slot])

    for i, task in enumerate(tasks[:DMA_DEPTH]):
        copy(task).start(priority=i % 2)
    for i, task in enumerate(tasks):
        w, r0, rows, cols, slot, last = task
        copy(task).wait()
        dsts[w][pl.ds(r0, rows), :] = stages[cols][slot].astype(BF16)
        if i + DMA_DEPTH < len(tasks):
            copy(tasks[i + DMA_DEPTH]).start(priority=(i + DMA_DEPTH) % 2)
        if last:
            on_done(w)


def _ctx_kernel(sink_ref, x_ref, mod_ref, gattn_ref, nrm_ref, gf_ref,
                win_hbm, woa_hbm, wob_hbm, wo_hbm, wg_hbm, wu_hbm, wd_hbm,
                y_ref, ka_ref, va_ref, kb_ref, vb_ref,
                win_out, woa_out, wob_out, wo_out, wg_out, wu_out, wd_out,
                win_ref, woa_ref, wob_ref, wo_ref, wg_ref, wu_ref, wd_ref, out_sem):
    srcs = (win_hbm, woa_hbm, wob_hbm, wo_hbm, wg_hbm, wu_hbm, wd_hbm)
    dsts = (win_ref, woa_ref, wob_ref, wo_ref, wg_ref, wu_ref, wd_ref)
    outs = (win_out, woa_out, wob_out, wo_out, wg_out, wu_out, wd_out)
    first = pl.program_id(0) == 0

    def out_copy(i):
        return pltpu.make_async_copy(dsts[i], outs[i], out_sem.at[i])

    @pl.when(first)
    def _():
        widths = tuple(W_CHUNK_ROWS)

        def scoped(*refs):
            stages = dict(zip(widths, refs[:len(widths)]))
            sems = dict(zip(widths, refs[len(widths):]))
            _load_cast_all(srcs, dsts, stages, sems, lambda w: out_copy(w).start())

        pl.run_scoped(scoped,
                      *[pltpu.VMEM((DMA_DEPTH, W_CHUNK_ROWS[c], c), F32) for c in widths],
                      *[pltpu.SemaphoreType.DMA((DMA_DEPTH,)) for _ in widths])

    x = x_ref[0]
    h = _rms_mod(x, gattn_ref[...], _mod(mod_ref, 0, 0), _mod(mod_ref, 0, 1)).astype(BF16)
    qa, ka, va, qb, kb, vb, ga, gb = _project(h, win_ref, nrm_ref)
    ka_ref[0] = ka.T
    va_ref[0] = va.T
    kb_ref[0] = kb.T
    vb_ref[0] = vb.T

    lo_mask = lax.broadcasted_iota(jnp.int32, (TOK, LANES), 1) < HEAD_DIM
    scale = HEAD_DIM ** -0.5 * LOG2E
    qa = (qa * scale).astype(BF16)
    qb = (qb * scale).astype(BF16)

    k_pl = [t.astype(BF16) for t in _place_halves(ka, lo_mask)]
    v_pl = [t.astype(BF16) for t in _place_halves(va, lo_mask)]
    ya = []
    for g in range(0, A_HEADS, CTX_GROUP):
        score_fns, values, sinks = [], [], []
        for head in range(g, g + CTX_GROUP):
            c, half = divmod(head, 2)
            i = 2 * (c // 2) + half
            score_fns.append(lambda c=c, i=i: [_dot_nt(qa[:, c * LANES:(c + 1) * LANES], k_pl[i])])
            values.append([(v_pl[i], False)])
            sinks.append(sink_ref[head] * LOG2E)
        ya += _attend_heads(score_fns, values, sinks)
    ya = _merge_head_pairs(ya)

    yb = []
    for g in range(0, B_HEADS, CTX_GROUP):
        score_fns, values = [], []
        for head in range(g, g + CTX_GROUP):
            c, half = divmod(head, 2)
            sl = slice(c * LANES, (c + 1) * LANES)
            hm = lo_mask if half == 0 else jnp.logical_not(lo_mask)
            km = jnp.where(hm, kb[:, sl], 0.0).astype(BF16)
            vm = jnp.where(hm, vb[:, sl], 0.0).astype(BF16)
            score_fns.append(lambda sl=sl, km=km: [_dot_nt(qb[:, sl], km)])
            values.append([(vm, False)])
        yb += _attend_heads(score_fns, values, [None] * CTX_GROUP)
    yb = _merge_head_pairs(yb)

    y_ref[0] = _tail(x, ya, yb, ga, gb, mod_ref, 0, woa_ref, wob_ref, wo_ref, gf_ref, wg_ref, wu_ref, wd_ref)

    @pl.when(first)
    def _():
        for i in range(N_WEIGHTS):
            out_copy(i).wait()


def _const_spec(shape):
    nd = len(shape)
    return pl.BlockSpec(shape, lambda *_: (0,) * nd, pipeline_mode=pl.Buffered(1))


def _ctx_layer(x, mods, sink, gattn, nrm, gf, weights):
    nb = x.shape[0]
    tok_spec = lambda w: pl.BlockSpec((1, SEQ, w), lambda b: (b, 0, 0))
    fm_spec = lambda w: pl.BlockSpec((1, w, SEQ), lambda b: (b, 0, 0))
    any_spec = pl.BlockSpec(memory_space=pl.ANY)
    in_specs = [
        pl.BlockSpec(memory_space=pltpu.SMEM),
        tok_spec(D_MODEL),
        _const_spec(mods.shape), _const_spec(gattn.shape), _const_spec(nrm.shape), _const_spec(gf.shape),
    ] + [any_spec] * len(weights)
    out_specs = ([tok_spec(D_MODEL), fm_spec(A_KV), fm_spec(A_KV), fm_spec(B_W), fm_spec(B_W)]
                 + [any_spec] * len(weights))
    out_shape = ([jax.ShapeDtypeStruct((nb, SEQ, D_MODEL), F32)]
                 + [jax.ShapeDtypeStruct((nb, w, SEQ), F32) for w in (A_KV, A_KV, B_W, B_W)]
                 + [jax.ShapeDtypeStruct(w.shape, BF16) for w in weights])
    scratch = [pltpu.VMEM(w.shape, BF16) for w in weights] + [pltpu.SemaphoreType.DMA((len(weights),))]
    return pl.pallas_call(
        _ctx_kernel,
        grid=(nb,),
        in_specs=in_specs,
        out_specs=out_specs,
        out_shape=out_shape,
        scratch_shapes=scratch,
        compiler_params=pltpu.CompilerParams(dimension_semantics=("arbitrary",), vmem_limit_bytes=VMEM_LIMIT),
        name="ctx_layer",
    )(sink, x, mods, gattn, nrm, gf, *weights)


ROPE_HALF = HEAD_DIM // 4


def _rope(x, cos, sin, upper_half):
    swapped = jnp.where(upper_half, pltpu.roll(x, ROPE_HALF, 1), pltpu.roll(x, LANES - ROPE_HALF, 1))
    return x * cos + swapped * sin


def _lat_proj_kernel(x_ref, mod_ref, gattn_ref, win_ref, nrm_ref, cos_ref, sin_ref,
                     qa_ref, ka_ref, va_ref, qb_ref, kb_ref, vb_ref, ga_ref, gb_ref):
    row = 1 + pl.program_id(0)
    x = x_ref[0]
    h = _rms_mod(x, gattn_ref[...], _mod(mod_ref, row, 0), _mod(mod_ref, row, 1)).astype(BF16)
    qa, ka, va, qb, kb, vb, ga, gb = _project(h, win_ref, nrm_ref)
    lane = lax.broadcasted_iota(jnp.int32, (TOK, LANES), 1)
    lo_mask = lane < HEAD_DIM
    upper_half = (lane & ROPE_HALF) != 0
    cos, sin = cos_ref[...], sin_ref[...]
    scale = HEAD_DIM ** -0.5 * LOG2E
    for c in range(A_Q // LANES):
        sl = slice(c * LANES, (c + 1) * LANES)
        qa_ref[0, :, sl] = (_rope(qa[:, sl], cos, sin, upper_half) * scale).astype(BF16)
    ka = _rope(ka, cos, sin, upper_half)
    for i, t in enumerate(_place_halves(ka, lo_mask)):
        ka_ref[0, :, i * LANES:(i + 1) * LANES] = t.astype(BF16)
    for i, t in enumerate(_place_halves(va, lo_mask)):
        va_ref[0, :, i * LANES:(i + 1) * LANES] = t.astype(BF16)
    qb_ref[0] = (qb * scale).astype(BF16)
    kb_ref[0] = kb.astype(BF16)
    vb_ref[0] = vb.astype(BF16)
    ga_ref[0] = ga
    gb_ref[0] = gb


def _lat_proj(x, mods, gattn, win, nrm, cos, sin):
    nb, n, _ = x.shape
    nt = n // TOK
    tok_spec = lambda w: pl.BlockSpec((1, TOK, w), lambda b, t: (b, t, 0))
    in_specs = [
        tok_spec(D_MODEL),
        _const_spec(mods.shape), _const_spec(gattn.shape), _const_spec(win.shape), _const_spec(nrm.shape),
        pl.BlockSpec((TOK, LANES), lambda b, t: (t, 0)),
        pl.BlockSpec((TOK, LANES), lambda b, t: (t, 0)),
    ]
    widths = (A_Q, 4 * LANES, 4 * LANES, B_W, B_W, B_W, D_MODEL, D_MODEL)
    dtypes = (BF16,) * 6 + (F32, F32)
    return pl.pallas_call(
        _lat_proj_kernel,
        grid=(nb, nt),
        in_specs=in_specs,
        out_specs=[tok_spec(w) for w in widths],
        out_shape=[jax.ShapeDtypeStruct((nb, n, w), dt) for w, dt in zip(widths, dtypes)],
        compiler_params=pltpu.CompilerParams(dimension_semantics=("arbitrary", "arbitrary"),
                                             vmem_limit_bytes=VMEM_LIMIT),
        name="lat_proj",
    )(x, mods, gattn, win, nrm, cos, sin)


def _build_na_bias(row_ref, bt_ref):
    w = lax.broadcasted_iota(jnp.int32, (GRID_W, LANES), 0)
    kcol = lax.broadcasted_iota(jnp.int32, (GRID_W, LANES), 1) & (GRID_W - 1)
    cs = jnp.clip(w - NA_COLS // 2, 0, GRID_W - NA_COLS)
    valid = (kcol >= cs) & (kcol < cs + NA_COLS)
    for h in range(B_HEADS):
        for d in range(2 * NA_ROWS - 1):
            rows = jnp.broadcast_to(row_ref[h, d:d + 1, :] * LOG2E, (GRID_W, LANES))
            t = pltpu.roll(rows, LANES - (NA_COLS - 1), 1, stride=1, stride_axis=0)
            bt_ref[h, d] = jnp.where(valid, t, NEG)


A_KEYS = 2 * TOK
B_KROWS = 12
B_KEYS = B_KROWS * GRID_W
ATTN_GROUP = 4
N_TAIL_WEIGHTS = 6


def _lat_attn_kernel(sink_ref, x_ref, mod_ref, qa_ref, ka_ref, va_ref, qb_ref, kb_ref, vb_ref,
                     ga_ref, gb_ref, cka_ref, cva_ref, ckb_ref, cvb_ref, rpb_ref, gf_ref,
                     woa_hbm, wob_hbm, wo_hbm, wg_hbm, wu_hbm, wd_hbm, y_ref,
                     bt_ref, woa_ref, wob_ref, wo_ref, wg_ref, wu_ref, wd_ref, w_sem):
    t = pl.program_id(1)
    row = 1 + pl.program_id(0)
    first = (pl.program_id(0) == 0) & (t == 0)
    x = x_ref[0]
    w_hbm = (woa_hbm, wob_hbm, wo_hbm, wg_hbm, wu_hbm, wd_hbm)
    w_vmem = (woa_ref, wob_ref, wo_ref, wg_ref, wu_ref, wd_ref)

    def w_copy(k):
        return pltpu.make_async_copy(w_hbm[k], w_vmem[k], w_sem.at[k])

    @pl.when(first)
    def _():
        for k in range(N_TAIL_WEIGHTS):
            w_copy(k).start(priority=k % 2)
        _build_na_bias(rpb_ref, bt_ref)

    ks = pl.multiple_of(jnp.clip(TOK * t - A_WINDOW, 0, DEC_SEQ - A_KEYS), LANES)
    qpos = TOK * t + lax.broadcasted_iota(jnp.int32, (TOK, A_KEYS), 0)
    kpos = ks + lax.broadcasted_iota(jnp.int32, (TOK, A_KEYS), 1)
    a_valid = jnp.abs(qpos - kpos) <= A_WINDOW
    ck_t, cv_t = cka_ref[0], cva_ref[0]
    ck_pl = [_place_rows(ck_t, i // 2, i % 2).astype(BF16) for i in range(4)]
    cv_pl = [_place_rows(cv_t, i // 2, i % 2).astype(BF16) for i in range(4)]
    ya = []
    for g in range(0, A_HEADS, ATTN_GROUP):
        score_fns, values, sinks = [], [], []
        for head in range(g, g + ATTN_GROUP):
            c, half = divmod(head, 2)
            i = 2 * (c // 2) + half

            def a_scores(c=c, i=i):
                qc = qa_ref[0, :, c * LANES:(c + 1) * LANES]
                k_loc = ka_ref[0, pl.ds(ks, A_KEYS), i * LANES:(i + 1) * LANES]
                return [jnp.where(a_valid, _dot_nt(qc, k_loc), NEG), _dot(qc, ck_pl[i])]

            score_fns.append(a_scores)
            values.append([(va_ref[0, pl.ds(ks, A_KEYS), i * LANES:(i + 1) * LANES], False), (cv_pl[i], True)])
            sinks.append(sink_ref[head] * LOG2E)
        ya.append(_merge_head_pairs(_attend_heads(score_fns, values, sinks)))
    ya = jnp.concatenate(ya, axis=1)

    kr0 = jnp.clip(4 * t - 4, 0, DEC_SEQ // GRID_W - B_KROWS)
    kst = pl.multiple_of(kr0 * GRID_W, TOK)
    lane_lo = lax.broadcasted_iota(jnp.int32, (1, LANES), 1) < HEAD_DIM
    b_dr, b_valid = [], []
    for qi in range(TOK // GRID_W):
        r = 4 * t + qi
        rs = jnp.clip(r - NA_ROWS // 2, 0, DEC_SEQ // GRID_W - NA_ROWS)
        drs, valids = [], []
        for p in range(B_KROWS // 2):
            kr = kr0 + 2 * p
            drs.append(jnp.clip(kr - r + (NA_ROWS - 1), 0, 2 * NA_ROWS - 2))
            ok = [((k >= rs) & (k < rs + NA_ROWS)).astype(jnp.int32) for k in (kr, kr + 1)]
            valids.append(jnp.where(lane_lo, ok[0], ok[1]) != 0)
        b_dr.append(drs)
        b_valid.append(valids)
    lo_k = lax.broadcasted_iota(jnp.int32, (B_KEYS, LANES), 1) < HEAD_DIM
    lo_f = lax.broadcasted_iota(jnp.int32, (LANES, PAST_LEN), 0) < HEAD_DIM
    zero = jnp.zeros((), BF16)
    yb = []
    for g in range(0, B_HEADS, ATTN_GROUP):
        score_fns, values = [], []
        for head in range(g, g + ATTN_GROUP):
            c, half = divmod(head, 2)
            sl = slice(c * LANES, (c + 1) * LANES)
            hm_k = lo_k if half == 0 else jnp.logical_not(lo_k)
            hm_f = lo_f if half == 0 else jnp.logical_not(lo_f)

            def b_scores(head=head, sl=sl, hm_k=hm_k, hm_f=hm_f):
                qc = qb_ref[0, :, sl]
                kk = jnp.where(hm_k, kb_ref[0, pl.ds(kst, B_KEYS), sl], zero)
                kc = jnp.where(hm_f, ckb_ref[0, sl, :], 0.0).astype(BF16)
                s = _dot_nt(qc, kk)
                rows = []
                for qi in range(TOK // GRID_W):
                    blocks = []
                    for p in range(B_KROWS // 2):
                        blk = s[qi * GRID_W:(qi + 1) * GRID_W, p * LANES:(p + 1) * LANES] + bt_ref[head, b_dr[qi][p]]
                        blocks.append(jnp.where(b_valid[qi][p], blk, NEG))
                    rows.append(jnp.concatenate(blocks, axis=1))
                return [jnp.concatenate(rows, axis=0), _dot(qc, kc)]

            score_fns.append(b_scores)
            vv = jnp.where(hm_k, vb_ref[0, pl.ds(kst, B_KEYS), sl], zero)
            vc = jnp.where(hm_f, cvb_ref[0, sl, :], 0.0).astype(BF16)
            values.append([(vv, False), (vc, True)])
        yb.append(_merge_head_pairs(_attend_heads(score_fns, values, [None] * ATTN_GROUP)))
    yb = jnp.concatenate(yb, axis=1)

    @pl.when(first)
    def _():
        for k in range(N_TAIL_WEIGHTS):
            w_copy(k).wait()

    y_ref[0] = _tail(x, ya, yb, ga_ref[0], gb_ref[0], mod_ref, row, woa_ref, wob_ref, wo_ref, gf_ref,
                     wg_ref, wu_ref, wd_ref)


def _lat_attn(sink, x, mods, qa, ka4, va4, qb, kb, vb, ga, gb, cka, cva, ckb, cvb, rpb_rows, gf, weights):
    nb, n, _ = x.shape
    nt = n // TOK
    tok_spec = lambda w: pl.BlockSpec((1, TOK, w), lambda b, t: (b, t, 0))
    seq_spec = lambda rows, w: pl.BlockSpec((1, rows, w), lambda b, t: (b, 0, 0),
                                            pipeline_mode=pl.Buffered(1))
    in_specs = [
        pl.BlockSpec(memory_space=pltpu.SMEM),
        tok_spec(D_MODEL),
        _const_spec(mods.shape),
        tok_spec(A_Q), seq_spec(n, 4 * LANES), seq_spec(n, 4 * LANES),
        tok_spec(B_W), seq_spec(n, B_W), seq_spec(n, B_W),
        tok_spec(D_MODEL), tok_spec(D_MODEL),
        seq_spec(A_KV, PAST_LEN), seq_spec(A_KV, PAST_LEN), seq_spec(B_W, PAST_LEN), seq_spec(B_W, PAST_LEN),
        _const_spec(rpb_rows.shape), _const_spec(gf.shape),
    ] + [pl.BlockSpec(memory_space=pl.ANY)] * len(weights)
    scratch = ([pltpu.VMEM((B_HEADS, 2 * NA_ROWS - 1, GRID_W, LANES), F32)]
               + [pltpu.VMEM(w.shape, BF16) for w in weights] + [pltpu.SemaphoreType.DMA((len(weights),))])
    return pl.pallas_call(
        _lat_attn_kernel,
        grid=(nb, nt),
        in_specs=in_specs,
        out_specs=tok_spec(D_MODEL),
        out_shape=jax.ShapeDtypeStruct((nb, n, D_MODEL), F32),
        scratch_shapes=scratch,
        compiler_params=pltpu.CompilerParams(dimension_semantics=("arbitrary", "arbitrary"),
                                             vmem_limit_bytes=VMEM_LIMIT),
        name="lat_attn",
    )(sink, x, mods, qa, ka4, va4, qb, kb, vb, ga, gb, cka, cva, ckb, cvb, rpb_rows, gf, *weights)


def _rope_tables(n):
    inv_freq = 1.0 / (ROPE_BASE ** (np.arange(ROPE_HALF, dtype=np.float64) / ROPE_HALF))
    t = np.arange(n)
    parts_c, parts_s = [], []
    for pos in (t // GRID_W, t % GRID_W):
        ang = pos.astype(np.float64)[:, None] * inv_freq[None, :]
        c, s = np.cos(ang), np.sin(ang)
        parts_c += [c, c]
        parts_s += [-s, s]
    cos = np.concatenate(parts_c * 2, axis=1).astype(np.float32)
    sin = np.concatenate(parts_s * 2, axis=1).astype(np.float32)
    return jnp.asarray(cos), jnp.asarray(sin)


def _feature_major(cache, l):
    b, _, s, h, d = cache.shape
    return jnp.transpose(cache[:, l], (0, 2, 3, 1)).reshape(b, h * d, s)


def _token_major(x, heads):
    b, _, s = x.shape
    return jnp.transpose(x.reshape(b, 1, heads, HEAD_DIM, s), (0, 1, 4, 2, 3))


def kernel(x_prompt, x_sample, cache_a_k, cache_a_v, cache_b_k, cache_b_v, c, c_ctx, w_ada, b_ada, g_attn, w_in, q_norm_a, k_norm_a, q_norm_b, k_norm_b, sink_a, rpb_b, w_out_a, w_out_b, w_o, g_ffn, w_gate, w_up, w_down):
    nd, n, _ = x_sample.shape
    assert w_in.shape == (1, D_MODEL, D_IN), "single-layer trunk with the stated widths only"
    assert x_prompt.shape[1:] == (SEQ, D_MODEL) and x_sample.shape[1:] == (DEC_SEQ, D_MODEL)
    assert cache_a_k.shape[2] == PAST_LEN and nd <= N_MOD_ROWS - 1
    l = 0

    cvec = jnp.concatenate([c_ctx[None, :], c, jnp.zeros((N_MOD_ROWS - 1 - nd, D_MODEL), F32)], axis=0)
    mods = _ada(cvec, w_ada[l], b_ada[l][None, :])

    gattn, gf = g_attn[l][None, :], g_ffn[l][None, :]
    nrm = jnp.stack([jnp.tile(v[l], A_Q // HEAD_DIM) for v in (q_norm_a, k_norm_a, q_norm_b, k_norm_b)])
    sink = sink_a[l]
    weights = (w_in[l], w_out_a[l], w_out_b[l], w_o[l], w_gate[l], w_up[l], w_down[l])

    (y_prompt, ka, va, kb, vb, win, woa, wob, wo, wg, wu, wd) = _ctx_layer(
        x_prompt, mods, sink, gattn, nrm, gf, weights)
    new_a_k = _token_major(ka, A_KV_HEADS)
    new_a_v = _token_major(va, A_KV_HEADS)
    new_b_k = _token_major(kb, B_HEADS)
    new_b_v = _token_major(vb, B_HEADS)

    cos, sin = _rope_tables(n)
    qa, ka4, va4, qb, lkb, lvb, ga, gb = _lat_proj(x_sample, mods, gattn, win, nrm, cos, sin)

    rp = jnp.pad(rpb_b[l], ((0, 0), (0, 1), (0, HEAD_DIM - (2 * NA_COLS - 1))))
    rpb_rows = jnp.concatenate([rp[:, :-1], rp[:, 1:]], axis=-1)

    y_sample = _lat_attn(sink, x_sample, mods, qa, ka4, va4, qb, lkb, lvb, ga, gb,
                         _feature_major(cache_a_k, l), _feature_major(cache_a_v, l),
                         _feature_major(cache_b_k, l), _feature_major(cache_b_v, l),
                         rpb_rows, gf, (woa, wob, wo, wg, wu, wd))
    return (y_prompt, y_sample, new_a_k, new_a_v, new_b_k, new_b_v)
```

```python
import numpy as np

import jax
import jax.numpy as jnp
from jax import lax
from jax.experimental import pallas as pl
from jax.experimental.pallas import tpu as pltpu

D_MODEL = 1024
SEQ = 256
DEC_SEQ = 1024
PAST_LEN = 256
GRID_W = 64
HEAD_DIM = 64
A_HEADS = 8
A_KV_HEADS = 2
A_WINDOW = 128
B_HEADS = 8
NA_ROWS = 8
NA_COLS = 16
D_FF = 2816
ROPE_BASE = 10000.0
RMS_EPS = 1e-6
NEG = -1e30
LOG2E = float(np.log2(np.e))

A_Q = A_HEADS * HEAD_DIM
A_KV = A_KV_HEADS * HEAD_DIM
B_W = B_HEADS * HEAD_DIM
O_QA, O_KA, O_VA = 0, A_Q, A_Q + A_KV
O_QB = A_Q + 2 * A_KV
O_KB, O_VB = O_QB + B_W, O_QB + 2 * B_W
O_GA = O_QB + 3 * B_W
O_GB = O_GA + D_MODEL
D_IN = O_GB + D_MODEL

LANES = 128
TOK = 256
VMEM_LIMIT = 58 * 1024 * 1024
N_MOD_ROWS = 8

F32 = jnp.float32
BF16 = jnp.bfloat16


def _dot(a, b):
    return jnp.dot(a, b, preferred_element_type=F32)


def _dot_nt(a, b):
    return lax.dot_general(a, b, (((1,), (1,)), ((), ())), preferred_element_type=F32)


def _split_bf16(x):
    hi = x.astype(BF16)
    lo = (x - hi.astype(F32)).astype(BF16)
    return hi, lo


def _sigmoid(x):
    return 1.0 / (1.0 + jnp.exp(-x))


def _rms_mod(x, g, shift, scale):
    ms = jnp.mean(x * x, axis=-1, keepdims=True)
    return (x * lax.rsqrt(ms + RMS_EPS) * g) * (1.0 + scale) + shift


def _head_norm(u, w):
    lo_mask = lax.broadcasted_iota(jnp.int32, (u.shape[0], LANES), 1) < HEAD_DIM
    parts = []
    for c in range(u.shape[1] // LANES):
        uc = u[:, c * LANES:(c + 1) * LANES]
        uu = uc * uc
        ss_lo = jnp.sum(jnp.where(lo_mask, uu, 0.0), axis=-1, keepdims=True)
        ss_hi = jnp.sum(jnp.where(lo_mask, 0.0, uu), axis=-1, keepdims=True)
        ss = jnp.where(lo_mask, ss_lo, ss_hi)
        parts.append(uc * lax.rsqrt(ss * (1.0 / HEAD_DIM) + RMS_EPS))
    un = parts[0] if len(parts) == 1 else jnp.concatenate(parts, axis=1)
    return un * w


def _project(h, win_ref, nrm_ref):
    qa = _head_norm(_dot(h, win_ref[:, O_QA:O_KA]), nrm_ref[0:1, :])
    ka = _head_norm(_dot(h, win_ref[:, O_KA:O_VA]), nrm_ref[1:2, :A_KV])
    va = _dot(h, win_ref[:, O_VA:O_QB])
    qb = _head_norm(_dot(h, win_ref[:, O_QB:O_KB]), nrm_ref[2:3, :])
    kb = _head_norm(_dot(h, win_ref[:, O_KB:O_VB]), nrm_ref[3:4, :])
    vb = _dot(h, win_ref[:, O_VB:O_GA])
    ga = _dot(h, win_ref[:, O_GA:O_GB])
    gb = _dot(h, win_ref[:, O_GB:D_IN])
    return qa, ka, va, qb, kb, vb, ga, gb


def _place_halves(x, lo_mask):
    h0_lo = jnp.where(lo_mask, x, 0.0)
    h1_hi = jnp.where(lo_mask, 0.0, x)
    return h0_lo, pltpu.roll(h0_lo, HEAD_DIM, 1), pltpu.roll(h1_hi, HEAD_DIM, 1), h1_hi


def _place_rows(xt, head, half):
    rows = xt[head * HEAD_DIM:(head + 1) * HEAD_DIM]
    zero = jnp.zeros_like(rows)
    return jnp.concatenate([rows, zero] if half == 0 else [zero, rows], axis=0)


def _row_reduce(blocks, combine, lane_reduce):
    acc = None
    for s in blocks:
        for c in range(s.shape[1] // LANES):
            chunk = s[:, c * LANES:(c + 1) * LANES]
            acc = chunk if acc is None else combine(acc, chunk)
    return lane_reduce(acc, axis=-1, keepdims=True)


def _attend_heads(score_fns, values, sinks):
    scores = [fn() for fn in score_fns]
    probs = []
    for blocks, sink in zip(scores, sinks):
        m = _row_reduce(blocks, jnp.maximum, jnp.max)
        if sink is not None:
            m = jnp.maximum(m, sink)
        ps = [jnp.exp2(s - m) for s in blocks]
        l = _row_reduce(ps, jnp.add, jnp.sum)
        if sink is not None:
            l = l + jnp.exp2(sink - m)
        probs.append(([p.astype(BF16) for p in ps], 1.0 / l))
    outs = []
    for (ps, inv_l), vals in zip(probs, values):
        out = None
        for p, (v, v_fm) in zip(ps, vals):
            o = _dot_nt(p, v) if v_fm else _dot(p, v)
            out = o if out is None else out + o
        outs.append(out * inv_l)
    return outs


def _merge_head_pairs(outs):
    return jnp.concatenate([outs[i] + outs[i + 1] for i in range(0, len(outs), 2)], axis=1)


def _mod(mod_ref, row, i):
    return mod_ref[pl.ds(row, 1), i * D_MODEL:(i + 1) * D_MODEL]


def _tail(x, ya, yb, ga, gb, mod_ref, row, woa_ref, wob_ref, wo_ref, gf_ref, wg_ref, wu_ref, wd_ref):
    gt1, sh2, sc2, gt2 = (_mod(mod_ref, row, i) for i in (2, 3, 4, 5))
    ma = _dot(ya.astype(BF16), woa_ref[...])
    mb = _dot(yb.astype(BF16), wob_ref[...])
    mg = _sigmoid(ga) * ma + _sigmoid(gb) * mb
    x1 = x + gt1 * _dot(mg.astype(BF16), wo_ref[...])
    h2 = _rms_mod(x1, gf_ref[...], sh2, sc2).astype(BF16)
    gate = _dot(h2, wg_ref[...])
    up = _dot(h2, wu_ref[...])
    act = (gate * _sigmoid(gate)) * up
    return x1 + gt2 * _dot(act.astype(BF16), wd_ref[...])


ADA_TK = 128
ADA_SLOTS = 4


def _ada_kernel(c_ref, b_ref, w_hbm, o_ref, stage_ref, sem):
    n_chunks = w_hbm.shape[0] // ADA_TK

    def copy(k):
        return pltpu.make_async_copy(w_hbm.at[pl.ds(k * ADA_TK, ADA_TK)], stage_ref.at[k % ADA_SLOTS],
                                     sem.at[k % ADA_SLOTS])

    for k in range(min(ADA_SLOTS, n_chunks)):
        copy(k).start()
    c = c_ref[...]
    s = c * _sigmoid(c)
    acc = jnp.broadcast_to(b_ref[...], o_ref.shape)
    for k in range(n_chunks):
        s_hi, s_lo = _split_bf16(s[:, k * ADA_TK:(k + 1) * ADA_TK])
        copy(k).wait()
        w_hi, w_lo = _split_bf16(stage_ref[k % ADA_SLOTS])
        if k + ADA_SLOTS < n_chunks:
            copy(k + ADA_SLOTS).start()
        r = _dot(jnp.concatenate([s_hi, s_lo], axis=0), w_hi)
        acc = acc + (r[:N_MOD_ROWS] + r[N_MOD_ROWS:] + _dot(s_hi, w_lo))
    o_ref[...] = acc


def _ada(cvec, w, b):
    k, n = w.shape
    return pl.pallas_call(
        _ada_kernel,
        in_specs=[pl.BlockSpec(memory_space=pltpu.VMEM), pl.BlockSpec(memory_space=pltpu.VMEM),
                  pl.BlockSpec(memory_space=pl.ANY)],
        out_specs=pl.BlockSpec(memory_space=pltpu.VMEM),
        out_shape=jax.ShapeDtypeStruct((N_MOD_ROWS, n), F32),
        scratch_shapes=[pltpu.VMEM((ADA_SLOTS, ADA_TK, n), F32), pltpu.SemaphoreType.DMA((ADA_SLOTS,))],
        name="ada_mod",
    )(cvec, b, w)


W_CHUNK_ROWS = {D_IN: 32, D_MODEL: 128, D_FF: 64}
DMA_DEPTH = 6
N_WEIGHTS = 7
CTX_GROUP = 1


def _load_cast_all(srcs, dsts, stages, sems):
    tasks, ring_pos = [], {}
    for w, src in enumerate(srcs):
        n_rows, cols = src.shape
        rows = W_CHUNK_ROWS[cols]
        for r0 in range(0, n_rows, rows):
            pos = ring_pos.get(cols, 0)
            ring_pos[cols] = pos + 1
            tasks.append((w, r0, rows, cols, pos % DMA_DEPTH))

    def copy(task):
        w, r0, rows, cols, slot = task
        return pltpu.make_async_copy(srcs[w].at[pl.ds(r0, rows)], stages[cols].at[slot], sems[cols].at[slot])

    for task in tasks[:DMA_DEPTH]:
        copy(task).start()
    for i, task in enumerate(tasks):
        w, r0, rows, cols, slot = task
        copy(task).wait()
        dsts[w][pl.ds(r0, rows), :] = stages[cols][slot].astype(BF16)
        if i + DMA_DEPTH < len(tasks):
            copy(tasks[i + DMA_DEPTH]).start()


def _ctx_kernel(sink_ref, x_ref, mod_ref, gattn_ref, nrm_ref, gf_ref,
                win_hbm, woa_hbm, wob_hbm, wo_hbm, wg_hbm, wu_hbm, wd_hbm,
                y_ref, ka_ref, va_ref, kb_ref, vb_ref,
                win_out, woa_out, wob_out, wo_out, wg_out, wu_out, wd_out,
                win_ref, woa_ref, wob_ref, wo_ref, wg_ref, wu_ref, wd_ref, out_sem):
    srcs = (win_hbm, woa_hbm, wob_hbm, wo_hbm, wg_hbm, wu_hbm, wd_hbm)
    dsts = (win_ref, woa_ref, wob_ref, wo_ref, wg_ref, wu_ref, wd_ref)
    outs = (win_out, woa_out, wob_out, wo_out, wg_out, wu_out, wd_out)
    first = pl.program_id(0) == 0

    def out_copy(i):
        return pltpu.make_async_copy(dsts[i], outs[i], out_sem.at[i])

    @pl.when(first)
    def _():
        widths = tuple(W_CHUNK_ROWS)

        def scoped(*refs):
            stages = dict(zip(widths, refs[:len(widths)]))
            sems = dict(zip(widths, refs[len(widths):]))
            _load_cast_all(srcs, dsts, stages, sems)

        pl.run_scoped(scoped,
                      *[pltpu.VMEM((DMA_DEPTH, W_CHUNK_ROWS[c], c), F32) for c in widths],
                      *[pltpu.SemaphoreType.DMA((DMA_DEPTH,)) for _ in widths])
        for i in range(N_WEIGHTS):
            out_copy(i).start()

    x = x_ref[0]
    h = _rms_mod(x, gattn_ref[...], _mod(mod_ref, 0, 0), _mod(mod_ref, 0, 1)).astype(BF16)
    qa, ka, va, qb, kb, vb, ga, gb = _project(h, win_ref, nrm_ref)
    ka_ref[0] = ka.T
    va_ref[0] = va.T
    kb_ref[0] = kb.T
    vb_ref[0] = vb.T

    lo_mask = lax.broadcasted_iota(jnp.int32, (TOK, LANES), 1) < HEAD_DIM
    scale = HEAD_DIM ** -0.5 * LOG2E
    qa = (qa * scale).astype(BF16)
    qb = (qb * scale).astype(BF16)

    k_pl = [t.astype(BF16) for t in _place_halves(ka, lo_mask)]
    v_pl = [t.astype(BF16) for t in _place_halves(va, lo_mask)]
    ya = []
    for g in range(0, A_HEADS, CTX_GROUP):
        score_fns, values, sinks = [], [], []
        for head in range(g, g + CTX_GROUP):
            c, half = divmod(head, 2)
            i = 2 * (c // 2) + half
            score_fns.append(lambda c=c, i=i: [_dot_nt(qa[:, c * LANES:(c + 1) * LANES], k_pl[i])])
            values.append([(v_pl[i], False)])
            sinks.append(sink_ref[head] * LOG2E)
        ya += _attend_heads(score_fns, values, sinks)
    ya = _merge_head_pairs(ya)

    yb = []
    for g in range(0, B_HEADS, CTX_GROUP):
        score_fns, values = [], []
        for head in range(g, g + CTX_GROUP):
            c, half = divmod(head, 2)
            sl = slice(c * LANES, (c + 1) * LANES)
            hm = lo_mask if half == 0 else jnp.logical_not(lo_mask)
            km = jnp.where(hm, kb[:, sl], 0.0).astype(BF16)
            vm = jnp.where(hm, vb[:, sl], 0.0).astype(BF16)
            score_fns.append(lambda sl=sl, km=km: [_dot_nt(qb[:, sl], km)])
            values.append([(vm, False)])
        yb += _attend_heads(score_fns, values, [None] * CTX_GROUP)
    yb = _merge_head_pairs(yb)

    y_ref[0] = _tail(x, ya, yb, ga, gb, mod_ref, 0, woa_ref, wob_ref, wo_ref, gf_ref, wg_ref, wu_ref, wd_ref)

    @pl.when(first)
    def _():
        for i in range(N_WEIGHTS):
            out_copy(i).wait()


def _const_spec(shape):
    nd = len(shape)
    return pl.BlockSpec(shape, lambda *_: (0,) * nd, pipeline_mode=pl.Buffered(1))


def _ctx_layer(x, mods, sink, gattn, nrm, gf, weights):
    nb = x.shape[0]
    tok_spec = lambda w: pl.BlockSpec((1, SEQ, w), lambda b: (b, 0, 0))
    fm_spec = lambda w: pl.BlockSpec((1, w, SEQ), lambda b: (b, 0, 0))
    any_spec = pl.BlockSpec(memory_space=pl.ANY)
    in_specs = [
        pl.BlockSpec(memory_space=pltpu.SMEM),
        tok_spec(D_MODEL),
        _const_spec(mods.shape), _const_spec(gattn.shape), _const_spec(nrm.shape), _const_spec(gf.shape),
    ] + [any_spec] * len(weights)
    out_specs = ([tok_spec(D_MODEL), fm_spec(A_KV), fm_spec(A_KV), fm_spec(B_W), fm_spec(B_W)]
                 + [any_spec] * len(weights))
    out_shape = ([jax.ShapeDtypeStruct((nb, SEQ, D_MODEL), F32)]
                 + [jax.ShapeDtypeStruct((nb, w, SEQ), F32) for w in (A_KV, A_KV, B_W, B_W)]
                 + [jax.ShapeDtypeStruct(w.shape, BF16) for w in weights])
    scratch = [pltpu.VMEM(w.shape, BF16) for w in weights] + [pltpu.SemaphoreType.DMA((len(weights),))]
    return pl.pallas_call(
        _ctx_kernel,
        grid=(nb,),
        in_specs=in_specs,
        out_specs=out_specs,
        out_shape=out_shape,
        scratch_shapes=scratch,
        compiler_params=pltpu.CompilerParams(dimension_semantics=("arbitrary",), vmem_limit_bytes=VMEM_LIMIT),
        name="ctx_layer",
    )(sink, x, mods, gattn, nrm, gf, *weights)


ROPE_HALF = HEAD_DIM // 4


def _rope(x, cos, sin, upper_half):
    swapped = jnp.where(upper_half, pltpu.roll(x, ROPE_HALF, 1), pltpu.roll(x, LANES - ROPE_HALF, 1))
    return x * cos + swapped * sin


def _lat_proj_kernel(x_ref, mod_ref, gattn_ref, win_ref, nrm_ref, cos_ref, sin_ref,
                     qa_ref, ka_ref, va_ref, qb_ref, kb_ref, vb_ref, ga_ref, gb_ref):
    row = 1 + pl.program_id(0)
    x = x_ref[0]
    h = _rms_mod(x, gattn_ref[...], _mod(mod_ref, row, 0), _mod(mod_ref, row, 1)).astype(BF16)
    qa, ka, va, qb, kb, vb, ga, gb = _project(h, win_ref, nrm_ref)
    lane = lax.broadcasted_iota(jnp.int32, (TOK, LANES), 1)
    lo_mask = lane < HEAD_DIM
    upper_half = (lane & ROPE_HALF) != 0
    cos, sin = cos_ref[...], sin_ref[...]
    scale = HEAD_DIM ** -0.5 * LOG2E
    for c in range(A_Q // LANES):
        sl = slice(c * LANES, (c + 1) * LANES)
        qa_ref[0, :, sl] = (_rope(qa[:, sl], cos, sin, upper_half) * scale).astype(BF16)
    ka = _rope(ka, cos, sin, upper_half)
    for i, t in enumerate(_place_halves(ka, lo_mask)):
        ka_ref[0, :, i * LANES:(i + 1) * LANES] = t.astype(BF16)
    for i, t in enumerate(_place_halves(va, lo_mask)):
        va_ref[0, :, i * LANES:(i + 1) * LANES] = t.astype(BF16)
    qb_ref[0] = (qb * scale).astype(BF16)
    kb_ref[0] = kb.astype(BF16)
    vb_ref[0] = vb.astype(BF16)
    ga_ref[0] = ga
    gb_ref[0] = gb


def _lat_proj(x, mods, gattn, win, nrm, cos, sin):
    nb, n, _ = x.shape
    nt = n // TOK
    tok_spec = lambda w: pl.BlockSpec((1, TOK, w), lambda b, t: (b, t, 0))
    in_specs = [
        tok_spec(D_MODEL),
        _const_spec(mods.shape), _const_spec(gattn.shape), _const_spec(win.shape), _const_spec(nrm.shape),
        pl.BlockSpec((TOK, LANES), lambda b, t: (t, 0)),
        pl.BlockSpec((TOK, LANES), lambda b, t: (t, 0)),
    ]
    widths = (A_Q, 4 * LANES, 4 * LANES, B_W, B_W, B_W, D_MODEL, D_MODEL)
    dtypes = (BF16,) * 6 + (F32, F32)
    return pl.pallas_call(
        _lat_proj_kernel,
        grid=(nb, nt),
        in_specs=in_specs,
        out_specs=[tok_spec(w) for w in widths],
        out_shape=[jax.ShapeDtypeStruct((nb, n, w), dt) for w, dt in zip(widths, dtypes)],
        compiler_params=pltpu.CompilerParams(dimension_semantics=("arbitrary", "arbitrary"),
                                             vmem_limit_bytes=VMEM_LIMIT),
        name="lat_proj",
    )(x, mods, gattn, win, nrm, cos, sin)


def _build_na_bias(row_ref, bt_ref):
    w = lax.broadcasted_iota(jnp.int32, (GRID_W, LANES), 0)
    kcol = lax.broadcasted_iota(jnp.int32, (GRID_W, LANES), 1) & (GRID_W - 1)
    cs = jnp.clip(w - NA_COLS // 2, 0, GRID_W - NA_COLS)
    valid = (kcol >= cs) & (kcol < cs + NA_COLS)
    for h in range(B_HEADS):
        for d in range(2 * NA_ROWS - 1):
            rows = jnp.broadcast_to(row_ref[h, d:d + 1, :] * LOG2E, (GRID_W, LANES))
            t = pltpu.roll(rows, LANES - (NA_COLS - 1), 1, stride=1, stride_axis=0)
            bt_ref[h, d] = jnp.where(valid, t, NEG)


A_KEYS = 2 * TOK
B_KROWS = 12
B_KEYS = B_KROWS * GRID_W
ATTN_GROUP = 4
N_TAIL_WEIGHTS = 6


def _lat_attn_kernel(sink_ref, x_ref, mod_ref, qa_ref, ka_ref, va_ref, qb_ref, kb_ref, vb_ref,
                     ga_ref, gb_ref, cka_ref, cva_ref, ckb_ref, cvb_ref, rpb_ref, gf_ref,
                     woa_hbm, wob_hbm, wo_hbm, wg_hbm, wu_hbm, wd_hbm, y_ref,
                     bt_ref, woa_ref, wob_ref, wo_ref, wg_ref, wu_ref, wd_ref, w_sem):
    t = pl.program_id(1)
    row = 1 + pl.program_id(0)
    first = (pl.program_id(0) == 0) & (t == 0)
    x = x_ref[0]
    w_hbm = (woa_hbm, wob_hbm, wo_hbm, wg_hbm, wu_hbm, wd_hbm)
    w_vmem = (woa_ref, wob_ref, wo_ref, wg_ref, wu_ref, wd_ref)

    def w_copy(k):
        return pltpu.make_async_copy(w_hbm[k], w_vmem[k], w_sem.at[k])

    @pl.when(first)
    def _():
        for k in range(N_TAIL_WEIGHTS):
            w_copy(k).start()
        _build_na_bias(rpb_ref, bt_ref)

    ks = pl.multiple_of(jnp.clip(TOK * t - A_WINDOW, 0, DEC_SEQ - A_KEYS), LANES)
    qpos = TOK * t + lax.broadcasted_iota(jnp.int32, (TOK, A_KEYS), 0)
    kpos = ks + lax.broadcasted_iota(jnp.int32, (TOK, A_KEYS), 1)
    a_valid = jnp.abs(qpos - kpos) <= A_WINDOW
    ck_t, cv_t = cka_ref[0], cva_ref[0]
    ck_pl = [_place_rows(ck_t, i // 2, i % 2).astype(BF16) for i in range(4)]
    cv_pl = [_place_rows(cv_t, i // 2, i % 2).astype(BF16) for i in range(4)]
    ya = []
    for g in range(0, A_HEADS, ATTN_GROUP):
        score_fns, values, sinks = [], [], []
        for head in range(g, g + ATTN_GROUP):
            c, half = divmod(head, 2)
            i = 2 * (c // 2) + half

            def a_scores(c=c, i=i):
                qc = qa_ref[0, :, c * LANES:(c + 1) * LANES]
                k_loc = ka_ref[0, pl.ds(ks, A_KEYS), i * LANES:(i + 1) * LANES]
                return [jnp.where(a_valid, _dot_nt(qc, k_loc), NEG), _dot(qc, ck_pl[i])]

            score_fns.append(a_scores)
            values.append([(va_ref[0, pl.ds(ks, A_KEYS), i * LANES:(i + 1) * LANES], False), (cv_pl[i], True)])
            sinks.append(sink_ref[head] * LOG2E)
        ya.append(_merge_head_pairs(_attend_heads(score_fns, values, sinks)))
    ya = jnp.concatenate(ya, axis=1)

    kr0 = jnp.clip(4 * t - 4, 0, DEC_SEQ // GRID_W - B_KROWS)
    kst = pl.multiple_of(kr0 * GRID_W, TOK)
    lane_lo = lax.broadcasted_iota(jnp.int32, (1, LANES), 1) < HEAD_DIM
    b_dr, b_valid = [], []
    for qi in range(TOK // GRID_W):
        r = 4 * t + qi
        rs = jnp.clip(r - NA_ROWS // 2, 0, DEC_SEQ // GRID_W - NA_ROWS)
        drs, valids = [], []
        for p in range(B_KROWS // 2):
            kr = kr0 + 2 * p
            drs.append(jnp.clip(kr - r + (NA_ROWS - 1), 0, 2 * NA_ROWS - 2))
            ok = [((k >= rs) & (k < rs + NA_ROWS)).astype(jnp.int32) for k in (kr, kr + 1)]
            valids.append(jnp.where(lane_lo, ok[0], ok[1]) != 0)
        b_dr.append(drs)
        b_valid.append(valids)
    lo_k = lax.broadcasted_iota(jnp.int32, (B_KEYS, LANES), 1) < HEAD_DIM
    lo_f = lax.broadcasted_iota(jnp.int32, (LANES, PAST_LEN), 0) < HEAD_DIM
    zero = jnp.zeros((), BF16)
    yb = []
    for g in range(0, B_HEADS, ATTN_GROUP):
        score_fns, values = [], []
        for head in range(g, g + ATTN_GROUP):
            c, half = divmod(head, 2)
            sl = slice(c * LANES, (c + 1) * LANES)
            hm_k = lo_k if half == 0 else jnp.logical_not(lo_k)
            hm_f = lo_f if half == 0 else jnp.logical_not(lo_f)

            def b_scores(head=head, sl=sl, hm_k=hm_k, hm_f=hm_f):
                qc = qb_ref[0, :, sl]
                kk = jnp.where(hm_k, kb_ref[0, pl.ds(kst, B_KEYS), sl], zero)
                kc = jnp.where(hm_f, ckb_ref[0, sl, :], 0.0).astype(BF16)
                s = _dot_nt(qc, kk)
                rows = []
                for qi in range(TOK // GRID_W):
                    blocks = []
                    for p in range(B_KROWS // 2):
                        blk = s[qi * GRID_W:(qi + 1) * GRID_W, p * LANES:(p + 1) * LANES] + bt_ref[head, b_dr[qi][p]]
                        blocks.append(jnp.where(b_valid[qi][p], blk, NEG))
                    rows.append(jnp.concatenate(blocks, axis=1))
                return [jnp.concatenate(rows, axis=0), _dot(qc, kc)]

            score_fns.append(b_scores)
            vv = jnp.where(hm_k, vb_ref[0, pl.ds(kst, B_KEYS), sl], zero)
            vc = jnp.where(hm_f, cvb_ref[0, sl, :], 0.0).astype(BF16)
            values.append([(vv, False), (vc, True)])
        yb.append(_merge_head_pairs(_attend_heads(score_fns, values, [None] * ATTN_GROUP)))
    yb = jnp.concatenate(yb, axis=1)

    @pl.when(first)
    def _():
        for k in range(N_TAIL_WEIGHTS):
            w_copy(k).wait()

    y_ref[0] = _tail(x, ya, yb, ga_ref[0], gb_ref[0], mod_ref, row, woa_ref, wob_ref, wo_ref, gf_ref,
                     wg_ref, wu_ref, wd_ref)


def _lat_attn(sink, x, mods, qa, ka4, va4, qb, kb, vb, ga, gb, cka, cva, ckb, cvb, rpb_rows, gf, weights):
    nb, n, _ = x.shape
    nt = n // TOK
    tok_spec = lambda w: pl.BlockSpec((1, TOK, w), lambda b, t: (b, t, 0))
    seq_spec = lambda rows, w: pl.BlockSpec((1, rows, w), lambda b, t: (b, 0, 0),
                                            pipeline_mode=pl.Buffered(1))
    in_specs = [
        pl.BlockSpec(memory_space=pltpu.SMEM),
        tok_spec(D_MODEL),
        _const_spec(mods.shape),
        tok_spec(A_Q), seq_spec(n, 4 * LANES), seq_spec(n, 4 * LANES),
        tok_spec(B_W), seq_spec(n, B_W), seq_spec(n, B_W),
        tok_spec(D_MODEL), tok_spec(D_MODEL),
        seq_spec(A_KV, PAST_LEN), seq_spec(A_KV, PAST_LEN), seq_spec(B_W, PAST_LEN), seq_spec(B_W, PAST_LEN),
        _const_spec(rpb_rows.shape), _const_spec(gf.shape),
    ] + [pl.BlockSpec(memory_space=pl.ANY)] * len(weights)
    scratch = ([pltpu.VMEM((B_HEADS, 2 * NA_ROWS - 1, GRID_W, LANES), F32)]
               + [pltpu.VMEM(w.shape, BF16) for w in weights] + [pltpu.SemaphoreType.DMA((len(weights),))])
    return pl.pallas_call(
        _lat_attn_kernel,
        grid=(nb, nt),
        in_specs=in_specs,
        out_specs=tok_spec(D_MODEL),
        out_shape=jax.ShapeDtypeStruct((nb, n, D_MODEL), F32),
        scratch_shapes=scratch,
        compiler_params=pltpu.CompilerParams(dimension_semantics=("arbitrary", "arbitrary"),
                                             vmem_limit_bytes=VMEM_LIMIT),
        name="lat_attn",
    )(sink, x, mods, qa, ka4, va4, qb, kb, vb, ga, gb, cka, cva, ckb, cvb, rpb_rows, gf, *weights)


def _rope_tables(n):
    inv_freq = 1.0 / (ROPE_BASE ** (np.arange(ROPE_HALF, dtype=np.float64) / ROPE_HALF))
    t = np.arange(n)
    parts_c, parts_s = [], []
    for pos in (t // GRID_W, t % GRID_W):
        ang = pos.astype(np.float64)[:, None] * inv_freq[None, :]
        c, s = np.cos(ang), np.sin(ang)
        parts_c += [c, c]
        parts_s += [-s, s]
    cos = np.concatenate(parts_c * 2, axis=1).astype(np.float32)
    sin = np.concatenate(parts_s * 2, axis=1).astype(np.float32)
    return jnp.asarray(cos), jnp.asarray(sin)


def _feature_major(cache, l):
    b, _, s, h, d = cache.shape
    return jnp.transpose(cache[:, l], (0, 2, 3, 1)).reshape(b, h * d, s)


def _token_major(x, heads):
    b, _, s = x.shape
    return jnp.transpose(x.reshape(b, 1, heads, HEAD_DIM, s), (0, 1, 4, 2, 3))


def kernel(x_prompt, x_sample, cache_a_k, cache_a_v, cache_b_k, cache_b_v, c, c_ctx, w_ada, b_ada, g_attn, w_in, q_norm_a, k_norm_a, q_norm_b, k_norm_b, sink_a, rpb_b, w_out_a, w_out_b, w_o, g_ffn, w_gate, w_up, w_down):
    nd, n, _ = x_sample.shape
    assert w_in.shape == (1, D_MODEL, D_IN), "single-layer trunk with the stated widths only"
    assert x_prompt.shape[1:] == (SEQ, D_MODEL) and x_sample.shape[1:] == (DEC_SEQ, D_MODEL)
    assert cache_a_k.shape[2] == PAST_LEN and nd <= N_MOD_ROWS - 1
    l = 0

    cvec = jnp.concatenate([c_ctx[None, :], c, jnp.zeros((N_MOD_ROWS - 1 - nd, D_MODEL), F32)], axis=0)
    mods = _ada(cvec, w_ada[l], b_ada[l][None, :])

    gattn, gf = g_attn[l][None, :], g_ffn[l][None, :]
    nrm = jnp.stack([jnp.tile(v[l], A_Q // HEAD_DIM) for v in (q_norm_a, k_norm_a, q_norm_b, k_norm_b)])
    sink = sink_a[l]
    weights = (w_in[l], w_out_a[l], w_out_b[l], w_o[l], w_gate[l], w_up[l], w_down[l])

    (y_prompt, ka, va, kb, vb, win, woa, wob, wo, wg, wu, wd) = _ctx_layer(
        x_prompt, mods, sink, gattn, nrm, gf, weights)
    new_a_k = _token_major(ka, A_KV_HEADS)
    new_a_v = _token_major(va, A_KV_HEADS)
    new_b_k = _token_major(kb, B_HEADS)
    new_b_v = _token_major(vb, B_HEADS)

    cos, sin = _rope_tables(n)
    qa, ka4, va4, qb, lkb, lvb, ga, gb = _lat_proj(x_sample, mods, gattn, win, nrm, cos, sin)

    rp = jnp.pad(rpb_b[l], ((0, 0), (0, 1), (0, HEAD_DIM - (2 * NA_COLS - 1))))
    rpb_rows = jnp.concatenate([rp[:, :-1], rp[:, 1:]], axis=-1)

    y_sample = _lat_attn(sink, x_sample, mods, qa, ka4, va4, qb, lkb, lvb, ga, gb,
                         _feature_major(cache_a_k, l), _feature_major(cache_a_v, l),
                         _feature_major(cache_b_k, l), _feature_major(cache_b_v, l),
                         rpb_rows, gf, (woa, wob, wo, wg, wu, wd))
    return (y_prompt, y_sample, new_a_k, new_a_v, new_b_k, new_b_v)
```

```python
import numpy as np

import jax
import jax.numpy as jnp
from jax import lax
from jax.experimental import pallas as pl
from jax.experimental.pallas import tpu as pltpu

D_MODEL = 1024
SEQ = 256
DEC_SEQ = 1024
PAST_LEN = 256
GRID_W = 64
HEAD_DIM = 64
A_HEADS = 8
A_KV_HEADS = 2
A_WINDOW = 128
B_HEADS = 8
NA_ROWS = 8
NA_COLS = 16
D_FF = 2816
ROPE_BASE = 10000.0
RMS_EPS = 1e-6
NEG = -1e30
LOG2E = float(np.log2(np.e))

A_Q = A_HEADS * HEAD_DIM
A_KV = A_KV_HEADS * HEAD_DIM
B_W = B_HEADS * HEAD_DIM
O_QA, O_KA, O_VA = 0, A_Q, A_Q + A_KV
O_QB = A_Q + 2 * A_KV
O_KB, O_VB = O_QB + B_W, O_QB + 2 * B_W
O_GA = O_QB + 3 * B_W
O_GB = O_GA + D_MODEL
D_IN = O_GB + D_MODEL

LANES = 128
TOK = 256
VMEM_LIMIT = 58 * 1024 * 1024
N_MOD_ROWS = 8

F32 = jnp.float32
BF16 = jnp.bfloat16


def _dot(a, b):
    return jnp.dot(a, b, preferred_element_type=F32)


def _dot_nt(a, b):
    return lax.dot_general(a, b, (((1,), (1,)), ((), ())), preferred_element_type=F32)


def _split_bf16(x):
    hi = x.astype(BF16)
    lo = (x - hi.astype(F32)).astype(BF16)
    return hi, lo


def _sigmoid(x):
    return 1.0 / (1.0 + jnp.exp(-x))


def _rms_mod(x, g, shift, scale):
    ms = jnp.mean(x * x, axis=-1, keepdims=True)
    return (x * lax.rsqrt(ms + RMS_EPS) * g) * (1.0 + scale) + shift


def _head_norm(u, w):
    lo_mask = lax.broadcasted_iota(jnp.int32, (u.shape[0], LANES), 1) < HEAD_DIM
    parts = []
    for c in range(u.shape[1] // LANES):
        uc = u[:, c * LANES:(c + 1) * LANES]
        uu = uc * uc
        ss_lo = jnp.sum(jnp.where(lo_mask, uu, 0.0), axis=-1, keepdims=True)
        ss_hi = jnp.sum(jnp.where(lo_mask, 0.0, uu), axis=-1, keepdims=True)
        ss = jnp.where(lo_mask, ss_lo, ss_hi)
        parts.append(uc * lax.rsqrt(ss * (1.0 / HEAD_DIM) + RMS_EPS))
    un = parts[0] if len(parts) == 1 else jnp.concatenate(parts, axis=1)
    return un * w


def _project(h, win_ref, nrm_ref):
    qa = _head_norm(_dot(h, win_ref[:, O_QA:O_KA]), nrm_ref[0:1, :])
    ka = _head_norm(_dot(h, win_ref[:, O_KA:O_VA]), nrm_ref[1:2, :A_KV])
    va = _dot(h, win_ref[:, O_VA:O_QB])
    qb = _head_norm(_dot(h, win_ref[:, O_QB:O_KB]), nrm_ref[2:3, :])
    kb = _head_norm(_dot(h, win_ref[:, O_KB:O_VB]), nrm_ref[3:4, :])
    vb = _dot(h, win_ref[:, O_VB:O_GA])
    ga = _dot(h, win_ref[:, O_GA:O_GB])
    gb = _dot(h, win_ref[:, O_GB:D_IN])
    return qa, ka, va, qb, kb, vb, ga, gb


def _place_halves(x, lo_mask):
    h0_lo = jnp.where(lo_mask, x, 0.0)
    h1_hi = jnp.where(lo_mask, 0.0, x)
    return h0_lo, pltpu.roll(h0_lo, HEAD_DIM, 1), pltpu.roll(h1_hi, HEAD_DIM, 1), h1_hi


def _place_rows(xt, head, half):
    rows = xt[head * HEAD_DIM:(head + 1) * HEAD_DIM]
    zero = jnp.zeros_like(rows)
    return jnp.concatenate([rows, zero] if half == 0 else [zero, rows], axis=0)


def _row_reduce(blocks, combine, lane_reduce):
    acc = None
    for s in blocks:
        for c in range(s.shape[1] // LANES):
            chunk = s[:, c * LANES:(c + 1) * LANES]
            acc = chunk if acc is None else combine(acc, chunk)
    return lane_reduce(acc, axis=-1, keepdims=True)


def _attend_heads(score_fns, values, sinks):
    scores = [fn() for fn in score_fns]
    probs = []
    for blocks, sink in zip(scores, sinks):
        m = _row_reduce(blocks, jnp.maximum, jnp.max)
        if sink is not None:
            m = jnp.maximum(m, sink)
        ps = [jnp.exp2(s - m) for s in blocks]
        l = _row_reduce(ps, jnp.add, jnp.sum)
        if sink is not None:
            l = l + jnp.exp2(sink - m)
        probs.append(([p.astype(BF16) for p in ps], 1.0 / l))
    outs = []
    for (ps, inv_l), vals in zip(probs, values):
        out = None
        for p, (v, v_fm) in zip(ps, vals):
            o = _dot_nt(p, v) if v_fm else _dot(p, v)
            out = o if out is None else out + o
        outs.append(out * inv_l)
    return outs


def _merge_head_pairs(outs):
    return jnp.concatenate([outs[i] + outs[i + 1] for i in range(0, len(outs), 2)], axis=1)


def _mod(mod_ref, row, i):
    return mod_ref[pl.ds(row, 1), i * D_MODEL:(i + 1) * D_MODEL]


def _tail(x, ya, yb, ga, gb, mod_ref, row, woa_ref, wob_ref, wo_ref, gf_ref, wg_ref, wu_ref, wd_ref):
    gt1, sh2, sc2, gt2 = (_mod(mod_ref, row, i) for i in (2, 3, 4, 5))
    ma = _dot(ya.astype(BF16), woa_ref[...])
    mb = _dot(yb.astype(BF16), wob_ref[...])
    mg = _sigmoid(ga) * ma + _sigmoid(gb) * mb
    x1 = x + gt1 * _dot(mg.astype(BF16), wo_ref[...])
    h2 = _rms_mod(x1, gf_ref[...], sh2, sc2).astype(BF16)
    gate = _dot(h2, wg_ref[...])
    up = _dot(h2, wu_ref[...])
    act = (gate * _sigmoid(gate)) * up
    return x1 + gt2 * _dot(act.astype(BF16), wd_ref[...])


ADA_TK = 128
ADA_SLOTS = 4


def _ada_kernel(cctx_ref, clat_ref, b_ref, w_hbm, o_ref, stage_ref, sem):
    n_chunks = w_hbm.shape[0] // ADA_TK

    def copy(k):
        return pltpu.make_async_copy(w_hbm.at[pl.ds(k * ADA_TK, ADA_TK)], stage_ref.at[k % ADA_SLOTS],
                                     sem.at[k % ADA_SLOTS])

    for k in range(min(ADA_SLOTS, n_chunks)):
        copy(k).start()
    row = lax.broadcasted_iota(jnp.int32, (N_MOD_ROWS, D_MODEL), 0)
    c = jnp.where(row == 0, cctx_ref[...], 0.0)
    for i in range(clat_ref.shape[0]):
        c = jnp.where(row == 1 + i, clat_ref[i:i + 1, :], c)
    s = c * _sigmoid(c)
    acc = jnp.broadcast_to(b_ref[...], o_ref.shape)
    for k in range(n_chunks):
        s_hi, s_lo = _split_bf16(s[:, k * ADA_TK:(k + 1) * ADA_TK])
        copy(k).wait()
        w_hi, w_lo = _split_bf16(stage_ref[k % ADA_SLOTS])
        if k + ADA_SLOTS < n_chunks:
            copy(k + ADA_SLOTS).start()
        r = _dot(jnp.concatenate([s_hi, s_lo], axis=0), w_hi)
        acc = acc + (r[:N_MOD_ROWS] + r[N_MOD_ROWS:] + _dot(s_hi, w_lo))
    o_ref[...] = acc


def _ada(c_ctx, c_lat, w, b):
    k, n = w.shape
    vmem = pl.BlockSpec(memory_space=pltpu.VMEM)
    return pl.pallas_call(
        _ada_kernel,
        in_specs=[vmem, vmem, vmem, pl.BlockSpec(memory_space=pl.ANY)],
        out_specs=vmem,
        out_shape=jax.ShapeDtypeStruct((N_MOD_ROWS, n), F32),
        scratch_shapes=[pltpu.VMEM((ADA_SLOTS, ADA_TK, n), F32), pltpu.SemaphoreType.DMA((ADA_SLOTS,))],
        name="ada_mod",
    )(c_ctx, c_lat, b, w)


W_CHUNK_ROWS = {D_IN: 32, D_MODEL: 128, D_FF: 64}
DMA_DEPTH = 6
N_WEIGHTS = 7
CTX_GROUP = 1


def _load_cast_all(srcs, dsts, stages, sems):
    tasks, ring_pos = [], {}
    for w, src in enumerate(srcs):
        n_rows, cols = src.shape
        rows = W_CHUNK_ROWS[cols]
        for r0 in range(0, n_rows, rows):
            pos = ring_pos.get(cols, 0)
            ring_pos[cols] = pos + 1
            tasks.append((w, r0, rows, cols, pos % DMA_DEPTH))

    def copy(task):
        w, r0, rows, cols, slot = task
        return pltpu.make_async_copy(srcs[w].at[pl.ds(r0, rows)], stages[cols].at[slot], sems[cols].at[slot])

    for task in tasks[:DMA_DEPTH]:
        copy(task).start()
    for i, task in enumerate(tasks):
        w, r0, rows, cols, slot = task
        copy(task).wait()
        dsts[w][pl.ds(r0, rows), :] = stages[cols][slot].astype(BF16)
        if i + DMA_DEPTH < len(tasks):
            copy(tasks[i + DMA_DEPTH]).start()


def _ctx_kernel(sink_ref, x_ref, mod_ref, gattn_ref, nrm_ref, gf_ref,
                win_hbm, woa_hbm, wob_hbm, wo_hbm, wg_hbm, wu_hbm, wd_hbm,
                y_ref, ka_ref, va_ref, kb_ref, vb_ref,
                win_out, woa_out, wob_out, wo_out, wg_out, wu_out, wd_out,
                win_ref, woa_ref, wob_ref, wo_ref, wg_ref, wu_ref, wd_ref, out_sem):
    srcs = (win_hbm, woa_hbm, wob_hbm, wo_hbm, wg_hbm, wu_hbm, wd_hbm)
    dsts = (win_ref, woa_ref, wob_ref, wo_ref, wg_ref, wu_ref, wd_ref)
    outs = (win_out, woa_out, wob_out, wo_out, wg_out, wu_out, wd_out)
    first = pl.program_id(0) == 0

    def out_copy(i):
        return pltpu.make_async_copy(dsts[i], outs[i], out_sem.at[i])

    @pl.when(first)
    def _():
        widths = tuple(W_CHUNK_ROWS)

        def scoped(*refs):
            stages = dict(zip(widths, refs[:len(widths)]))
            sems = dict(zip(widths, refs[len(widths):]))
            _load_cast_all(srcs, dsts, stages, sems)

        pl.run_scoped(scoped,
                      *[pltpu.VMEM((DMA_DEPTH, W_CHUNK_ROWS[c], c), F32) for c in widths],
                      *[pltpu.SemaphoreType.DMA((DMA_DEPTH,)) for _ in widths])
        for i in range(N_WEIGHTS):
            out_copy(i).start()

    x = x_ref[0]
    h = _rms_mod(x, gattn_ref[...], _mod(mod_ref, 0, 0), _mod(mod_ref, 0, 1)).astype(BF16)
    qa, ka, va, qb, kb, vb, ga, gb = _project(h, win_ref, nrm_ref)
    ka_ref[0] = ka.T
    va_ref[0] = va.T
    kb_ref[0] = kb.T
    vb_ref[0] = vb.T

    lo_mask = lax.broadcasted_iota(jnp.int32, (TOK, LANES), 1) < HEAD_DIM
    scale = HEAD_DIM ** -0.5 * LOG2E
    qa = (qa * scale).astype(BF16)
    qb = (qb * scale).astype(BF16)

    k_pl = [t.astype(BF16) for t in _place_halves(ka, lo_mask)]
    v_pl = [t.astype(BF16) for t in _place_halves(va, lo_mask)]
    ya = []
    for g in range(0, A_HEADS, CTX_GROUP):
        score_fns, values, sinks = [], [], []
        for head in range(g, g + CTX_GROUP):
            c, half = divmod(head, 2)
            i = 2 * (c // 2) + half
            score_fns.append(lambda c=c, i=i: [_dot_nt(qa[:, c * LANES:(c + 1) * LANES], k_pl[i])])
            values.append([(v_pl[i], False)])
            sinks.append(sink_ref[head] * LOG2E)
        ya += _attend_heads(score_fns, values, sinks)
    ya = _merge_head_pairs(ya)

    yb = []
    for g in range(0, B_HEADS, CTX_GROUP):
        score_fns, values = [], []
        for head in range(g, g + CTX_GROUP):
            c, half = divmod(head, 2)
            sl = slice(c * LANES, (c + 1) * LANES)
            hm = lo_mask if half == 0 else jnp.logical_not(lo_mask)
            km = jnp.where(hm, kb[:, sl], 0.0).astype(BF16)
            vm = jnp.where(hm, vb[:, sl], 0.0).astype(BF16)
            score_fns.append(lambda sl=sl, km=km: [_dot_nt(qb[:, sl], km)])
            values.append([(vm, False)])
        yb += _attend_heads(score_fns, values, [None] * CTX_GROUP)
    yb = _merge_head_pairs(yb)

    y_ref[0] = _tail(x, ya, yb, ga, gb, mod_ref, 0, woa_ref, wob_ref, wo_ref, gf_ref, wg_ref, wu_ref, wd_ref)

    @pl.when(first)
    def _():
        for i in range(N_WEIGHTS):
            out_copy(i).wait()


def _const_spec(shape):
    nd = len(shape)
    return pl.BlockSpec(shape, lambda *_: (0,) * nd, pipeline_mode=pl.Buffered(1))


def _ctx_layer(x, mods, sink, gattn, nrm, gf, weights):
    nb = x.shape[0]
    tok_spec = lambda w: pl.BlockSpec((1, SEQ, w), lambda b: (b, 0, 0))
    fm_spec = lambda w: pl.BlockSpec((1, w, SEQ), lambda b: (b, 0, 0))
    any_spec = pl.BlockSpec(memory_space=pl.ANY)
    in_specs = [
        pl.BlockSpec(memory_space=pltpu.SMEM),
        tok_spec(D_MODEL),
        _const_spec(mods.shape), _const_spec(gattn.shape), _const_spec(nrm.shape), _const_spec(gf.shape),
    ] + [any_spec] * len(weights)
    out_specs = ([tok_spec(D_MODEL), fm_spec(A_KV), fm_spec(A_KV), fm_spec(B_W), fm_spec(B_W)]
                 + [any_spec] * len(weights))
    out_shape = ([jax.ShapeDtypeStruct((nb, SEQ, D_MODEL), F32)]
                 + [jax.ShapeDtypeStruct((nb, w, SEQ), F32) for w in (A_KV, A_KV, B_W, B_W)]
                 + [jax.ShapeDtypeStruct(w.shape, BF16) for w in weights])
    scratch = [pltpu.VMEM(w.shape, BF16) for w in weights] + [pltpu.SemaphoreType.DMA((len(weights),))]
    return pl.pallas_call(
        _ctx_kernel,
        grid=(nb,),
        in_specs=in_specs,
        out_specs=out_specs,
        out_shape=out_shape,
        scratch_shapes=scratch,
        compiler_params=pltpu.CompilerParams(dimension_semantics=("arbitrary",), vmem_limit_bytes=VMEM_LIMIT),
        name="ctx_layer",
    )(sink, x, mods, gattn, nrm, gf, *weights)


ROPE_HALF = HEAD_DIM // 4


def _rope(x, cos, sin, upper_half):
    swapped = jnp.where(upper_half, pltpu.roll(x, ROPE_HALF, 1), pltpu.roll(x, LANES - ROPE_HALF, 1))
    return x * cos + swapped * sin


def _lat_proj_kernel(x_ref, mod_ref, gattn_ref, win_ref, nrm_ref, cos_ref, sin_ref,
                     qa_ref, ka_ref, va_ref, qb_ref, kb_ref, vb_ref, ga_ref, gb_ref):
    row = 1 + pl.program_id(0)
    x = x_ref[0]
    h = _rms_mod(x, gattn_ref[...], _mod(mod_ref, row, 0), _mod(mod_ref, row, 1)).astype(BF16)
    qa, ka, va, qb, kb, vb, ga, gb = _project(h, win_ref, nrm_ref)
    lane = lax.broadcasted_iota(jnp.int32, (TOK, LANES), 1)
    lo_mask = lane < HEAD_DIM
    upper_half = (lane & ROPE_HALF) != 0
    cos, sin = cos_ref[...], sin_ref[...]
    scale = HEAD_DIM ** -0.5 * LOG2E
    for c in range(A_Q // LANES):
        sl = slice(c * LANES, (c + 1) * LANES)
        qa_ref[0, :, sl] = (_rope(qa[:, sl], cos, sin, upper_half) * scale).astype(BF16)
    ka = _rope(ka, cos, sin, upper_half)
    for i, t in enumerate(_place_halves(ka, lo_mask)):
        ka_ref[0, :, i * LANES:(i + 1) * LANES] = t.astype(BF16)
    for i, t in enumerate(_place_halves(va, lo_mask)):
        va_ref[0, :, i * LANES:(i + 1) * LANES] = t.astype(BF16)
    qb_ref[0] = (qb * scale).astype(BF16)
    kb_ref[0] = kb.astype(BF16)
    vb_ref[0] = vb.astype(BF16)
    ga_ref[0] = ga
    gb_ref[0] = gb


def _lat_proj(x, mods, gattn, win, nrm, cos, sin):
    nb, n, _ = x.shape
    nt = n // TOK
    tok_spec = lambda w: pl.BlockSpec((1, TOK, w), lambda b, t: (b, t, 0))
    in_specs = [
        tok_spec(D_MODEL),
        _const_spec(mods.shape), _const_spec(gattn.shape), _const_spec(win.shape), _const_spec(nrm.shape),
        pl.BlockSpec((TOK, LANES), lambda b, t: (t, 0)),
        pl.BlockSpec((TOK, LANES), lambda b, t: (t, 0)),
    ]
    widths = (A_Q, 4 * LANES, 4 * LANES, B_W, B_W, B_W, D_MODEL, D_MODEL)
    dtypes = (BF16,) * 6 + (F32, F32)
    return pl.pallas_call(
        _lat_proj_kernel,
        grid=(nb, nt),
        in_specs=in_specs,
        out_specs=[tok_spec(w) for w in widths],
        out_shape=[jax.ShapeDtypeStruct((nb, n, w), dt) for w, dt in zip(widths, dtypes)],
        compiler_params=pltpu.CompilerParams(dimension_semantics=("arbitrary", "arbitrary"),
                                             vmem_limit_bytes=VMEM_LIMIT),
        name="lat_proj",
    )(x, mods, gattn, win, nrm, cos, sin)


def _build_na_bias(row_ref, bt_ref):
    w = lax.broadcasted_iota(jnp.int32, (GRID_W, LANES), 0)
    kcol = lax.broadcasted_iota(jnp.int32, (GRID_W, LANES), 1) & (GRID_W - 1)
    cs = jnp.clip(w - NA_COLS // 2, 0, GRID_W - NA_COLS)
    valid = (kcol >= cs) & (kcol < cs + NA_COLS)
    for h in range(B_HEADS):
        for d in range(2 * NA_ROWS - 1):
            rows = jnp.broadcast_to(row_ref[h, d:d + 1, :] * LOG2E, (GRID_W, LANES))
            t = pltpu.roll(rows, LANES - (NA_COLS - 1), 1, stride=1, stride_axis=0)
            bt_ref[h, d] = jnp.where(valid, t, NEG)


A_KEYS = 2 * TOK
B_KROWS = 12
B_KEYS = B_KROWS * GRID_W
ATTN_GROUP = 4
N_TAIL_WEIGHTS = 6


def _lat_attn_kernel(sink_ref, x_ref, mod_ref, qa_ref, ka_ref, va_ref, qb_ref, kb_ref, vb_ref,
                     ga_ref, gb_ref, cka_ref, cva_ref, ckb_ref, cvb_ref, rpb_ref, gf_ref,
                     woa_hbm, wob_hbm, wo_hbm, wg_hbm, wu_hbm, wd_hbm, y_ref,
                     bt_ref, woa_ref, wob_ref, wo_ref, wg_ref, wu_ref, wd_ref, w_sem):
    t = pl.program_id(1)
    row = 1 + pl.program_id(0)
    first = (pl.program_id(0) == 0) & (t == 0)
    x = x_ref[0]
    w_hbm = (woa_hbm, wob_hbm, wo_hbm, wg_hbm, wu_hbm, wd_hbm)
    w_vmem = (woa_ref, wob_ref, wo_ref, wg_ref, wu_ref, wd_ref)

    def w_copy(k):
        return pltpu.make_async_copy(w_hbm[k], w_vmem[k], w_sem.at[k])

    @pl.when(first)
    def _():
        for k in range(N_TAIL_WEIGHTS):
            w_copy(k).start()
        _build_na_bias(rpb_ref, bt_ref)

    ks = pl.multiple_of(jnp.clip(TOK * t - A_WINDOW, 0, DEC_SEQ - A_KEYS), LANES)
    qpos = TOK * t + lax.broadcasted_iota(jnp.int32, (TOK, A_KEYS), 0)
    kpos = ks + lax.broadcasted_iota(jnp.int32, (TOK, A_KEYS), 1)
    a_valid = jnp.abs(qpos - kpos) <= A_WINDOW
    ck_t, cv_t = cka_ref[0], cva_ref[0]
    ck_pl = [_place_rows(ck_t, i // 2, i % 2).astype(BF16) for i in range(4)]
    cv_pl = [_place_rows(cv_t, i // 2, i % 2).astype(BF16) for i in range(4)]
    ya = []
    for g in range(0, A_HEADS, ATTN_GROUP):
        score_fns, values, sinks = [], [], []
        for head in range(g, g + ATTN_GROUP):
            c, half = divmod(head, 2)
            i = 2 * (c // 2) + half

            def a_scores(c=c, i=i):
                qc = qa_ref[0, :, c * LANES:(c + 1) * LANES]
                k_loc = ka_ref[0, pl.ds(ks, A_KEYS), i * LANES:(i + 1) * LANES]
                return [jnp.where(a_valid, _dot_nt(qc, k_loc), NEG), _dot(qc, ck_pl[i])]

            score_fns.append(a_scores)
            values.append([(va_ref[0, pl.ds(ks, A_KEYS), i * LANES:(i + 1) * LANES], False), (cv_pl[i], True)])
            sinks.append(sink_ref[head] * LOG2E)
        ya.append(_merge_head_pairs(_attend_heads(score_fns, values, sinks)))
    ya = jnp.concatenate(ya, axis=1)

    kr0 = jnp.clip(4 * t - 4, 0, DEC_SEQ // GRID_W - B_KROWS)
    kst = pl.multiple_of(kr0 * GRID_W, TOK)
    lane_lo = lax.broadcasted_iota(jnp.int32, (1, LANES), 1) < HEAD_DIM
    b_dr, b_valid = [], []
    for qi in range(TOK // GRID_W):
        r = 4 * t + qi
        rs = jnp.clip(r - NA_ROWS // 2, 0, DEC_SEQ // GRID_W - NA_ROWS)
        drs, valids = [], []
        for p in range(B_KROWS // 2):
            kr = kr0 + 2 * p
            drs.append(jnp.clip(kr - r + (NA_ROWS - 1), 0, 2 * NA_ROWS - 2))
            ok = [((k >= rs) & (k < rs + NA_ROWS)).astype(jnp.int32) for k in (kr, kr + 1)]
            valids.append(jnp.where(lane_lo, ok[0], ok[1]) != 0)
        b_dr.append(drs)
        b_valid.append(valids)
    lo_k = lax.broadcasted_iota(jnp.int32, (B_KEYS, LANES), 1) < HEAD_DIM
    lo_f = lax.broadcasted_iota(jnp.int32, (LANES, PAST_LEN), 0) < HEAD_DIM
    zero = jnp.zeros((), BF16)
    yb = []
    for g in range(0, B_HEADS, ATTN_GROUP):
        score_fns, values = [], []
        for head in range(g, g + ATTN_GROUP):
            c, half = divmod(head, 2)
            sl = slice(c * LANES, (c + 1) * LANES)
            hm_k = lo_k if half == 0 else jnp.logical_not(lo_k)
            hm_f = lo_f if half == 0 else jnp.logical_not(lo_f)

            def b_scores(head=head, sl=sl, hm_k=hm_k, hm_f=hm_f):
                qc = qb_ref[0, :, sl]
                kk = jnp.where(hm_k, kb_ref[0, pl.ds(kst, B_KEYS), sl], zero)
                kc = jnp.where(hm_f, ckb_ref[0, sl, :], 0.0).astype(BF16)
                s = _dot_nt(qc, kk)
                rows = []
                for qi in range(TOK // GRID_W):
                    blocks = []
                    for p in range(B_KROWS // 2):
                        blk = s[qi * GRID_W:(qi + 1) * GRID_W, p * LANES:(p + 1) * LANES] + bt_ref[head, b_dr[qi][p]]
                        blocks.append(jnp.where(b_valid[qi][p], blk, NEG))
                    rows.append(jnp.concatenate(blocks, axis=1))
                return [jnp.concatenate(rows, axis=0), _dot(qc, kc)]

            score_fns.append(b_scores)
            vv = jnp.where(hm_k, vb_ref[0, pl.ds(kst, B_KEYS), sl], zero)
            vc = jnp.where(hm_f, cvb_ref[0, sl, :], 0.0).astype(BF16)
            values.append([(vv, False), (vc, True)])
        yb.append(_merge_head_pairs(_attend_heads(score_fns, values, [None] * ATTN_GROUP)))
    yb = jnp.concatenate(yb, axis=1)

    @pl.when(first)
    def _():
        for k in range(N_TAIL_WEIGHTS):
            w_copy(k).wait()

    y_ref[0] = _tail(x, ya, yb, ga_ref[0], gb_ref[0], mod_ref, row, woa_ref, wob_ref, wo_ref, gf_ref,
                     wg_ref, wu_ref, wd_ref)


def _lat_attn(sink, x, mods, qa, ka4, va4, qb, kb, vb, ga, gb, cka, cva, ckb, cvb, rpb_rows, gf, weights):
    nb, n, _ = x.shape
    nt = n // TOK
    tok_spec = lambda w: pl.BlockSpec((1, TOK, w), lambda b, t: (b, t, 0))
    seq_spec = lambda rows, w: pl.BlockSpec((1, rows, w), lambda b, t: (b, 0, 0))
    in_specs = [
        pl.BlockSpec(memory_space=pltpu.SMEM),
        tok_spec(D_MODEL),
        _const_spec(mods.shape),
        tok_spec(A_Q), seq_spec(n, 4 * LANES), seq_spec(n, 4 * LANES),
        tok_spec(B_W), seq_spec(n, B_W), seq_spec(n, B_W),
        tok_spec(D_MODEL), tok_spec(D_MODEL),
        seq_spec(A_KV, PAST_LEN), seq_spec(A_KV, PAST_LEN), seq_spec(B_W, PAST_LEN), seq_spec(B_W, PAST_LEN),
        _const_spec(rpb_rows.shape), _const_spec(gf.shape),
    ] + [pl.BlockSpec(memory_space=pl.ANY)] * len(weights)
    scratch = ([pltpu.VMEM((B_HEADS, 2 * NA_ROWS - 1, GRID_W, LANES), F32)]
               + [pltpu.VMEM(w.shape, BF16) for w in weights] + [pltpu.SemaphoreType.DMA((len(weights),))])
    return pl.pallas_call(
        _lat_attn_kernel,
        grid=(nb, nt),
        in_specs=in_specs,
        out_specs=tok_spec(D_MODEL),
        out_shape=jax.ShapeDtypeStruct((nb, n, D_MODEL), F32),
        scratch_shapes=scratch,
        compiler_params=pltpu.CompilerParams(dimension_semantics=("arbitrary", "arbitrary"),
                                             vmem_limit_bytes=VMEM_LIMIT),
        name="lat_attn",
    )(sink, x, mods, qa, ka4, va4, qb, kb, vb, ga, gb, cka, cva, ckb, cvb, rpb_rows, gf, *weights)


def _rope_tables(n):
    inv_freq = 1.0 / (ROPE_BASE ** (np.arange(ROPE_HALF, dtype=np.float64) / ROPE_HALF))
    t = np.arange(n)
    parts_c, parts_s = [], []
    for pos in (t // GRID_W, t % GRID_W):
        ang = pos.astype(np.float64)[:, None] * inv_freq[None, :]
        c, s = np.cos(ang), np.sin(ang)
        parts_c += [c, c]
        parts_s += [-s, s]
    cos = np.concatenate(parts_c * 2, axis=1).astype(np.float32)
    sin = np.concatenate(parts_s * 2, axis=1).astype(np.float32)
    return jnp.asarray(cos), jnp.asarray(sin)


def _feature_major(cache, l):
    b, _, s, h, d = cache.shape
    return jnp.transpose(cache[:, l], (0, 2, 3, 1)).reshape(b, h * d, s)


def _token_major(x, heads):
    b, _, s = x.shape
    return jnp.transpose(x.reshape(b, 1, heads, HEAD_DIM, s), (0, 1, 4, 2, 3))


def kernel(x_prompt, x_sample, cache_a_k, cache_a_v, cache_b_k, cache_b_v, c, c_ctx, w_ada, b_ada, g_attn, w_in, q_norm_a, k_norm_a, q_norm_b, k_norm_b, sink_a, rpb_b, w_out_a, w_out_b, w_o, g_ffn, w_gate, w_up, w_down):
    nd, n, _ = x_sample.shape
    assert w_in.shape == (1, D_MODEL, D_IN), "single-layer trunk with the stated widths only"
    assert x_prompt.shape[1:] == (SEQ, D_MODEL) and x_sample.shape[1:] == (DEC_SEQ, D_MODEL)
    assert cache_a_k.shape[2] == PAST_LEN and nd <= N_MOD_ROWS - 1
    l = 0

    mods = _ada(c_ctx[None, :], c, w_ada[l], b_ada[l][None, :])

    gattn, gf = g_attn[l][None, :], g_ffn[l][None, :]
    nrm = jnp.stack([jnp.tile(v[l], A_Q // HEAD_DIM) for v in (q_norm_a, k_norm_a, q_norm_b, k_norm_b)])
    sink = sink_a[l]
    weights = (w_in[l], w_out_a[l], w_out_b[l], w_o[l], w_gate[l], w_up[l], w_down[l])

    (y_prompt, ka, va, kb, vb, win, woa, wob, wo, wg, wu, wd) = _ctx_layer(
        x_prompt, mods, sink, gattn, nrm, gf, weights)
    new_a_k = _token_major(ka, A_KV_HEADS)
    new_a_v = _token_major(va, A_KV_HEADS)
    new_b_k = _token_major(kb, B_HEADS)
    new_b_v = _token_major(vb, B_HEADS)

    cos, sin = _rope_tables(n)
    qa, ka4, va4, qb, lkb, lvb, ga, gb = _lat_proj(x_sample, mods, gattn, win, nrm, cos, sin)

    rp = jnp.pad(rpb_b[l], ((0, 0), (0, 1), (0, HEAD_DIM - (2 * NA_COLS - 1))))
    rpb_rows = jnp.concatenate([rp[:, :-1], rp[:, 1:]], axis=-1)

    y_sample = _lat_attn(sink, x_sample, mods, qa, ka4, va4, qb, lkb, lvb, ga, gb,
                         _feature_major(cache_a_k, l), _feature_major(cache_a_v, l),
                         _feature_major(cache_b_k, l), _feature_major(cache_b_v, l),
                         rpb_rows, gf, (woa, wob, wo, wg, wu, wd))
    return (y_prompt, y_sample, new_a_k, new_a_v, new_b_k, new_b_v)
```

```python
import numpy as np

import jax
import jax.numpy as jnp
from jax import lax
from jax.experimental import pallas as pl
from jax.experimental.pallas import tpu as pltpu

D_MODEL = 1024
SEQ = 256
DEC_SEQ = 1024
PAST_LEN = 256
GRID_W = 64
HEAD_DIM = 64
A_HEADS = 8
A_KV_HEADS = 2
A_WINDOW = 128
B_HEADS = 8
NA_ROWS = 8
NA_COLS = 16
D_FF = 2816
ROPE_BASE = 10000.0
RMS_EPS = 1e-6
NEG = -1e30
LOG2E = float(np.log2(np.e))

A_Q = A_HEADS * HEAD_DIM
A_KV = A_KV_HEADS * HEAD_DIM
B_W = B_HEADS * HEAD_DIM
O_QA, O_KA, O_VA = 0, A_Q, A_Q + A_KV
O_QB = A_Q + 2 * A_KV
O_KB, O_VB = O_QB + B_W, O_QB + 2 * B_W
O_GA = O_QB + 3 * B_W
O_GB = O_GA + D_MODEL
D_IN = O_GB + D_MODEL

LANES = 128
TOK = 256
VMEM_LIMIT = 58 * 1024 * 1024
N_MOD_ROWS = 8

F32 = jnp.float32
BF16 = jnp.bfloat16


def _dot(a, b):
    return jnp.dot(a, b, preferred_element_type=F32)


def _dot_nt(a, b):
    return lax.dot_general(a, b, (((1,), (1,)), ((), ())), preferred_element_type=F32)


def _split_bf16(x):
    hi = x.astype(BF16)
    lo = (x - hi.astype(F32)).astype(BF16)
    return hi, lo


def _sigmoid(x):
    return 1.0 / (1.0 + jnp.exp(-x))


def _rms_mod(x, g, shift, scale):
    ms = jnp.mean(x * x, axis=-1, keepdims=True)
    return (x * lax.rsqrt(ms + RMS_EPS) * g) * (1.0 + scale) + shift


def _head_norm(u, w):
    lo_mask = lax.broadcasted_iota(jnp.int32, (u.shape[0], LANES), 1) < HEAD_DIM
    parts = []
    for c in range(u.shape[1] // LANES):
        uc = u[:, c * LANES:(c + 1) * LANES]
        uu = uc * uc
        ss_lo = jnp.sum(jnp.where(lo_mask, uu, 0.0), axis=-1, keepdims=True)
        ss_hi = jnp.sum(jnp.where(lo_mask, 0.0, uu), axis=-1, keepdims=True)
        ss = jnp.where(lo_mask, ss_lo, ss_hi)
        parts.append(uc * lax.rsqrt(ss * (1.0 / HEAD_DIM) + RMS_EPS))
    un = parts[0] if len(parts) == 1 else jnp.concatenate(parts, axis=1)
    return un * w


def _project(h, win_ref, nrm_ref):
    qa = _head_norm(_dot(h, win_ref[:, O_QA:O_KA]), nrm_ref[0:1, :])
    ka = _head_norm(_dot(h, win_ref[:, O_KA:O_VA]), nrm_ref[1:2, :A_KV])
    va = _dot(h, win_ref[:, O_VA:O_QB])
    qb = _head_norm(_dot(h, win_ref[:, O_QB:O_KB]), nrm_ref[2:3, :])
    kb = _head_norm(_dot(h, win_ref[:, O_KB:O_VB]), nrm_ref[3:4, :])
    vb = _dot(h, win_ref[:, O_VB:O_GA])
    ga = _dot(h, win_ref[:, O_GA:O_GB])
    gb = _dot(h, win_ref[:, O_GB:D_IN])
    return qa, ka, va, qb, kb, vb, ga, gb


def _place_halves(x, lo_mask):
    h0_lo = jnp.where(lo_mask, x, 0.0)
    h1_hi = jnp.where(lo_mask, 0.0, x)
    return h0_lo, pltpu.roll(h0_lo, HEAD_DIM, 1), pltpu.roll(h1_hi, HEAD_DIM, 1), h1_hi


def _place_rows(xt, head, half):
    rows = xt[head * HEAD_DIM:(head + 1) * HEAD_DIM]
    zero = jnp.zeros_like(rows)
    return jnp.concatenate([rows, zero] if half == 0 else [zero, rows], axis=0)


def _row_reduce(blocks, combine, lane_reduce):
    acc = None
    for s in blocks:
        for c in range(s.shape[1] // LANES):
            chunk = s[:, c * LANES:(c + 1) * LANES]
            acc = chunk if acc is None else combine(acc, chunk)
    return lane_reduce(acc, axis=-1, keepdims=True)


def _attend_heads(score_fns, values, sinks):
    scores = [fn() for fn in score_fns]
    probs = []
    for blocks, sink in zip(scores, sinks):
        m = _row_reduce(blocks, jnp.maximum, jnp.max)
        if sink is not None:
            m = jnp.maximum(m, sink)
        ps = [jnp.exp2(s - m) for s in blocks]
        l = _row_reduce(ps, jnp.add, jnp.sum)
        if sink is not None:
            l = l + jnp.exp2(sink - m)
        probs.append(([p.astype(BF16) for p in ps], 1.0 / l))
    outs = []
    for (ps, inv_l), vals in zip(probs, values):
        out = None
        for p, (v, v_fm) in zip(ps, vals):
            o = _dot_nt(p, v) if v_fm else _dot(p, v)
            out = o if out is None else out + o
        outs.append(out * inv_l)
    return outs


def _merge_head_pairs(outs):
    return jnp.concatenate([outs[i] + outs[i + 1] for i in range(0, len(outs), 2)], axis=1)


def _mod(mod_ref, row, i):
    return mod_ref[pl.ds(row, 1), i * D_MODEL:(i + 1) * D_MODEL]


def _tail(x, ya, yb, ga, gb, mod_ref, row, woa_ref, wob_ref, wo_ref, gf_ref, wg_ref, wu_ref, wd_ref):
    gt1, sh2, sc2, gt2 = (_mod(mod_ref, row, i) for i in (2, 3, 4, 5))
    ma = _dot(ya.astype(BF16), woa_ref[...])
    mb = _dot(yb.astype(BF16), wob_ref[...])
    mg = _sigmoid(ga) * ma + _sigmoid(gb) * mb
    x1 = x + gt1 * _dot(mg.astype(BF16), wo_ref[...])
    h2 = _rms_mod(x1, gf_ref[...], sh2, sc2).astype(BF16)
    gate = _dot(h2, wg_ref[...])
    up = _dot(h2, wu_ref[...])
    act = (gate * _sigmoid(gate)) * up
    return x1 + gt2 * _dot(act.astype(BF16), wd_ref[...])


ADA_TK = 64
ADA_SLOTS = 8


def _ada_kernel(cctx_ref, clat_ref, b_ref, w_hbm, o_ref, stage_ref, sem):
    n_chunks = w_hbm.shape[0] // ADA_TK

    def copy(k):
        return pltpu.make_async_copy(w_hbm.at[pl.ds(k * ADA_TK, ADA_TK)], stage_ref.at[k % ADA_SLOTS],
                                     sem.at[k % ADA_SLOTS])

    for k in range(min(ADA_SLOTS, n_chunks)):
        copy(k).start()
    row = lax.broadcasted_iota(jnp.int32, (N_MOD_ROWS, D_MODEL), 0)
    c = jnp.where(row == 0, cctx_ref[...], 0.0)
    for i in range(clat_ref.shape[0]):
        c = jnp.where(row == 1 + i, clat_ref[i:i + 1, :], c)
    s = c * _sigmoid(c)
    acc = jnp.broadcast_to(b_ref[...], o_ref.shape)
    for k in range(n_chunks):
        s_hi, s_lo = _split_bf16(s[:, k * ADA_TK:(k + 1) * ADA_TK])
        copy(k).wait()
        w_hi, w_lo = _split_bf16(stage_ref[k % ADA_SLOTS])
        if k + ADA_SLOTS < n_chunks:
            copy(k + ADA_SLOTS).start()
        r = _dot(jnp.concatenate([s_hi, s_lo], axis=0), w_hi)
        acc = acc + (r[:N_MOD_ROWS] + r[N_MOD_ROWS:] + _dot(s_hi, w_lo))
    o_ref[...] = acc


def _ada(c_ctx, c_lat, w, b):
    k, n = w.shape
    vmem = pl.BlockSpec(memory_space=pltpu.VMEM)
    return pl.pallas_call(
        _ada_kernel,
        in_specs=[vmem, vmem, vmem, pl.BlockSpec(memory_space=pl.ANY)],
        out_specs=vmem,
        out_shape=jax.ShapeDtypeStruct((N_MOD_ROWS, n), F32),
        scratch_shapes=[pltpu.VMEM((ADA_SLOTS, ADA_TK, n), F32), pltpu.SemaphoreType.DMA((ADA_SLOTS,))],
        name="ada_mod",
    )(c_ctx, c_lat, b, w)


W_CHUNK_ROWS = {D_IN: 32, D_MODEL: 128, D_FF: 64}
DMA_DEPTH = 6
N_WEIGHTS = 7
CTX_GROUP = 1


def _load_cast_all(srcs, dsts, stages, sems):
    tasks, ring_pos = [], {}
    for w, src in enumerate(srcs):
        n_rows, cols = src.shape
        rows = W_CHUNK_ROWS[cols]
        for r0 in range(0, n_rows, rows):
            pos = ring_pos.get(cols, 0)
            ring_pos[cols] = pos + 1
            tasks.append((w, r0, rows, cols, pos % DMA_DEPTH))

    def copy(task):
        w, r0, rows, cols, slot = task
        return pltpu.make_async_copy(srcs[w].at[pl.ds(r0, rows)], stages[cols].at[slot], sems[cols].at[slot])

    for task in tasks[:DMA_DEPTH]:
        copy(task).start()
    for i, task in enumerate(tasks):
        w, r0, rows, cols, slot = task
        copy(task).wait()
        dsts[w][pl.ds(r0, rows), :] = stages[cols][slot].astype(BF16)
        if i + DMA_DEPTH < len(tasks):
            copy(tasks[i + DMA_DEPTH]).start()


def _ctx_kernel(sink_ref, x_ref, mod_ref, gattn_ref, nrm_ref, gf_ref,
                win_hbm, woa_hbm, wob_hbm, wo_hbm, wg_hbm, wu_hbm, wd_hbm,
                y_ref, ka_ref, va_ref, kb_ref, vb_ref,
                win_out, woa_out, wob_out, wo_out, wg_out, wu_out, wd_out,
                win_ref, woa_ref, wob_ref, wo_ref, wg_ref, wu_ref, wd_ref, out_sem):
    srcs = (win_hbm, woa_hbm, wob_hbm, wo_hbm, wg_hbm, wu_hbm, wd_hbm)
    dsts = (win_ref, woa_ref, wob_ref, wo_ref, wg_ref, wu_ref, wd_ref)
    outs = (win_out, woa_out, wob_out, wo_out, wg_out, wu_out, wd_out)
    first = pl.program_id(0) == 0

    def out_copy(i):
        return pltpu.make_async_copy(dsts[i], outs[i], out_sem.at[i])

    @pl.when(first)
    def _():
        widths = tuple(W_CHUNK_ROWS)

        def scoped(*refs):
            stages = dict(zip(widths, refs[:len(widths)]))
            sems = dict(zip(widths, refs[len(widths):]))
            _load_cast_all(srcs, dsts, stages, sems)

        pl.run_scoped(scoped,
                      *[pltpu.VMEM((DMA_DEPTH, W_CHUNK_ROWS[c], c), F32) for c in widths],
                      *[pltpu.SemaphoreType.DMA((DMA_DEPTH,)) for _ in widths])
        for i in range(N_WEIGHTS):
            out_copy(i).start()

    x = x_ref[0]
    h = _rms_mod(x, gattn_ref[...], _mod(mod_ref, 0, 0), _mod(mod_ref, 0, 1)).astype(BF16)
    qa, ka, va, qb, kb, vb, ga, gb = _project(h, win_ref, nrm_ref)
    ka_ref[0] = ka.T
    va_ref[0] = va.T
    kb_ref[0] = kb.T
    vb_ref[0] = vb.T

    lo_mask = lax.broadcasted_iota(jnp.int32, (TOK, LANES), 1) < HEAD_DIM
    scale = HEAD_DIM ** -0.5 * LOG2E
    qa = (qa * scale).astype(BF16)
    qb = (qb * scale).astype(BF16)

    k_pl = [t.astype(BF16) for t in _place_halves(ka, lo_mask)]
    v_pl = [t.astype(BF16) for t in _place_halves(va, lo_mask)]
    ya = []
    for g in range(0, A_HEADS, CTX_GROUP):
        score_fns, values, sinks = [], [], []
        for head in range(g, g + CTX_GROUP):
            c, half = divmod(head, 2)
            i = 2 * (c // 2) + half
            score_fns.append(lambda c=c, i=i: [_dot_nt(qa[:, c * LANES:(c + 1) * LANES], k_pl[i])])
            values.append([(v_pl[i], False)])
            sinks.append(sink_ref[head] * LOG2E)
        ya += _attend_heads(score_fns, values, sinks)
    ya = _merge_head_pairs(ya)

    yb = []
    for g in range(0, B_HEADS, CTX_GROUP):
        score_fns, values = [], []
        for head in range(g, g + CTX_GROUP):
            c, half = divmod(head, 2)
            sl = slice(c * LANES, (c + 1) * LANES)
            hm = lo_mask if half == 0 else jnp.logical_not(lo_mask)
            km = jnp.where(hm, kb[:, sl], 0.0).astype(BF16)
            vm = jnp.where(hm, vb[:, sl], 0.0).astype(BF16)
            score_fns.append(lambda sl=sl, km=km: [_dot_nt(qb[:, sl], km)])
            values.append([(vm, False)])
        yb += _attend_heads(score_fns, values, [None] * CTX_GROUP)
    yb = _merge_head_pairs(yb)

    y_ref[0] = _tail(x, ya, yb, ga, gb, mod_ref, 0, woa_ref, wob_ref, wo_ref, gf_ref, wg_ref, wu_ref, wd_ref)

    @pl.when(first)
    def _():
        for i in range(N_WEIGHTS):
            out_copy(i).wait()


def _const_spec(shape):
    nd = len(shape)
    return pl.BlockSpec(shape, lambda *_: (0,) * nd, pipeline_mode=pl.Buffered(1))


def _ctx_layer(x, mods, sink, gattn, nrm, gf, weights):
    nb = x.shape[0]
    tok_spec = lambda w: pl.BlockSpec((1, SEQ, w), lambda b: (b, 0, 0))
    fm_spec = lambda w: pl.BlockSpec((1, w, SEQ), lambda b: (b, 0, 0))
    any_spec = pl.BlockSpec(memory_space=pl.ANY)
    in_specs = [
        pl.BlockSpec(memory_space=pltpu.SMEM),
        tok_spec(D_MODEL),
        _const_spec(mods.shape), _const_spec(gattn.shape), _const_spec(nrm.shape), _const_spec(gf.shape),
    ] + [any_spec] * len(weights)
    out_specs = ([tok_spec(D_MODEL), fm_spec(A_KV), fm_spec(A_KV), fm_spec(B_W), fm_spec(B_W)]
                 + [any_spec] * len(weights))
    out_shape = ([jax.ShapeDtypeStruct((nb, SEQ, D_MODEL), F32)]
                 + [jax.ShapeDtypeStruct((nb, w, SEQ), F32) for w in (A_KV, A_KV, B_W, B_W)]
                 + [jax.ShapeDtypeStruct(w.shape, BF16) for w in weights])
    scratch = [pltpu.VMEM(w.shape, BF16) for w in weights] + [pltpu.SemaphoreType.DMA((len(weights),))]
    return pl.pallas_call(
        _ctx_kernel,
        grid=(nb,),
        in_specs=in_specs,
        out_specs=out_specs,
        out_shape=out_shape,
        scratch_shapes=scratch,
        compiler_params=pltpu.CompilerParams(dimension_semantics=("arbitrary",), vmem_limit_bytes=VMEM_LIMIT),
        name="ctx_layer",
    )(sink, x, mods, gattn, nrm, gf, *weights)


ROPE_HALF = HEAD_DIM // 4


def _rope(x, cos, sin, upper_half):
    swapped = jnp.where(upper_half, pltpu.roll(x, ROPE_HALF, 1), pltpu.roll(x, LANES - ROPE_HALF, 1))
    return x * cos + swapped * sin


def _lat_proj_kernel(x_ref, mod_ref, gattn_ref, win_ref, nrm_ref, cos_ref, sin_ref,
                     qa_ref, ka_ref, va_ref, qb_ref, kb_ref, vb_ref, ga_ref, gb_ref):
    row = 1 + pl.program_id(0)
    x = x_ref[0]
    h = _rms_mod(x, gattn_ref[...], _mod(mod_ref, row, 0), _mod(mod_ref, row, 1)).astype(BF16)
    qa, ka, va, qb, kb, vb, ga, gb = _project(h, win_ref, nrm_ref)
    lane = lax.broadcasted_iota(jnp.int32, (TOK, LANES), 1)
    lo_mask = lane < HEAD_DIM
    upper_half = (lane & ROPE_HALF) != 0
    cos, sin = cos_ref[...], sin_ref[...]
    scale = HEAD_DIM ** -0.5 * LOG2E
    for c in range(A_Q // LANES):
        sl = slice(c * LANES, (c + 1) * LANES)
        qa_ref[0, :, sl] = (_rope(qa[:, sl], cos, sin, upper_half) * scale).astype(BF16)
    ka = _rope(ka, cos, sin, upper_half)
    for i, t in enumerate(_place_halves(ka, lo_mask)):
        ka_ref[0, :, i * LANES:(i + 1) * LANES] = t.astype(BF16)
    for i, t in enumerate(_place_halves(va, lo_mask)):
        va_ref[0, :, i * LANES:(i + 1) * LANES] = t.astype(BF16)
    qb_ref[0] = (qb * scale).astype(BF16)
    kb_ref[0] = kb.astype(BF16)
    vb_ref[0] = vb.astype(BF16)
    ga_ref[0] = ga
    gb_ref[0] = gb


def _lat_proj(x, mods, gattn, win, nrm, cos, sin):
    nb, n, _ = x.shape
    nt = n // TOK
    tok_spec = lambda w: pl.BlockSpec((1, TOK, w), lambda b, t: (b, t, 0))
    in_specs = [
        tok_spec(D_MODEL),
        _const_spec(mods.shape), _const_spec(gattn.shape), _const_spec(win.shape), _const_spec(nrm.shape),
        pl.BlockSpec((TOK, LANES), lambda b, t: (t, 0)),
        pl.BlockSpec((TOK, LANES), lambda b, t: (t, 0)),
    ]
    widths = (A_Q, 4 * LANES, 4 * LANES, B_W, B_W, B_W, D_MODEL, D_MODEL)
    dtypes = (BF16,) * 6 + (F32, F32)
    return pl.pallas_call(
        _lat_proj_kernel,
        grid=(nb, nt),
        in_specs=in_specs,
        out_specs=[tok_spec(w) for w in widths],
        out_shape=[jax.ShapeDtypeStruct((nb, n, w), dt) for w, dt in zip(widths, dtypes)],
        compiler_params=pltpu.CompilerParams(dimension_semantics=("arbitrary", "arbitrary"),
                                             vmem_limit_bytes=VMEM_LIMIT),
        name="lat_proj",
    )(x, mods, gattn, win, nrm, cos, sin)


def _build_na_bias(row_ref, bt_ref):
    w = lax.broadcasted_iota(jnp.int32, (GRID_W, LANES), 0)
    kcol = lax.broadcasted_iota(jnp.int32, (GRID_W, LANES), 1) & (GRID_W - 1)
    cs = jnp.clip(w - NA_COLS // 2, 0, GRID_W - NA_COLS)
    valid = (kcol >= cs) & (kcol < cs + NA_COLS)
    for h in range(B_HEADS):
        for d in range(2 * NA_ROWS - 1):
            rows = jnp.broadcast_to(row_ref[h, d:d + 1, :] * LOG2E, (GRID_W, LANES))
            t = pltpu.roll(rows, LANES - (NA_COLS - 1), 1, stride=1, stride_axis=0)
            bt_ref[h, d] = jnp.where(valid, t, NEG)


A_KEYS = 2 * TOK
B_KROWS = 12
B_KEYS = B_KROWS * GRID_W
A_GROUP, B_GROUP = 8, 2
N_TAIL_WEIGHTS = 6


def _lat_attn_kernel(sink_ref, x_ref, mod_ref, qa_ref, ka_ref, va_ref, qb_ref, kb_ref, vb_ref,
                     ga_ref, gb_ref, cka_ref, cva_ref, ckb_ref, cvb_ref, rpb_ref, gf_ref,
                     woa_hbm, wob_hbm, wo_hbm, wg_hbm, wu_hbm, wd_hbm, y_ref,
                     bt_ref, woa_ref, wob_ref, wo_ref, wg_ref, wu_ref, wd_ref, w_sem):
    t = pl.program_id(1)
    row = 1 + pl.program_id(0)
    first = (pl.program_id(0) == 0) & (t == 0)
    x = x_ref[0]
    w_hbm = (woa_hbm, wob_hbm, wo_hbm, wg_hbm, wu_hbm, wd_hbm)
    w_vmem = (woa_ref, wob_ref, wo_ref, wg_ref, wu_ref, wd_ref)

    def w_copy(k):
        return pltpu.make_async_copy(w_hbm[k], w_vmem[k], w_sem.at[k])

    @pl.when(first)
    def _():
        for k in range(N_TAIL_WEIGHTS):
            w_copy(k).start()
        _build_na_bias(rpb_ref, bt_ref)

    ks = pl.multiple_of(jnp.clip(TOK * t - A_WINDOW, 0, DEC_SEQ - A_KEYS), LANES)
    qpos = TOK * t + lax.broadcasted_iota(jnp.int32, (TOK, A_KEYS), 0)
    kpos = ks + lax.broadcasted_iota(jnp.int32, (TOK, A_KEYS), 1)
    a_valid = jnp.abs(qpos - kpos) <= A_WINDOW
    ck_t, cv_t = cka_ref[0], cva_ref[0]
    ck_pl = [_place_rows(ck_t, i // 2, i % 2).astype(BF16) for i in range(4)]
    cv_pl = [_place_rows(cv_t, i // 2, i % 2).astype(BF16) for i in range(4)]
    ya = []
    for g in range(0, A_HEADS, A_GROUP):
        score_fns, values, sinks = [], [], []
        for head in range(g, g + A_GROUP):
            c, half = divmod(head, 2)
            i = 2 * (c // 2) + half

            def a_scores(c=c, i=i):
                qc = qa_ref[0, :, c * LANES:(c + 1) * LANES]
                k_loc = ka_ref[0, pl.ds(ks, A_KEYS), i * LANES:(i + 1) * LANES]
                return [jnp.where(a_valid, _dot_nt(qc, k_loc), NEG), _dot(qc, ck_pl[i])]

            score_fns.append(a_scores)
            values.append([(va_ref[0, pl.ds(ks, A_KEYS), i * LANES:(i + 1) * LANES], False), (cv_pl[i], True)])
            sinks.append(sink_ref[head] * LOG2E)
        ya.append(_merge_head_pairs(_attend_heads(score_fns, values, sinks)))
    ya = jnp.concatenate(ya, axis=1)

    kr0 = jnp.clip(4 * t - 4, 0, DEC_SEQ // GRID_W - B_KROWS)
    kst = pl.multiple_of(kr0 * GRID_W, TOK)
    lane_lo = lax.broadcasted_iota(jnp.int32, (1, LANES), 1) < HEAD_DIM
    b_dr, b_valid = [], []
    for qi in range(TOK // GRID_W):
        r = 4 * t + qi
        rs = jnp.clip(r - NA_ROWS // 2, 0, DEC_SEQ // GRID_W - NA_ROWS)
        drs, valids = [], []
        for p in range(B_KROWS // 2):
            kr = kr0 + 2 * p
            drs.append(jnp.clip(kr - r + (NA_ROWS - 1), 0, 2 * NA_ROWS - 2))
            ok = [((k >= rs) & (k < rs + NA_ROWS)).astype(jnp.int32) for k in (kr, kr + 1)]
            valids.append(jnp.where(lane_lo, ok[0], ok[1]) != 0)
        b_dr.append(drs)
        b_valid.append(valids)
    lo_k = lax.broadcasted_iota(jnp.int32, (B_KEYS, LANES), 1) < HEAD_DIM
    lo_f = lax.broadcasted_iota(jnp.int32, (LANES, PAST_LEN), 0) < HEAD_DIM
    zero = jnp.zeros((), BF16)
    yb = []
    for g in range(0, B_HEADS, B_GROUP):
        score_fns, values = [], []
        for head in range(g, g + B_GROUP):
            c, half = divmod(head, 2)
            sl = slice(c * LANES, (c + 1) * LANES)
            hm_k = lo_k if half == 0 else jnp.logical_not(lo_k)
            hm_f = lo_f if half == 0 else jnp.logical_not(lo_f)

            def b_scores(head=head, sl=sl, hm_k=hm_k, hm_f=hm_f):
                qc = qb_ref[0, :, sl]
                kk = jnp.where(hm_k, kb_ref[0, pl.ds(kst, B_KEYS), sl], zero)
                kc = jnp.where(hm_f, ckb_ref[0, sl, :], 0.0).astype(BF16)
                s = _dot_nt(qc, kk)
                rows = []
                for qi in range(TOK // GRID_W):
                    blocks = []
                    for p in range(B_KROWS // 2):
                        blk = s[qi * GRID_W:(qi + 1) * GRID_W, p * LANES:(p + 1) * LANES] + bt_ref[head, b_dr[qi][p]]
                        blocks.append(jnp.where(b_valid[qi][p], blk, NEG))
                    rows.append(jnp.concatenate(blocks, axis=1))
                return [jnp.concatenate(rows, axis=0), _dot(qc, kc)]

            score_fns.append(b_scores)
            vv = jnp.where(hm_k, vb_ref[0, pl.ds(kst, B_KEYS), sl], zero)
            vc = jnp.where(hm_f, cvb_ref[0, sl, :], 0.0).astype(BF16)
            values.append([(vv, False), (vc, True)])
        yb.append(_merge_head_pairs(_attend_heads(score_fns, values, [None] * B_GROUP)))
    yb = jnp.concatenate(yb, axis=1)

    @pl.when(first)
    def _():
        for k in range(N_TAIL_WEIGHTS):
            w_copy(k).wait()

    y_ref[0] = _tail(x, ya, yb, ga_ref[0], gb_ref[0], mod_ref, row, woa_ref, wob_ref, wo_ref, gf_ref,
                     wg_ref, wu_ref, wd_ref)


def _lat_attn(sink, x, mods, qa, ka4, va4, qb, kb, vb, ga, gb, cka, cva, ckb, cvb, rpb_rows, gf, weights):
    nb, n, _ = x.shape
    nt = n // TOK
    tok_spec = lambda w: pl.BlockSpec((1, TOK, w), lambda b, t: (b, t, 0))
    seq_spec = lambda rows, w: pl.BlockSpec((1, rows, w), lambda b, t: (b, 0, 0))
    in_specs = [
        pl.BlockSpec(memory_space=pltpu.SMEM),
        tok_spec(D_MODEL),
        _const_spec(mods.shape),
        tok_spec(A_Q), seq_spec(n, 4 * LANES), seq_spec(n, 4 * LANES),
        tok_spec(B_W), seq_spec(n, B_W), seq_spec(n, B_W),
        tok_spec(D_MODEL), tok_spec(D_MODEL),
        seq_spec(A_KV, PAST_LEN), seq_spec(A_KV, PAST_LEN), seq_spec(B_W, PAST_LEN), seq_spec(B_W, PAST_LEN),
        _const_spec(rpb_rows.shape), _const_spec(gf.shape),
    ] + [pl.BlockSpec(memory_space=pl.ANY)] * len(weights)
    scratch = ([pltpu.VMEM((B_HEADS, 2 * NA_ROWS - 1, GRID_W, LANES), F32)]
               + [pltpu.VMEM(w.shape, BF16) for w in weights] + [pltpu.SemaphoreType.DMA((len(weights),))])
    return pl.pallas_call(
        _lat_attn_kernel,
        grid=(nb, nt),
        in_specs=in_specs,
        out_specs=tok_spec(D_MODEL),
        out_shape=jax.ShapeDtypeStruct((nb, n, D_MODEL), F32),
        scratch_shapes=scratch,
        compiler_params=pltpu.CompilerParams(dimension_semantics=("arbitrary", "arbitrary"),
                                             vmem_limit_bytes=VMEM_LIMIT),
        name="lat_attn",
    )(sink, x, mods, qa, ka4, va4, qb, kb, vb, ga, gb, cka, cva, ckb, cvb, rpb_rows, gf, *weights)


def _rope_tables(n):
    inv_freq = 1.0 / (ROPE_BASE ** (np.arange(ROPE_HALF, dtype=np.float64) / ROPE_HALF))
    t = np.arange(n)
    parts_c, parts_s = [], []
    for pos in (t // GRID_W, t % GRID_W):
        ang = pos.astype(np.float64)[:, None] * inv_freq[None, :]
        c, s = np.cos(ang), np.sin(ang)
        parts_c += [c, c]
        parts_s += [-s, s]
    cos = np.concatenate(parts_c * 2, axis=1).astype(np.float32)
    sin = np.concatenate(parts_s * 2, axis=1).astype(np.float32)
    return jnp.asarray(cos), jnp.asarray(sin)


def _feature_major(cache, l):
    b, _, s, h, d = cache.shape
    return jnp.transpose(cache[:, l], (0, 2, 3, 1)).reshape(b, h * d, s)


def _token_major(x, heads):
    b, _, s = x.shape
    return jnp.transpose(x.reshape(b, 1, heads, HEAD_DIM, s), (0, 1, 4, 2, 3))


def kernel(x_prompt, x_sample, cache_a_k, cache_a_v, cache_b_k, cache_b_v, c, c_ctx, w_ada, b_ada, g_attn, w_in, q_norm_a, k_norm_a, q_norm_b, k_norm_b, sink_a, rpb_b, w_out_a, w_out_b, w_o, g_ffn, w_gate, w_up, w_down):
    nd, n, _ = x_sample.shape
    assert w_in.shape == (1, D_MODEL, D_IN), "single-layer trunk with the stated widths only"
    assert x_prompt.shape[1:] == (SEQ, D_MODEL) and x_sample.shape[1:] == (DEC_SEQ, D_MODEL)
    assert cache_a_k.shape[2] == PAST_LEN and nd <= N_MOD_ROWS - 1
    l = 0

    mods = _ada(c_ctx[None, :], c, w_ada[l], b_ada[l][None, :])

    gattn, gf = g_attn[l][None, :], g_ffn[l][None, :]
    nrm = jnp.stack([jnp.tile(v[l], A_Q // HEAD_DIM) for v in (q_norm_a, k_norm_a, q_norm_b, k_norm_b)])
    sink = sink_a[l]
    weights = (w_in[l], w_out_a[l], w_out_b[l], w_o[l], w_gate[l], w_up[l], w_down[l])

    (y_prompt, ka, va, kb, vb, win, woa, wob, wo, wg, wu, wd) = _ctx_layer(
        x_prompt, mods, sink, gattn, nrm, gf, weights)
    new_a_k = _token_major(ka, A_KV_HEADS)
    new_a_v = _token_major(va, A_KV_HEADS)
    new_b_k = _token_major(kb, B_HEADS)
    new_b_v = _token_major(vb, B_HEADS)

    cos, sin = _rope_tables(n)
    qa, ka4, va4, qb, lkb, lvb, ga, gb = _lat_proj(x_sample, mods, gattn, win, nrm, cos, sin)

    rp = jnp.pad(rpb_b[l], ((0, 0), (0, 1), (0, HEAD_DIM - (2 * NA_COLS - 1))))
    rpb_rows = jnp.concatenate([rp[:, :-1], rp[:, 1:]], axis=-1)

    y_sample = _lat_attn(sink, x_sample, mods, qa, ka4, va4, qb, lkb, lvb, ga, gb,
                         _feature_major(cache_a_k, l), _feature_major(cache_a_v, l),
                         _feature_major(cache_b_k, l), _feature_major(cache_b_v, l),
                         rpb_rows, gf, (woa, wob, wo, wg, wu, wd))
    return (y_prompt, y_sample, new_a_k, new_a_v, new_b_k, new_b_v)
```
